```python
import math
import jax
import jax.numpy as jnp
from jax import lax
import numpy as np

D_MODEL = 1024
BATCH = 1
SEQ = 16384
DEPTH = 4

GRID_W = 64
CTX_LEN = 256
N_MIXERS = 3
N_HEADS = 16
N_KV_HEADS = 4
HEAD_DIM = 64
Q_PER_KV = N_HEADS // N_KV_HEADS
WINDOW = 128
ATTN_BLOCK = 128
ATTN_SCALE = HEAD_DIM ** -0.5
ROPE_BASE = 10000.0
ROPE_FREQS = HEAD_DIM // 4
CHUNK = 128
GMLP_WIDTH = D_MODEL
GMLP_GROUPS = 8
GMLP_GROUP_DIM = GMLP_WIDTH // GMLP_GROUPS
CONV_WIDTH = 3
N_EXPERTS = 16
N_EXPERT_GROUPS = 4
EXPERTS_PER_GROUP = N_EXPERTS // N_EXPERT_GROUPS
TOP_K = 2
D_EXPERT = 512
MOE_BLOCK = 128
ALPHA = (2 * DEPTH) ** 0.25
BETA = (8 * DEPTH) ** -0.25
LN_EPS = 1e-5

kernel_name = "hybrid_dit_interleaved_moe"


def layer_norm(x, g, b):
    xf = x.astype(jnp.float32)
    mu = jnp.mean(xf, axis=-1, keepdims=True)
    var = jnp.mean(jnp.square(xf - mu), axis=-1, keepdims=True)
    y = (xf - mu) * lax.rsqrt(var + LN_EPS) * g.astype(jnp.float32) + b.astype(jnp.float32)
    return y.astype(x.dtype)


def modulate(h, shift, scale):
    return h * (1 + scale) + shift


def axial_rope_tables(L, dtype):
    rows = L // GRID_W
    row = jnp.repeat(jnp.arange(rows), GRID_W).astype(jnp.float32)
    col = jnp.tile(jnp.arange(GRID_W), rows).astype(jnp.float32)
    freqs = jnp.power(ROPE_BASE, -jnp.arange(ROPE_FREQS, dtype=jnp.float32) / ROPE_FREQS)
    ar = row[:, None] * freqs[None, :]
    ac = col[:, None] * freqs[None, :]
    ang = jnp.concatenate([ar, ar, ac, ac], axis=-1)
    return jnp.cos(ang).astype(dtype), jnp.sin(ang).astype(dtype)


def apply_axial_rope(x, cos, sin):
    L = x.shape[1]
    bshape = (1, L) + (1,) * (x.ndim - 3) + (HEAD_DIM,)
    xr = x.reshape(x.shape[:-1] + (2, 2, ROPE_FREQS))
    rot = jnp.stack([-xr[..., 1, :], xr[..., 0, :]], axis=-2).reshape(x.shape)
    return x * cos.reshape(bshape) + rot * sin.reshape(bshape)


def qkv_project(h, w_qkv, with_q):
    B, L, _ = h.shape
    qd = N_HEADS * HEAD_DIM
    kd = N_KV_HEADS * HEAD_DIM
    if with_q:
        y = h @ w_qkv
        q = y[..., :qd].reshape(B, L, N_KV_HEADS, Q_PER_KV, HEAD_DIM)
        kv = y[..., qd:]
    else:
        q = None
        kv = h @ w_qkv[:, qd:]
    k = kv[..., :kd].reshape(B, L, N_KV_HEADS, HEAD_DIM)
    v = kv[..., kd:].reshape(B, L, N_KV_HEADS, HEAD_DIM)
    return q, k, v


def windowed_attention(q, k, v, kc, vc, sink):
    B, L = q.shape[:2]
    nb = L // ATTN_BLOCK
    n_loc = 3 * ATTN_BLOCK
    n_ctx = kc.shape[1]
    pad = ((0, 0), (ATTN_BLOCK, ATTN_BLOCK), (0, 0), (0, 0))
    kp = jnp.pad(k, pad)
    vp = jnp.pad(v, pad)
    qb = jnp.moveaxis(q.reshape(B, nb, ATTN_BLOCK, N_KV_HEADS, Q_PER_KV, HEAD_DIM), 1, 0)
    sink_col = jnp.broadcast_to(sink[None, :, :, None, None], (B, N_KV_HEADS, Q_PER_KV, ATTN_BLOCK, 1))
    offs_q = jnp.arange(ATTN_BLOCK)
    offs_k = jnp.arange(n_loc)

    def block(args):
        qblk, n = args
        kblk = lax.dynamic_slice_in_dim(kp, n * ATTN_BLOCK, n_loc, axis=1)
        vblk = lax.dynamic_slice_in_dim(vp, n * ATTN_BLOCK, n_loc, axis=1)
        qpos = n * ATTN_BLOCK + offs_q
        kpos = (n - 1) * ATTN_BLOCK + offs_k
        valid = ((jnp.abs(kpos[None, :] - qpos[:, None]) <= WINDOW)
                 & (kpos >= 0)[None, :] & (kpos < L)[None, :])
        s_loc = jnp.einsum("bqkgd,bskd->bkgqs", qblk, kblk, preferred_element_type=jnp.float32) * ATTN_SCALE
        s_loc = jnp.where(valid, s_loc, -jnp.inf)
        s_ctx = jnp.einsum("bqkgd,bckd->bkgqc", qblk, kc, preferred_element_type=jnp.float32) * ATTN_SCALE
        p = jax.nn.softmax(jnp.concatenate([s_loc, s_ctx, sink_col], axis=-1), axis=-1).astype(v.dtype)
        o = (jnp.einsum("bkgqs,bskd->bqkgd", p[..., :n_loc], vblk)
             + jnp.einsum("bkgqc,bckd->bqkgd", p[..., n_loc:n_loc + n_ctx], vc))
        return o

    out = lax.map(block, (qb, jnp.arange(nb)))
    return jnp.moveaxis(out, 0, 1).reshape(B, L, N_HEADS * HEAD_DIM)


def context_attention(qc, kc, vc, sink):
    B, C = qc.shape[:2]
    s = jnp.einsum("bqkgd,bckd->bkgqc", qc, kc, preferred_element_type=jnp.float32) * ATTN_SCALE
    sink_col = jnp.broadcast_to(sink[None, :, :, None, None], (B, N_KV_HEADS, Q_PER_KV, C, 1))
    p = jax.nn.softmax(jnp.concatenate([s, sink_col], axis=-1), axis=-1).astype(vc.dtype)
    o = jnp.einsum("bkgqc,bckd->bqkgd", p[..., :C], vc)
    return o.reshape(B, C, N_HEADS * HEAD_DIM)


def attention_mixer(hc, hl, w_qkv, w_o, sink, cos, sin, want_ctx):
    ql, kl, vl = qkv_project(hl, w_qkv, True)
    ql = apply_axial_rope(ql, cos, sin)
    kl = apply_axial_rope(kl, cos, sin)
    qc, kc, vc = qkv_project(hc, w_qkv, want_ctx)
    sink_f = sink.astype(jnp.float32).reshape(N_KV_HEADS, Q_PER_KV)
    yl = windowed_attention(ql, kl, vl, kc, vc, sink_f) @ w_o
    yc = context_attention(qc, kc, vc, sink_f) @ w_o if want_ctx else None
    return yc, yl


def chunk_gmlp(h, w_in, b_in, ln_g, ln_b, w_s, b_s, w_out):
    B, L, _ = h.shape
    n = L // CHUNK
    z = jax.nn.gelu(h @ w_in + b_in)
    u, v = z[..., :GMLP_WIDTH], z[..., GMLP_WIDTH:]
    v = layer_norm(v, ln_g, ln_b).reshape(B, n, CHUNK, GMLP_GROUPS, GMLP_GROUP_DIM)
    s = jnp.einsum("gpq,bnqgc->bnpgc", w_s, v) + b_s[None, None, :, :, None]
    return (u * s.reshape(B, L, GMLP_WIDTH)) @ w_out


def short_gated_conv(h, w_in, w_conv, w_out):
    D = h.shape[-1]
    proj = h @ w_in
    gb, gc, z = proj[..., :D], proj[..., D:2 * D], proj[..., 2 * D:]
    z = gc * z
    zp = jnp.pad(z, ((0, 0), (1, 1), (0, 0)))
    zc = w_conv[0] * zp[:, :-2] + w_conv[1] * zp[:, 1:-1] + w_conv[2] * zp[:, 2:]
    return (gb * zc) @ w_out


def routed_moe(h, router_w, router_bias, w1, w3, w2):
    T, D = h.shape
    s = jax.nn.sigmoid(jnp.dot(h, router_w, preferred_element_type=jnp.float32))
    s_sel = s + router_bias.astype(jnp.float32)
    grp_score = lax.top_k(s_sel.reshape(T, N_EXPERT_GROUPS, EXPERTS_PER_GROUP), TOP_K)[0].sum(-1)
    g = jnp.argmax(grp_score, axis=-1)
    in_group = (jnp.arange(N_EXPERTS) // EXPERTS_PER_GROUP)[None, :] == g[:, None]
    _, idx = lax.top_k(jnp.where(in_group, s_sel, -jnp.inf), TOP_K)
    sc = jnp.take_along_axis(s, idx, axis=-1)
    gate = sc / jnp.sum(sc, axis=-1, keepdims=True)
    N = T * TOP_K
    e_flat = idx.reshape(N).astype(jnp.int32)
    tok = jnp.repeat(jnp.arange(T, dtype=jnp.int32), TOP_K)
    counts = jnp.bincount(e_flat, length=N_EXPERTS)
    padded = (counts + MOE_BLOCK - 1) // MOE_BLOCK * MOE_BLOCK
    pad_end = jnp.cumsum(padded)
    pad_start = pad_end - padded
    start = jnp.cumsum(counts) - counts
    order = jnp.argsort(e_flat, stable=True)
    e_sorted = e_flat[order]
    dest = pad_start[e_sorted] + jnp.arange(N, dtype=jnp.int32) - start[e_sorted]
    n_blocks = -(-N // MOE_BLOCK) + N_EXPERTS
    P = n_blocks * MOE_BLOCK
    row_tok = jnp.full((P,), T, jnp.int32).at[dest].set(tok[order])
    row_gate = jnp.zeros((P,), jnp.float32).at[dest].set(gate.reshape(N)[order])
    blk_expert = jnp.minimum(
        jnp.searchsorted(pad_end, jnp.arange(n_blocks, dtype=jnp.int32) * MOE_BLOCK, side="right"),
        N_EXPERTS - 1)
    h_pad = jnp.concatenate([h, jnp.zeros((1, D), h.dtype)], axis=0)
    xs = h_pad[row_tok].reshape(n_blocks, MOE_BLOCK, D)

    def expert_block(args):
        xb, e = args
        return (jax.nn.silu(xb @ w1[e]) * (xb @ w3[e])) @ w2[e]

    ys = lax.map(expert_block, (xs, blk_expert)).reshape(P, D)
    out = jax.ops.segment_sum(ys * row_gate[:, None].astype(ys.dtype), row_tok, num_segments=T + 1)
    return out[:T]


def setup_inputs(seed: int = 0) -> dict:
    key = jax.random.key(seed)
    keys = jax.random.split(key, 32)
    kgen = (keys[i] for i in range(32))
    D = D_MODEL

    def nrm(shape, scale):
        return jax.random.normal(next(kgen), shape, jnp.float32) * scale

    n_a = len(range(0, DEPTH, N_MIXERS))
    n_b = len(range(1, DEPTH, N_MIXERS))
    n_c = len(range(2, DEPTH, N_MIXERS))
    qkv_width = (N_HEADS + 2 * N_KV_HEADS) * HEAD_DIM
    return {
        "x": nrm((BATCH, SEQ, D), 1.0),
        "c": nrm((BATCH, D), 1.0),
        "ctx": nrm((BATCH, CTX_LEN, D), 1.0),
        "c_ctx": nrm((D,), 1.0),
        "w_mod": nrm((DEPTH, D, 6 * D), 0.5 * D ** -0.5),
        "b_mod": nrm((DEPTH, 6 * D), 0.02),
        "ln1_g": 1.0 + nrm((DEPTH, D), 0.02),
        "ln1_b": nrm((DEPTH, D), 0.02),
        "ln2_g": 1.0 + nrm((DEPTH, D), 0.02),
        "ln2_b": nrm((DEPTH, D), 0.02),
        "router_w": nrm((D, N_EXPERTS), D ** -0.5),
        "router_bias": nrm((N_EXPERTS,), 0.01),
        "moe_w1": nrm((DEPTH, N_EXPERTS, D, D_EXPERT), D ** -0.5),
        "moe_w3": nrm((DEPTH, N_EXPERTS, D, D_EXPERT), D ** -0.5),
        "moe_w2": nrm((DEPTH, N_EXPERTS, D_EXPERT, D), BETA * D_EXPERT ** -0.5),
        "a_w_qkv": nrm((n_a, D, qkv_width), D ** -0.5),
        "a_w_o": nrm((n_a, N_HEADS * HEAD_DIM, D), BETA * (N_HEADS * HEAD_DIM) ** -0.5),
        "a_sink": nrm((n_a, N_HEADS), 0.5),
        "b_w_in": nrm((n_b, D, 2 * GMLP_WIDTH), D ** -0.5),
        "b_b_in": nrm((n_b, 2 * GMLP_WIDTH), 0.02),
        "b_ln_g": 1.0 + nrm((n_b, GMLP_WIDTH), 0.02),
        "b_ln_b": nrm((n_b, GMLP_WIDTH), 0.02),
        "b_w_s": nrm((n_b, GMLP_GROUPS, CHUNK, CHUNK), 0.5 * CHUNK ** -0.5),
        "b_b_s": 1.0 + nrm((n_b, CHUNK, GMLP_GROUPS), 0.02),
        "b_w_out": nrm((n_b, GMLP_WIDTH, D), BETA * GMLP_WIDTH ** -0.5),
        "c_w_in": nrm((n_c, D, 3 * D), D ** -0.5),
        "c_w_conv": nrm((n_c, CONV_WIDTH, D), CONV_WIDTH ** -0.5),
        "c_w_out": nrm((n_c, D, D), BETA * D ** -0.5),
    }


def reference(x, c, ctx, c_ctx, w_mod, b_mod, ln1_g, ln1_b, ln2_g, ln2_b,
              router_w, router_bias, moe_w1, moe_w3, moe_w2,
              a_w_qkv, a_w_o, a_sink,
              b_w_in, b_b_in, b_ln_g, b_ln_b, b_w_s, b_b_s, b_w_out,
              c_w_in, c_w_conv, c_w_out):
    B, L, D = x.shape
    C = ctx.shape[1]
    cos, sin = axial_rope_tables(L, x.dtype)
    silu_c = jax.nn.silu(c)
    silu_cc = jax.nn.silu(c_ctx)
    for i in range(DEPTH):
        kind, j = i % N_MIXERS, i // N_MIXERS
        want_ctx = i < DEPTH - 1
        mod_l = (silu_c @ w_mod[i] + b_mod[i])[:, None, :]
        mod_c = (silu_cc @ w_mod[i] + b_mod[i])[None, None, :]
        sh_a, sc_a, g_a, sh_f, sc_f, g_f = jnp.split(mod_l, 6, axis=-1)
        csh_a, csc_a, cg_a, csh_f, csc_f, cg_f = jnp.split(mod_c, 6, axis=-1)
        hl = modulate(x, sh_a, sc_a)
        hc = modulate(ctx, csh_a, csc_a)
        if kind == 0:
            yc, yl = attention_mixer(hc, hl, a_w_qkv[j], a_w_o[j], a_sink[j], cos, sin, want_ctx)
        elif kind == 1:
            gm = (b_w_in[j], b_b_in[j], b_ln_g[j], b_ln_b[j], b_w_s[j], b_b_s[j], b_w_out[j])
            yl = chunk_gmlp(hl, *gm)
            yc = chunk_gmlp(hc, *gm) if want_ctx else None
        else:
            sc_w = (c_w_in[j], c_w_conv[j], c_w_out[j])
            yl = short_gated_conv(hl, *sc_w)
            yc = short_gated_conv(hc, *sc_w) if want_ctx else None
        x = layer_norm(ALPHA * x + g_a * yl, ln1_g[i], ln1_b[i])
        if want_ctx:
            ctx = layer_norm(ALPHA * ctx + cg_a * yc, ln1_g[i], ln1_b[i])
        hl = modulate(x, sh_f, sc_f)
        if want_ctx:
            hc = modulate(ctx, csh_f, csc_f)
            tokens = jnp.concatenate([hc.reshape(-1, D), hl.reshape(-1, D)], axis=0)
        else:
            tokens = hl.reshape(-1, D)
        y = routed_moe(tokens, router_w, router_bias, moe_w1[i], moe_w3[i], moe_w2[i])
        x = layer_norm(ALPHA * x + g_f * y[-B * L:].reshape(B, L, D), ln2_g[i], ln2_b[i])
        if want_ctx:
            ctx = layer_norm(ALPHA * ctx + cg_f * y[:B * C].reshape(B, C, D), ln2_g[i], ln2_b[i])
    return x
```

```python
import functools

import jax
import jax.numpy as jnp
from jax import lax
from jax.experimental import pallas as pl
from jax.experimental.pallas import tpu as pltpu

F32 = jnp.float32
BF16 = jnp.bfloat16

D_MODEL = 1024
SEQ = 16384
DEPTH = 4
GRID_W = 64
CTX_LEN = 256
N_MIXERS = 3
N_HEADS = 16
N_KV_HEADS = 4
HEAD_DIM = 64
ATTN_BLOCK = 128
ATTN_SCALE = HEAD_DIM ** -0.5
ROPE_BASE = 10000.0
ROPE_FREQS = HEAD_DIM // 4
CHUNK = 128
GMLP_GROUPS = 8
N_EXPERTS = 16
N_EXPERT_GROUPS = 4
EXPERTS_PER_GROUP = 4
D_EXPERT = 512
ALPHA = (2 * DEPTH) ** 0.25
LN_EPS = 1e-5

LANES = 128
T_TOK = CTX_LEN + SEQ
TM = 256
N_TILES = T_TOK // TM
CTX_TILES = CTX_LEN // TM
N_ABLK = T_TOK // ATTN_BLOCK
CTX_ABLK = CTX_LEN // ATTN_BLOCK
PAIRS_PER_GROUP = 6
N_CLASSES = N_EXPERT_GROUPS * PAIRS_PER_GROUP
MOE_BLOCK = 128
N_MOE_BLOCKS = T_TOK // MOE_BLOCK + N_CLASSES
P_ROWS = N_MOE_BLOCKS * MOE_BLOCK
XE_W = D_MODEL + LANES
PERM_ROWS = 2080
NEG_BIG = -1e30
VMEM_LIMIT = 52 * 1024 * 1024


def _cparams(sem="arbitrary"):
    return pltpu.CompilerParams(dimension_semantics=(sem,), vmem_limit_bytes=VMEM_LIMIT)


def _mod_row(ref):
    is_ctx = pl.program_id(0) < CTX_TILES
    return jnp.where(is_ctx, ref[1:2, :], ref[0:1, :])


def _layer_norm(x, g, b):
    mu = jnp.mean(x, axis=-1, keepdims=True)
    xc = x - mu
    var = jnp.mean(xc * xc, axis=-1, keepdims=True)
    return xc * lax.rsqrt(var + LN_EPS) * g + b


def _sigmoid(x):
    return 1.0 / (1.0 + jnp.exp(-x))


def _mod_kernel(cc_ref, w_ref, b_ref, o_ref):
    cc = cc_ref[...]
    act = cc * _sigmoid(cc)
    o_ref[...] = jnp.dot(act, w_ref[...], preferred_element_type=F32,
                         precision=lax.Precision.HIGHEST) + b_ref[...]


def _modulation(cc, w_mod, b_mod):
    nt = 1536
    return pl.pallas_call(
        _mod_kernel,
        grid=(DEPTH, 6 * D_MODEL // nt),
        in_specs=[
            pl.BlockSpec((8, D_MODEL), lambda l, n: (0, 0)),
            pl.BlockSpec((None, D_MODEL, nt), lambda l, n: (l, 0, n)),
            pl.BlockSpec((None, 1, nt), lambda l, n: (l, 0, n)),
        ],
        out_specs=pl.BlockSpec((None, 8, nt), lambda l, n: (l, 0, n)),
        out_shape=jax.ShapeDtypeStruct((DEPTH, 8, 6 * D_MODEL), F32),
        compiler_params=pltpu.CompilerParams(
            dimension_semantics=("arbitrary", "arbitrary"), vmem_limit_bytes=VMEM_LIMIT),
        name="modulation",
    )(cc, w_mod, b_mod.reshape(DEPTH, 1, 6 * D_MODEL))


def _mod_spec(layer, chunk):
    return pl.BlockSpec((None, 8, D_MODEL), lambda i: (layer, 0, chunk))


def _qkv_kernel(x_ref, sh_ref, sc_ref, w_ref, cos_ref, sa_ref, sb_ref, q_ref, k_ref, v_ref):
    h = (x_ref[...] * (1.0 + _mod_row(sc_ref)) + _mod_row(sh_ref)).astype(BF16)
    y = jnp.dot(h, w_ref[...], preferred_element_type=F32)
    cos, sa, sb = cos_ref[...], sa_ref[...], sb_ref[...]
    n_q = D_MODEL // LANES
    n_k = N_KV_HEADS
    for t in range(n_q + n_k):
        yt = y[:, t * LANES:(t + 1) * LANES]
        r = yt * cos + pltpu.roll(yt, LANES - ROPE_FREQS, 1) * sa + pltpu.roll(yt, ROPE_FREQS, 1) * sb
        if t < n_q:
            q_ref[:, t * LANES:(t + 1) * LANES] = (r * ATTN_SCALE).astype(BF16)
        else:
            k_ref[:, (t - n_q) * LANES:(t - n_q + 1) * LANES] = r.astype(BF16)
    v_ref[...] = y[:, (n_q + n_k) * LANES:].astype(BF16)


def _qkv_project(x, mods, layer, w, cos, sa, sb):
    kvw = N_KV_HEADS * LANES
    row = lambda i: (i, 0)
    return pl.pallas_call(
        _qkv_kernel,
        grid=(N_TILES,),
        in_specs=[
            pl.BlockSpec((TM, D_MODEL), row),
            _mod_spec(layer, 0), _mod_spec(layer, 1),
            pl.BlockSpec((D_MODEL, D_MODEL + 2 * kvw), lambda i: (0, 0)),
            pl.BlockSpec((TM, LANES), row), pl.BlockSpec((TM, LANES), row), pl.BlockSpec((TM, LANES), row),
        ],
        out_specs=[pl.BlockSpec((TM, D_MODEL), row), pl.BlockSpec((TM, kvw), row), pl.BlockSpec((TM, kvw), row)],
        out_shape=[jax.ShapeDtypeStruct((T_TOK, D_MODEL), BF16),
                   jax.ShapeDtypeStruct((T_TOK, kvw), BF16),
                   jax.ShapeDtypeStruct((T_TOK, kvw), BF16)],
        compiler_params=_cparams("parallel"),
        name="qkv_rope",
    )(x, mods, mods, w, cos, sa, sb)


def _attn_kernel(sink_ref, q_ref, kp_ref, kc_ref, kn_ref, kx_ref, vp_ref, vc_ref, vn_ref, vx_ref, o_ref):
    i = pl.program_id(0)
    blk = ATTN_BLOCK
    prev_ok = i >= CTX_ABLK + 1
    cur_ok = i >= CTX_ABLK
    next_ok = jnp.logical_and(i >= CTX_ABLK, i <= N_ABLK - 2)
    r = lax.broadcasted_iota(jnp.int32, (2 * blk, blk), 0) & (blk - 1)
    c = lax.broadcasted_iota(jnp.int32, (2 * blk, blk), 1)
    m_prev = jnp.logical_and(c >= r, prev_ok)
    m_cur = jnp.logical_and(c >= 0, cur_ok)
    m_next = jnp.logical_and(c <= r, next_ok)
    low_half = lax.broadcasted_iota(jnp.int32, (blk, LANES), 1) < HEAD_DIM
    top_rows = lax.broadcasted_iota(jnp.int32, (2 * blk, 1), 0) < blk
    nt_dims = (((1,), (1,)), ((), ()))
    for j in range(N_KV_HEADS):
        ks = slice(j * LANES, (j + 1) * LANES)
        kp, kc, kn, kx = kp_ref[:, ks], kc_ref[:, ks], kn_ref[:, ks], kx_ref[:, ks]
        vp, vc, vn, vx = vp_ref[:, ks], vc_ref[:, ks], vn_ref[:, ks], vx_ref[:, ks]
        for t in range(2):
            tile = 2 * j + t
            qs = slice(tile * LANES, (tile + 1) * LANES)
            qf = q_ref[:, qs].astype(F32)
            lhs = jnp.concatenate([jnp.where(low_half, qf, 0.0), jnp.where(low_half, 0.0, qf)],
                                  axis=0).astype(BF16)
            s_p = jnp.where(m_prev, lax.dot_general(lhs, kp, nt_dims, preferred_element_type=F32), NEG_BIG)
            s_c = jnp.where(m_cur, lax.dot_general(lhs, kc, nt_dims, preferred_element_type=F32), NEG_BIG)
            s_n = jnp.where(m_next, lax.dot_general(lhs, kn, nt_dims, preferred_element_type=F32), NEG_BIG)
            s_x = lax.dot_general(lhs, kx, nt_dims, preferred_element_type=F32)
            sink = jnp.where(top_rows, sink_ref[2 * tile], sink_ref[2 * tile + 1])
            m = jnp.maximum(
                jnp.maximum(jnp.max(s_p, axis=1, keepdims=True), jnp.max(s_c, axis=1, keepdims=True)),
                jnp.maximum(jnp.max(s_n, axis=1, keepdims=True), jnp.max(s_x, axis=1, keepdims=True)))
            m = jnp.maximum(m, sink)
            p_p, p_c, p_n, p_x = jnp.exp(s_p - m), jnp.exp(s_c - m), jnp.exp(s_n - m), jnp.exp(s_x - m)
            den = (jnp.sum(p_p, axis=1, keepdims=True) + jnp.sum(p_c, axis=1, keepdims=True)
                   + jnp.sum(p_n, axis=1, keepdims=True) + jnp.sum(p_x, axis=1, keepdims=True)
                   + jnp.exp(sink - m))
            o = (jnp.dot(p_p.astype(BF16), vp, preferred_element_type=F32)
                 + jnp.dot(p_c.astype(BF16), vc, preferred_element_type=F32)
                 + jnp.dot(p_n.astype(BF16), vn, preferred_element_type=F32)
                 + jnp.dot(p_x.astype(BF16), vx, preferred_element_type=F32))
            o = o / den
            o_ref[:, qs] = jnp.where(low_half, o[:blk], o[blk:]).astype(BF16)


def _attention(sink, q, kd, vd):
    kvw = N_KV_HEADS * LANES
    blk = ATTN_BLOCK
    lo, hi = CTX_ABLK, N_ABLK - 1
    prev_map = lambda i: (jnp.clip(i - 1, lo, hi), 0)
    cur_map = lambda i: (i, 0)
    next_map = lambda i: (jnp.clip(i + 1, lo, hi), 0)
    ctx_map = lambda i: (0, 0)
    kv_specs = [pl.BlockSpec((blk, kvw), prev_map), pl.BlockSpec((blk, kvw), cur_map),
                pl.BlockSpec((blk, kvw), next_map), pl.BlockSpec((CTX_LEN, kvw), ctx_map)]
    return pl.pallas_call(
        _attn_kernel,
        grid=(N_ABLK,),
        in_specs=[pl.BlockSpec(memory_space=pltpu.SMEM), pl.BlockSpec((blk, D_MODEL), cur_map)]
        + kv_specs + kv_specs,
        out_specs=pl.BlockSpec((blk, D_MODEL), cur_map),
        out_shape=jax.ShapeDtypeStruct((T_TOK, D_MODEL), BF16),
        compiler_params=_cparams("parallel"),
        name="window_attention",
    )(sink, q, kd, kd, kd, kd, vd, vd, vd, vd)


def _gmlp_kernel(x_ref, sh_ref, sc_ref, w_ref, b_ref, g_ref, be_ref, ws_ref, bs_ref, o_ref):
    h = (x_ref[...] * (1.0 + _mod_row(sc_ref)) + _mod_row(sh_ref)).astype(BF16)
    z = jax.nn.gelu(jnp.dot(h, w_ref[...], preferred_element_type=F32) + b_ref[...], approximate=True)
    u = z[:, :D_MODEL]
    v = _layer_norm(z[:, D_MODEL:], g_ref[...], be_ref[...]).astype(BF16)
    gd = D_MODEL // GMLP_GROUPS
    for n in range(TM // CHUNK):
        rows = slice(n * CHUNK, (n + 1) * CHUNK)
        for g in range(GMLP_GROUPS):
            cols = slice(g * gd, (g + 1) * gd)
            s = jnp.dot(ws_ref[g], v[rows, cols], preferred_element_type=F32) + bs_ref[:, cols]
            o_ref[rows, cols] = (u[rows, cols] * s).astype(BF16)


def _gmlp_mixer(x, mods, layer, w_in, b_in, ln_g, ln_b, w_s, b_s):
    row = lambda i: (i, 0)
    fix2 = lambda i: (0, 0)
    return pl.pallas_call(
        _gmlp_kernel,
        grid=(N_TILES,),
        in_specs=[
            pl.BlockSpec((TM, D_MODEL), row),
            _mod_spec(layer, 0), _mod_spec(layer, 1),
            pl.BlockSpec((D_MODEL, 2 * D_MODEL), fix2),
            pl.BlockSpec((1, 2 * D_MODEL), fix2),
            pl.BlockSpec((1, D_MODEL), fix2), pl.BlockSpec((1, D_MODEL), fix2),
            pl.BlockSpec((GMLP_GROUPS, CHUNK, CHUNK), lambda i: (0, 0, 0)),
            pl.BlockSpec((CHUNK, D_MODEL), fix2),
        ],
        out_specs=pl.BlockSpec((TM, D_MODEL), row),
        out_shape=jax.ShapeDtypeStruct((T_TOK, D_MODEL), BF16),
        compiler_params=_cparams("parallel"),
        name="gmlp_mixer",
    )(x, mods, mods, w_in, b_in, ln_g, ln_b, w_s, b_s)


def _conv_in_kernel(x_ref, sh_ref, sc_ref, w_ref, gb_ref, zz_ref):
    h = (x_ref[...] * (1.0 + _mod_row(sc_ref)) + _mod_row(sh_ref)).astype(BF16)
    proj = jnp.dot(h, w_ref[...], preferred_element_type=F32)
    gb_ref[...] = proj[:, :D_MODEL]
    zz_ref[...] = proj[:, D_MODEL:2 * D_MODEL] * proj[:, 2 * D_MODEL:]


def _conv_in(x, mods, layer, w_in):
    row = lambda i: (i, 0)
    return pl.pallas_call(
        _conv_in_kernel,
        grid=(N_TILES,),
        in_specs=[pl.BlockSpec((TM, D_MODEL), row), _mod_spec(layer, 0), _mod_spec(layer, 1),
                  pl.BlockSpec((D_MODEL, 3 * D_MODEL), lambda i: (0, 0))],
        out_specs=[pl.BlockSpec((TM, D_MODEL), row), pl.BlockSpec((TM, D_MODEL), row)],
        out_shape=[jax.ShapeDtypeStruct((T_TOK, D_MODEL), F32), jax.ShapeDtypeStruct((T_TOK, D_MODEL), F32)],
        compiler_params=_cparams("parallel"),
        name="conv_in",
    )(x, mods, mods, w_in)


def _route(h2, rw_ref, rb_ref, tri_ref, carry_ref, xe_ref, meta_ref, cnt_ref):
    logits = jnp.dot(h2, rw_ref[...], preferred_element_type=F32, precision=lax.Precision.HIGHEST)
    lane_i = lax.broadcasted_iota(jnp.int32, (TM, LANES), 1)
    lane_f = lane_i.astype(F32)
    valid = lane_i < N_EXPERTS
    s = _sigmoid(logits)
    ninf = -jnp.inf
    ssel = jnp.where(valid, s + rb_ref[...], ninf)
    odd1 = (lane_i & 1) == 1
    odd2 = (lane_i & 2) == 2
    xor1 = lambda v: jnp.where(odd1, pltpu.roll(v, 1, 1), pltpu.roll(v, LANES - 1, 1))
    xor2 = lambda v: jnp.where(odd2, pltpu.roll(v, 2, 1), pltpu.roll(v, LANES - 2, 1))
    a = ssel
    b = xor1(a)
    c = xor2(a)
    d = xor1(c)
    top2 = jnp.maximum(jnp.maximum(jnp.maximum(a + b, a + c), jnp.maximum(a + d, b + c)),
                       jnp.maximum(b + d, c + d))
    gs = jnp.where(valid, top2, ninf)
    gmax = jnp.max(gs, axis=1, keepdims=True)
    grp_f = (lane_i >> 2).astype(F32)
    far = 1e9
    gsel = jnp.min(jnp.where(gs == gmax, grp_f, far), axis=1, keepdims=True)
    masked = jnp.where(grp_f == gsel, ssel, ninf)
    m1 = jnp.max(masked, axis=1, keepdims=True)
    i1 = jnp.min(jnp.where(masked == m1, lane_f, far), axis=1, keepdims=True)
    masked2 = jnp.where(lane_f == i1, ninf, masked)
    m2 = jnp.max(masked2, axis=1, keepdims=True)
    i2 = jnp.min(jnp.where(masked2 == m2, lane_f, far), axis=1, keepdims=True)
    lo = jnp.minimum(i1, i2)
    hi = jnp.maximum(i1, i2)
    s_lo = jnp.sum(jnp.where(lane_f == lo, s, 0.0), axis=1, keepdims=True)
    s_hi = jnp.sum(jnp.where(lane_f == hi, s, 0.0), axis=1, keepdims=True)
    den = s_lo + s_hi
    g_lo = s_lo / den
    g_hi = s_hi / den
    lo_l = lo - EXPERTS_PER_GROUP * gsel
    hi_l = hi - EXPERTS_PER_GROUP * gsel
    pair = lo_l * (7.0 - lo_l) * 0.5 + hi_l - lo_l - 1.0
    cls = gsel * PAIRS_PER_GROUP + pair
    onehot = lane_f == cls
    oh_f = jnp.where(onehot, 1.0, 0.0)
    before = jnp.dot(tri_ref[...], oh_f.astype(BF16), preferred_element_type=F32)
    carry = carry_ref[0:1, :]
    rank = jnp.sum(jnp.where(onehot, before + carry, 0.0), axis=1, keepdims=True)
    carry = carry + jnp.sum(oh_f, axis=0, keepdims=True)
    carry_ref[...] = jnp.broadcast_to(carry, carry_ref.shape)
    cnt_ref[...] = jnp.broadcast_to(carry, cnt_ref.shape)
    xe_ref[:, :D_MODEL] = h2
    xe_ref[:, D_MODEL:] = jnp.where(lane_i < LANES // 2, g_lo, g_hi)
    meta_ref[...] = jnp.where(lane_i == 0, cls, jnp.where(lane_i == 1, rank, 0.0)).astype(jnp.int32)


def _post_kernel(*refs, conv):
    if conv:
        (gb_ref, zz_ref, zp_ref, zn_ref, wc_ref, *refs) = refs
    else:
        (a_ref, *refs) = refs
    (w_ref, x_ref, ga_ref, shf_ref, scf_ref, lg_ref, lb_ref, rw_ref, rb_ref,
     x1_ref, xe_ref, meta_ref, cnt_ref, tri_ref, carry_ref) = refs
    i = pl.program_id(0)

    @pl.when(i == 0)
    def _():
        carry_ref[...] = jnp.zeros_like(carry_ref)
        rr = lax.broadcasted_iota(jnp.int32, (TM, TM), 0)
        cc = lax.broadcasted_iota(jnp.int32, (TM, TM), 1)
        tri_ref[...] = jnp.where(cc < rr, 1.0, 0.0).astype(BF16)

    if conv:
        zz = zz_ref[...]
        prev_ok = i >= CTX_TILES + 1
        next_ok = jnp.logical_and(i >= CTX_TILES, i <= N_TILES - 2)
        z_before = jnp.where(prev_ok, zp_ref[7:8, :], 0.0)
        z_after = jnp.where(next_ok, zn_ref[0:1, :], 0.0)
        rows = lax.broadcasted_iota(jnp.int32, (TM, 1), 0)
        zm1 = jnp.where(rows == 0, z_before, pltpu.roll(zz, 1, 0))
        zp1 = jnp.where(rows == TM - 1, z_after, pltpu.roll(zz, TM - 1, 0))
        zc = wc_ref[0:1, :] * zm1 + wc_ref[1:2, :] * zz + wc_ref[2:3, :] * zp1
        a = (gb_ref[...] * zc).astype(BF16)
    else:
        a = a_ref[...]
    y = jnp.dot(a, w_ref[...], preferred_element_type=F32)
    x1 = _layer_norm(ALPHA * x_ref[...] + _mod_row(ga_ref) * y, lg_ref[...], lb_ref[...])
    x1_ref[...] = x1
    h2 = x1 * (1.0 + _mod_row(scf_ref)) + _mod_row(shf_ref)
    _route(h2, rw_ref, rb_ref, tri_ref, carry_ref, xe_ref, meta_ref, cnt_ref)


def _post(mix, w_out, x, mods, layer, ln_g, ln_b, rw, rb, conv):
    row = lambda i: (i, 0)
    fix2 = lambda i: (0, 0)
    tile = pl.BlockSpec((TM, D_MODEL), row)
    if conv:
        gb, zz, wc = mix
        sub = TM // 8
        mix_args = (gb, zz, zz, zz, wc)
        mix_specs = [tile, tile,
                     pl.BlockSpec((8, D_MODEL), lambda i: (jnp.maximum(i * sub - 1, 0), 0)),
                     pl.BlockSpec((8, D_MODEL), lambda i: (jnp.minimum((i + 1) * sub, T_TOK // 8 - 1), 0)),
                     pl.BlockSpec((8, D_MODEL), fix2)]
    else:
        mix_args = (mix,)
        mix_specs = [tile]
    vec = pl.BlockSpec((1, D_MODEL), fix2)
    return pl.pallas_call(
        functools.partial(_post_kernel, conv=conv),
        grid=(N_TILES,),
        in_specs=mix_specs + [
            pl.BlockSpec((D_MODEL, D_MODEL), fix2), tile,
            _mod_spec(layer, 2), _mod_spec(layer, 3), _mod_spec(layer, 4),
            vec, vec,
            pl.BlockSpec((D_MODEL, LANES), fix2), pl.BlockSpec((1, LANES), fix2),
        ],
        out_specs=[tile, pl.BlockSpec((TM, XE_W), row), pl.BlockSpec((TM, LANES), row),
                   pl.BlockSpec((8, LANES), fix2)],
        out_shape=[jax.ShapeDtypeStruct((T_TOK, D_MODEL), F32),
                   jax.ShapeDtypeStruct((T_TOK, XE_W), F32),
                   jax.ShapeDtypeStruct((T_TOK, LANES), jnp.int32),
                   jax.ShapeDtypeStruct((8, LANES), F32)],
        scratch_shapes=[pltpu.VMEM((TM, TM), BF16), pltpu.VMEM((8, LANES), F32)],
        compiler_params=_cparams("arbitrary"),
        name="post_conv" if conv else "post",
    )(*mix_args, w_out, x, mods, mods, mods, ln_g, ln_b, rw, rb)


def _permute_kernel(src_ref, dst_ref, in_ref, *rest):
    out_ref, sem = rest[-2], rest[-1]

    def row_copy(s, d):
        return pltpu.make_async_copy(in_ref.at[pl.ds(s, 1)], out_ref.at[pl.ds(d, 1)], sem)

    def issue(j, carry):
        row_copy(src_ref[0, j], dst_ref[0, j]).start()
        return carry

    def drain(j, carry):
        row_copy(0, 0).wait()
        return carry

    lax.fori_loop(0, PERM_ROWS, issue, 0, unroll=8)
    lax.fori_loop(0, PERM_ROWS, drain, 0, unroll=8)


def _permute_rows(src_idx, dst_idx, data, out_rows, init=None):
    steps = T_TOK // PERM_ROWS
    idx_spec = pl.BlockSpec((None, 1, PERM_ROWS), lambda i: (i, 0, 0), memory_space=pltpu.SMEM)
    any_spec = pl.BlockSpec(memory_space=pl.ANY)
    args = [src_idx.reshape(steps, 1, PERM_ROWS), dst_idx.reshape(steps, 1, PERM_ROWS), data]
    specs = [idx_spec, idx_spec, any_spec]
    aliases = {}
    if init is not None:
        args.append(init)
        specs.append(any_spec)
        aliases = {3: 0}
    return pl.pallas_call(
        _permute_kernel,
        grid=(steps,),
        in_specs=specs,
        out_specs=any_spec,
        out_shape=jax.ShapeDtypeStruct((out_rows, data.shape[1]), data.dtype),
        scratch_shapes=[pltpu.SemaphoreType.DMA(())],
        input_output_aliases=aliases,
        compiler_params=_cparams("arbitrary"),
        name="permute_rows",
    )(*args)


def _expert_kernel(ea_ref, eb_ref, nused_ref, x_ref, w1a_ref, w3a_ref, w2a_ref, w1b_ref, w3b_ref, w2b_ref,
                   o_ref, w1a_s, w3a_s, w2a_s, w1b_s, w3b_s, w2b_s):
    b = pl.program_id(0)
    prev = jnp.maximum(b - 1, 0)
    first = b == 0

    @pl.when(jnp.logical_or(first, ea_ref[b] != ea_ref[prev]))
    def _():
        w1a_s[...] = w1a_ref[...].astype(BF16)
        w3a_s[...] = w3a_ref[...].astype(BF16)
        w2a_s[...] = w2a_ref[...].astype(BF16)

    @pl.when(jnp.logical_or(first, eb_ref[b] != eb_ref[prev]))
    def _():
        w1b_s[...] = w1b_ref[...].astype(BF16)
        w3b_s[...] = w3b_ref[...].astype(BF16)
        w2b_s[...] = w2b_ref[...].astype(BF16)

    used = b < nused_ref[0]

    @pl.when(used)
    def _():
        xb = x_ref[:, :D_MODEL].astype(BF16)
        gates = x_ref[:, D_MODEL:]

        def expert(w1, w3, w2):
            h1 = jnp.dot(xb, w1[...], preferred_element_type=F32)
            h3 = jnp.dot(xb, w3[...], preferred_element_type=F32)
            act = (h1 * _sigmoid(h1) * h3).astype(BF16)
            return jnp.dot(act, w2[...], preferred_element_type=F32)

        ya = expert(w1a_s, w3a_s, w2a_s)
        yb = expert(w1b_s, w3b_s, w2b_s)
        o_ref[...] = gates[:, 0:1] * ya + gates[:, LANES // 2:LANES // 2 + 1] * yb

    @pl.when(jnp.logical_not(used))
    def _():
        o_ref[...] = jnp.zeros_like(o_ref)


def _experts(ea, eb, nused, xs, w1, w3, w2, layer):
    up = lambda sel: pl.BlockSpec((None, None, D_MODEL, D_EXPERT),
                                  lambda b, ea, eb, nu: (layer, (ea, eb)[sel][b], 0, 0))
    down = lambda sel: pl.BlockSpec((None, None, D_EXPERT, D_MODEL),
                                    lambda b, ea, eb, nu: (layer, (ea, eb)[sel][b], 0, 0))
    grid_spec = pltpu.PrefetchScalarGridSpec(
        num_scalar_prefetch=3,
        grid=(N_MOE_BLOCKS,),
        in_specs=[pl.BlockSpec((MOE_BLOCK, XE_W), lambda b, ea, eb, nu: (b, 0)),
                  up(0), up(0), down(0), up(1), up(1), down(1)],
        out_specs=pl.BlockSpec((MOE_BLOCK, D_MODEL), lambda b, ea, eb, nu: (b, 0)),
        scratch_shapes=[pltpu.VMEM((D_MODEL, D_EXPERT), BF16), pltpu.VMEM((D_MODEL, D_EXPERT), BF16),
                        pltpu.VMEM((D_EXPERT, D_MODEL), BF16),
                        pltpu.VMEM((D_MODEL, D_EXPERT), BF16), pltpu.VMEM((D_MODEL, D_EXPERT), BF16),
                        pltpu.VMEM((D_EXPERT, D_MODEL), BF16)],
    )
    return pl.pallas_call(
        _expert_kernel,
        grid_spec=grid_spec,
        out_shape=jax.ShapeDtypeStruct((P_ROWS, D_MODEL), F32),
        compiler_params=_cparams("arbitrary"),
        name="expert_pairs",
    )(ea, eb, nused, xs, w1, w3, w2, w1, w3, w2)


def _ln2_kernel(x_ref, y_ref, gf_ref, lg_ref, lb_ref, o_ref):
    o_ref[...] = _layer_norm(ALPHA * x_ref[...] + _mod_row(gf_ref) * y_ref[...], lg_ref[...], lb_ref[...])


def _ln2(x1, y, mods, layer, ln_g, ln_b):
    row = lambda i: (i, 0)
    tile = pl.BlockSpec((TM, D_MODEL), row)
    vec = pl.BlockSpec((1, D_MODEL), lambda i: (0, 0))
    return pl.pallas_call(
        _ln2_kernel,
        grid=(N_TILES,),
        in_specs=[tile, tile, _mod_spec(layer, 5), vec, vec],
        out_specs=tile,
        out_shape=jax.ShapeDtypeStruct((T_TOK, D_MODEL), F32),
        compiler_params=_cparams("parallel"),
        name="moe_residual_ln",
    )(x1, y, mods, ln_g, ln_b)


def _class_tables():
    lo, hi = [], []
    for g in range(N_EXPERT_GROUPS):
        for a in range(EXPERTS_PER_GROUP):
            for b in range(a + 1, EXPERTS_PER_GROUP):
                lo.append(g * EXPERTS_PER_GROUP + a)
                hi.append(g * EXPERTS_PER_GROUP + b)
    return jnp.array(lo, jnp.int32), jnp.array(hi, jnp.int32)


def _dispatch_plan(meta, counts):
    cls = meta[:, 0]
    rank = meta[:, 1]
    cnt = counts[0, :N_CLASSES].astype(jnp.int32)
    padded = (cnt + MOE_BLOCK - 1) // MOE_BLOCK * MOE_BLOCK
    pad_end = jnp.cumsum(padded)
    pad_start = pad_end - padded
    dest = pad_start[cls] + rank
    blk_cls = jnp.minimum(
        jnp.searchsorted(pad_end, jnp.arange(N_MOE_BLOCKS, dtype=jnp.int32) * MOE_BLOCK, side="right"),
        N_CLASSES - 1)
    cls_lo, cls_hi = _class_tables()
    nused = (pad_end[-1:] // MOE_BLOCK).astype(jnp.int32)
    return dest.astype(jnp.int32), cls_lo[blk_cls], cls_hi[blk_cls], nused


def _rope_tables():
    pos = jnp.arange(SEQ)
    row = (pos // GRID_W).astype(F32)
    col = (pos % GRID_W).astype(F32)
    freqs = jnp.power(ROPE_BASE, -jnp.arange(ROPE_FREQS, dtype=F32) / ROPE_FREQS)
    ar = row[:, None] * freqs[None, :]
    ac = col[:, None] * freqs[None, :]
    ang = jnp.concatenate([ar, ar, ac, ac], axis=-1)
    cos, sin = jnp.cos(ang), jnp.sin(ang)
    cos = jnp.concatenate([jnp.ones((CTX_LEN, HEAD_DIM), F32), cos], axis=0)
    sin = jnp.concatenate([jnp.zeros((CTX_LEN, HEAD_DIM), F32), sin], axis=0)
    first_half = (jnp.arange(HEAD_DIM) % (2 * ROPE_FREQS)) < ROPE_FREQS
    sa = jnp.where(first_half[None, :], -sin, 0.0)
    sb = jnp.where(first_half[None, :], 0.0, sin)
    two = lambda t: jnp.concatenate([t, t], axis=-1)
    return two(cos), two(sa), two(sb)


def _dup_heads(w):
    w4 = w.reshape(D_MODEL, N_KV_HEADS, 1, HEAD_DIM)
    return jnp.broadcast_to(w4, (D_MODEL, N_KV_HEADS, LANES // HEAD_DIM, HEAD_DIM)).reshape(D_MODEL, -1)


def kernel(x, c, ctx, c_ctx, w_mod, b_mod, ln1_g, ln1_b, ln2_g, ln2_b, router_w, router_bias, moe_w1, moe_w3, moe_w2, a_w_qkv, a_w_o, a_sink, b_w_in, b_b_in, b_ln_g, b_ln_b, b_w_s, b_b_s, b_w_out, c_w_in, c_w_conv, c_w_out):
    assert x.shape == (1, SEQ, D_MODEL) and ctx.shape == (1, CTX_LEN, D_MODEL)
    tok = jnp.concatenate([ctx[0], x[0]], axis=0)
    cc = jnp.zeros((8, D_MODEL), F32).at[0].set(c[0]).at[1].set(c_ctx)
    mods = _modulation(cc, w_mod, b_mod)
    cos, sa, sb = _rope_tables()
    rw = jnp.pad(router_w, ((0, 0), (0, LANES - N_EXPERTS)))
    rb = jnp.pad(router_bias, (0, LANES - N_EXPERTS)).reshape(1, LANES)
    token_ids = jnp.arange(T_TOK, dtype=jnp.int32)
    qd = N_HEADS * HEAD_DIM
    kd = N_KV_HEADS * HEAD_DIM

    for i in range(DEPTH):
        kind, j = i % N_MIXERS, i // N_MIXERS
        if kind == 0:
            w = a_w_qkv[j]
            w_all = jnp.concatenate([w[:, :qd], _dup_heads(w[:, qd:qd + kd]), _dup_heads(w[:, qd + kd:])],
                                    axis=1).astype(BF16)
            q, k2, v2 = _qkv_project(tok, mods, i, w_all, cos, sa, sb)
            mix = _attention(a_sink[j], q, k2, v2)
            w_out = a_w_o[j]
        elif kind == 1:
            bs = jnp.repeat(b_b_s[j], D_MODEL // GMLP_GROUPS, axis=1)
            mix = _gmlp_mixer(tok, mods, i, b_w_in[j].astype(BF16), b_b_in[j].reshape(1, -1),
                              b_ln_g[j].reshape(1, -1), b_ln_b[j].reshape(1, -1), b_w_s[j].astype(BF16), bs)
            w_out = b_w_out[j]
        else:
            gb, zz = _conv_in(tok, mods, i, c_w_in[j].astype(BF16))
            mix = (gb, zz, jnp.pad(c_w_conv[j], ((0, 5), (0, 0))))
            w_out = c_w_out[j]
        x1, xe, meta, counts = _post(mix, w_out.astype(BF16), tok, mods, i,
                                     ln1_g[i].reshape(1, -1), ln1_b[i].reshape(1, -1), rw, rb, conv=(kind == 2))
        dest, ea, eb, nused = _dispatch_plan(meta, counts)
        xs = _permute_rows(token_ids, dest, xe, P_ROWS, init=jnp.zeros((P_ROWS, XE_W), F32))
        ys = _experts(ea, eb, nused, xs, moe_w1, moe_w3, moe_w2, i)
        y = _permute_rows(dest, token_ids, ys, T_TOK)
        tok = _ln2(x1, y, mods, i, ln2_g[i].reshape(1, -1), ln2_b[i].reshape(1, -1))
    return tok[CTX_LEN:].reshape(1, SEQ, D_MODEL)
```

```python
import functools

import jax
import jax.numpy as jnp
from jax import lax
from jax.experimental import pallas as pl
from jax.experimental.pallas import tpu as pltpu

F32 = jnp.float32
BF16 = jnp.bfloat16

D_MODEL = 1024
SEQ = 16384
DEPTH = 4
GRID_W = 64
CTX_LEN = 256
N_MIXERS = 3
N_HEADS = 16
N_KV_HEADS = 4
HEAD_DIM = 64
ATTN_BLOCK = 128
ATTN_SCALE = HEAD_DIM ** -0.5
ROPE_BASE = 10000.0
ROPE_FREQS = HEAD_DIM // 4
CHUNK = 128
GMLP_GROUPS = 8
N_EXPERTS = 16
N_EXPERT_GROUPS = 4
EXPERTS_PER_GROUP = 4
D_EXPERT = 512
ALPHA = (2 * DEPTH) ** 0.25
LN_EPS = 1e-5

LANES = 128
T_TOK = CTX_LEN + SEQ
TM = 256
N_TILES = T_TOK // TM
CTX_TILES = CTX_LEN // TM
N_ABLK = T_TOK // ATTN_BLOCK
CTX_ABLK = CTX_LEN // ATTN_BLOCK
PAIRS_PER_GROUP = 6
N_CLASSES = N_EXPERT_GROUPS * PAIRS_PER_GROUP
MOE_BLOCK = 128
N_MOE_BLOCKS = T_TOK // MOE_BLOCK + N_CLASSES
P_ROWS = N_MOE_BLOCKS * MOE_BLOCK
XE_W = D_MODEL + LANES
SCAT_TILES = 5
NEG_BIG = -1e30
VMEM_LIMIT = 52 * 1024 * 1024


def _cparams(sem="arbitrary"):
    return pltpu.CompilerParams(dimension_semantics=(sem,), vmem_limit_bytes=VMEM_LIMIT)


def _mod_row(ref):
    is_ctx = pl.program_id(0) < CTX_TILES
    return jnp.where(is_ctx, ref[1:2, :], ref[0:1, :])


def _layer_norm(x, g, b):
    mu = jnp.mean(x, axis=-1, keepdims=True)
    xc = x - mu
    var = jnp.mean(xc * xc, axis=-1, keepdims=True)
    return xc * lax.rsqrt(var + LN_EPS) * g + b


def _sigmoid(x):
    return 1.0 / (1.0 + jnp.exp(-x))


def _mod_kernel(cc_ref, w_ref, b_ref, o_ref):
    cc = cc_ref[...]
    act = cc * _sigmoid(cc)
    o_ref[...] = jnp.dot(act, w_ref[...], preferred_element_type=F32,
                         precision=lax.Precision.HIGHEST) + b_ref[...]


def _modulation(cc, w_mod, b_mod):
    nt = 1536
    return pl.pallas_call(
        _mod_kernel,
        grid=(DEPTH, 6 * D_MODEL // nt),
        in_specs=[
            pl.BlockSpec((8, D_MODEL), lambda l, n: (0, 0)),
            pl.BlockSpec((None, D_MODEL, nt), lambda l, n: (l, 0, n)),
            pl.BlockSpec((None, 1, nt), lambda l, n: (l, 0, n)),
        ],
        out_specs=pl.BlockSpec((None, 8, nt), lambda l, n: (l, 0, n)),
        out_shape=jax.ShapeDtypeStruct((DEPTH, 8, 6 * D_MODEL), F32),
        compiler_params=pltpu.CompilerParams(
            dimension_semantics=("arbitrary", "arbitrary"), vmem_limit_bytes=VMEM_LIMIT),
        name="modulation",
    )(cc, w_mod, b_mod.reshape(DEPTH, 1, 6 * D_MODEL))


def _mod_spec(layer, chunk):
    return pl.BlockSpec((None, 8, D_MODEL), lambda i: (layer, 0, chunk))


def _qkv_kernel(x_ref, sh_ref, sc_ref, w_ref, cos_ref, sa_ref, sb_ref, q_ref, k_ref, v_ref):
    h = (x_ref[...] * (1.0 + _mod_row(sc_ref)) + _mod_row(sh_ref)).astype(BF16)
    y = jnp.dot(h, w_ref[...], preferred_element_type=F32)
    cos, sa, sb = cos_ref[...], sa_ref[...], sb_ref[...]
    n_q = D_MODEL // LANES
    n_k = N_KV_HEADS
    for t in range(n_q + n_k):
        yt = y[:, t * LANES:(t + 1) * LANES]
        r = yt * cos + pltpu.roll(yt, LANES - ROPE_FREQS, 1) * sa + pltpu.roll(yt, ROPE_FREQS, 1) * sb
        if t < n_q:
            q_ref[:, t * LANES:(t + 1) * LANES] = (r * ATTN_SCALE).astype(BF16)
        else:
            k_ref[:, (t - n_q) * LANES:(t - n_q + 1) * LANES] = r.astype(BF16)
    v_ref[...] = y[:, (n_q + n_k) * LANES:].astype(BF16)


def _qkv_project(x, mods, layer, w, cos, sa, sb):
    kvw = N_KV_HEADS * LANES
    row = lambda i: (i, 0)
    return pl.pallas_call(
        _qkv_kernel,
        grid=(N_TILES,),
        in_specs=[
            pl.BlockSpec((TM, D_MODEL), row),
            _mod_spec(layer, 0), _mod_spec(layer, 1),
            pl.BlockSpec((D_MODEL, D_MODEL + 2 * kvw), lambda i: (0, 0)),
            pl.BlockSpec((TM, LANES), row), pl.BlockSpec((TM, LANES), row), pl.BlockSpec((TM, LANES), row),
        ],
        out_specs=[pl.BlockSpec((TM, D_MODEL), row), pl.BlockSpec((TM, kvw), row), pl.BlockSpec((TM, kvw), row)],
        out_shape=[jax.ShapeDtypeStruct((T_TOK, D_MODEL), BF16),
                   jax.ShapeDtypeStruct((T_TOK, kvw), BF16),
                   jax.ShapeDtypeStruct((T_TOK, kvw), BF16)],
        compiler_params=_cparams("parallel"),
        name="qkv_rope",
    )(x, mods, mods, w, cos, sa, sb)


def _attn_kernel(sink_ref, q_ref, kp_ref, kc_ref, kn_ref, kx_ref, vp_ref, vc_ref, vn_ref, vx_ref, o_ref):
    i = pl.program_id(0)
    blk = ATTN_BLOCK
    prev_ok = i >= CTX_ABLK + 1
    cur_ok = i >= CTX_ABLK
    next_ok = jnp.logical_and(i >= CTX_ABLK, i <= N_ABLK - 2)
    r = lax.broadcasted_iota(jnp.int32, (2 * blk, blk), 0) & (blk - 1)
    c = lax.broadcasted_iota(jnp.int32, (2 * blk, blk), 1)
    m_prev = jnp.logical_and(c >= r, prev_ok)
    m_cur = jnp.logical_and(c >= 0, cur_ok)
    m_next = jnp.logical_and(c <= r, next_ok)
    low_half = lax.broadcasted_iota(jnp.int32, (blk, LANES), 1) < HEAD_DIM
    top_rows = lax.broadcasted_iota(jnp.int32, (2 * blk, 1), 0) < blk
    nt_dims = (((1,), (1,)), ((), ()))
    for j in range(N_KV_HEADS):
        ks = slice(j * LANES, (j + 1) * LANES)
        kp, kc, kn, kx = kp_ref[:, ks], kc_ref[:, ks], kn_ref[:, ks], kx_ref[:, ks]
        vp, vc, vn, vx = vp_ref[:, ks], vc_ref[:, ks], vn_ref[:, ks], vx_ref[:, ks]
        for t in range(2):
            tile = 2 * j + t
            qs = slice(tile * LANES, (tile + 1) * LANES)
            qf = q_ref[:, qs].astype(F32)
            lhs = jnp.concatenate([jnp.where(low_half, qf, 0.0), jnp.where(low_half, 0.0, qf)],
                                  axis=0).astype(BF16)
            s_p = jnp.where(m_prev, lax.dot_general(lhs, kp, nt_dims, preferred_element_type=F32), NEG_BIG)
            s_c = jnp.where(m_cur, lax.dot_general(lhs, kc, nt_dims, preferred_element_type=F32), NEG_BIG)
            s_n = jnp.where(m_next, lax.dot_general(lhs, kn, nt_dims, preferred_element_type=F32), NEG_BIG)
            s_x = lax.dot_general(lhs, kx, nt_dims, preferred_element_type=F32)
            sink = jnp.where(top_rows, sink_ref[2 * tile], sink_ref[2 * tile + 1])
            m = jnp.maximum(
                jnp.maximum(jnp.max(s_p, axis=1, keepdims=True), jnp.max(s_c, axis=1, keepdims=True)),
                jnp.maximum(jnp.max(s_n, axis=1, keepdims=True), jnp.max(s_x, axis=1, keepdims=True)))
            m = jnp.maximum(m, sink)
            p_p, p_c, p_n, p_x = jnp.exp(s_p - m), jnp.exp(s_c - m), jnp.exp(s_n - m), jnp.exp(s_x - m)
            den = (jnp.sum(p_p, axis=1, keepdims=True) + jnp.sum(p_c, axis=1, keepdims=True)
                   + jnp.sum(p_n, axis=1, keepdims=True) + jnp.sum(p_x, axis=1, keepdims=True)
                   + jnp.exp(sink - m))
            o = (jnp.dot(p_p.astype(BF16), vp, preferred_element_type=F32)
                 + jnp.dot(p_c.astype(BF16), vc, preferred_element_type=F32)
                 + jnp.dot(p_n.astype(BF16), vn, preferred_element_type=F32)
                 + jnp.dot(p_x.astype(BF16), vx, preferred_element_type=F32))
            o = o / den
            o_ref[:, qs] = jnp.where(low_half, o[:blk], o[blk:]).astype(BF16)


def _attention(sink, q, kd, vd):
    kvw = N_KV_HEADS * LANES
    blk = ATTN_BLOCK
    lo, hi = CTX_ABLK, N_ABLK - 1
    prev_map = lambda i: (jnp.clip(i - 1, lo, hi), 0)
    cur_map = lambda i: (i, 0)
    next_map = lambda i: (jnp.clip(i + 1, lo, hi), 0)
    ctx_map = lambda i: (0, 0)
    kv_specs = [pl.BlockSpec((blk, kvw), prev_map), pl.BlockSpec((blk, kvw), cur_map),
                pl.BlockSpec((blk, kvw), next_map), pl.BlockSpec((CTX_LEN, kvw), ctx_map)]
    return pl.pallas_call(
        _attn_kernel,
        grid=(N_ABLK,),
        in_specs=[pl.BlockSpec(memory_space=pltpu.SMEM), pl.BlockSpec((blk, D_MODEL), cur_map)]
        + kv_specs + kv_specs,
        out_specs=pl.BlockSpec((blk, D_MODEL), cur_map),
        out_shape=jax.ShapeDtypeStruct((T_TOK, D_MODEL), BF16),
        compiler_params=_cparams("parallel"),
        name="window_attention",
    )(sink, q, kd, kd, kd, kd, vd, vd, vd, vd)


def _gmlp_kernel(x_ref, sh_ref, sc_ref, w_ref, b_ref, g_ref, be_ref, ws_ref, bs_ref, o_ref):
    h = (x_ref[...] * (1.0 + _mod_row(sc_ref)) + _mod_row(sh_ref)).astype(BF16)
    z = jax.nn.gelu(jnp.dot(h, w_ref[...], preferred_element_type=F32) + b_ref[...], approximate=True)
    u = z[:, :D_MODEL]
    v = _layer_norm(z[:, D_MODEL:], g_ref[...], be_ref[...]).astype(BF16)
    gd = D_MODEL // GMLP_GROUPS
    for n in range(TM // CHUNK):
        rows = slice(n * CHUNK, (n + 1) * CHUNK)
        for g in range(GMLP_GROUPS):
            cols = slice(g * gd, (g + 1) * gd)
            s = jnp.dot(ws_ref[g], v[rows, cols], preferred_element_type=F32) + bs_ref[:, cols]
            o_ref[rows, cols] = (u[rows, cols] * s).astype(BF16)


def _gmlp_mixer(x, mods, layer, w_in, b_in, ln_g, ln_b, w_s, b_s):
    row = lambda i: (i, 0)
    fix2 = lambda i: (0, 0)
    return pl.pallas_call(
        _gmlp_kernel,
        grid=(N_TILES,),
        in_specs=[
            pl.BlockSpec((TM, D_MODEL), row),
            _mod_spec(layer, 0), _mod_spec(layer, 1),
            pl.BlockSpec((D_MODEL, 2 * D_MODEL), fix2),
            pl.BlockSpec((1, 2 * D_MODEL), fix2),
            pl.BlockSpec((1, D_MODEL), fix2), pl.BlockSpec((1, D_MODEL), fix2),
            pl.BlockSpec((GMLP_GROUPS, CHUNK, CHUNK), lambda i: (0, 0, 0)),
            pl.BlockSpec((CHUNK, D_MODEL), fix2),
        ],
        out_specs=pl.BlockSpec((TM, D_MODEL), row),
        out_shape=jax.ShapeDtypeStruct((T_TOK, D_MODEL), BF16),
        compiler_params=_cparams("parallel"),
        name="gmlp_mixer",
    )(x, mods, mods, w_in, b_in, ln_g, ln_b, w_s, b_s)


def _conv_in_kernel(x_ref, sh_ref, sc_ref, w_ref, gb_ref, zz_ref):
    h = (x_ref[...] * (1.0 + _mod_row(sc_ref)) + _mod_row(sh_ref)).astype(BF16)
    proj = jnp.dot(h, w_ref[...], preferred_element_type=F32)
    gb_ref[...] = proj[:, :D_MODEL]
    zz_ref[...] = proj[:, D_MODEL:2 * D_MODEL] * proj[:, 2 * D_MODEL:]


def _conv_in(x, mods, layer, w_in):
    row = lambda i: (i, 0)
    return pl.pallas_call(
        _conv_in_kernel,
        grid=(N_TILES,),
        in_specs=[pl.BlockSpec((TM, D_MODEL), row), _mod_spec(layer, 0), _mod_spec(layer, 1),
                  pl.BlockSpec((D_MODEL, 3 * D_MODEL), lambda i: (0, 0))],
        out_specs=[pl.BlockSpec((TM, D_MODEL), row), pl.BlockSpec((TM, D_MODEL), row)],
        out_shape=[jax.ShapeDtypeStruct((T_TOK, D_MODEL), F32), jax.ShapeDtypeStruct((T_TOK, D_MODEL), F32)],
        compiler_params=_cparams("parallel"),
        name="conv_in",
    )(x, mods, mods, w_in)


def _route(h2, rw_ref, rb_ref, tri_ref, carry_ref, xe_ref, meta_ref, cnt_ref):
    logits = jnp.dot(h2, rw_ref[...], preferred_element_type=F32, precision=lax.Precision.HIGHEST)
    lane_i = lax.broadcasted_iota(jnp.int32, (TM, LANES), 1)
    lane_f = lane_i.astype(F32)
    valid = lane_i < N_EXPERTS
    s = _sigmoid(logits)
    ninf = -jnp.inf
    ssel = jnp.where(valid, s + rb_ref[...], ninf)
    odd1 = (lane_i & 1) == 1
    odd2 = (lane_i & 2) == 2
    xor1 = lambda v: jnp.where(odd1, pltpu.roll(v, 1, 1), pltpu.roll(v, LANES - 1, 1))
    xor2 = lambda v: jnp.where(odd2, pltpu.roll(v, 2, 1), pltpu.roll(v, LANES - 2, 1))
    a = ssel
    b = xor1(a)
    c = xor2(a)
    d = xor1(c)
    top2 = jnp.maximum(jnp.maximum(jnp.maximum(a + b, a + c), jnp.maximum(a + d, b + c)),
                       jnp.maximum(b + d, c + d))
    gs = jnp.where(valid, top2, ninf)
    gmax = jnp.max(gs, axis=1, keepdims=True)
    grp_f = (lane_i >> 2).astype(F32)
    far = 1e9
    gsel = jnp.min(jnp.where(gs == gmax, grp_f, far), axis=1, keepdims=True)
    masked = jnp.where(grp_f == gsel, ssel, ninf)
    m1 = jnp.max(masked, axis=1, keepdims=True)
    i1 = jnp.min(jnp.where(masked == m1, lane_f, far), axis=1, keepdims=True)
    masked2 = jnp.where(lane_f == i1, ninf, masked)
    m2 = jnp.max(masked2, axis=1, keepdims=True)
    i2 = jnp.min(jnp.where(masked2 == m2, lane_f, far), axis=1, keepdims=True)
    lo = jnp.minimum(i1, i2)
    hi = jnp.maximum(i1, i2)
    s_lo = jnp.sum(jnp.where(lane_f == lo, s, 0.0), axis=1, keepdims=True)
    s_hi = jnp.sum(jnp.where(lane_f == hi, s, 0.0), axis=1, keepdims=True)
    den = s_lo + s_hi
    g_lo = s_lo / den
    g_hi = s_hi / den
    lo_l = lo - EXPERTS_PER_GROUP * gsel
    hi_l = hi - EXPERTS_PER_GROUP * gsel
    pair = lo_l * (7.0 - lo_l) * 0.5 + hi_l - lo_l - 1.0
    cls = gsel * PAIRS_PER_GROUP + pair
    onehot = lane_f == cls
    oh_f = jnp.where(onehot, 1.0, 0.0)
    before = jnp.dot(tri_ref[...], oh_f.astype(BF16), preferred_element_type=F32)
    carry = carry_ref[0:1, :]
    rank = jnp.sum(jnp.where(onehot, before + carry, 0.0), axis=1, keepdims=True)
    carry = carry + jnp.sum(oh_f, axis=0, keepdims=True)
    carry_ref[...] = jnp.broadcast_to(carry, carry_ref.shape)
    cnt_ref[...] = jnp.broadcast_to(carry, cnt_ref.shape)
    xe_ref[:, :D_MODEL] = h2
    xe_ref[:, D_MODEL:] = jnp.where(lane_i < LANES // 2, g_lo, g_hi)
    meta = jnp.where(lane_i == 0, cls, jnp.where(lane_i == 1, rank, 0.0))
    meta_ref[...] = jnp.transpose(meta)[0:8, :].astype(jnp.int32)


def _post_kernel(*refs, conv):
    if conv:
        (gb_ref, zz_ref, zp_ref, zn_ref, wc_ref, *refs) = refs
    else:
        (a_ref, *refs) = refs
    (w_ref, x_ref, ga_ref, shf_ref, scf_ref, lg_ref, lb_ref, rw_ref, rb_ref,
     x1_ref, xe_ref, meta_ref, cnt_ref, tri_ref, carry_ref) = refs
    i = pl.program_id(0)

    @pl.when(i == 0)
    def _():
        carry_ref[...] = jnp.zeros_like(carry_ref)
        rr = lax.broadcasted_iota(jnp.int32, (TM, TM), 0)
        cc = lax.broadcasted_iota(jnp.int32, (TM, TM), 1)
        tri_ref[...] = jnp.where(cc < rr, 1.0, 0.0).astype(BF16)

    if conv:
        zz = zz_ref[...]
        prev_ok = i >= CTX_TILES + 1
        next_ok = jnp.logical_and(i >= CTX_TILES, i <= N_TILES - 2)
        z_before = jnp.where(prev_ok, zp_ref[7:8, :], 0.0)
        z_after = jnp.where(next_ok, zn_ref[0:1, :], 0.0)
        rows = lax.broadcasted_iota(jnp.int32, (TM, 1), 0)
        zm1 = jnp.where(rows == 0, z_before, pltpu.roll(zz, 1, 0))
        zp1 = jnp.where(rows == TM - 1, z_after, pltpu.roll(zz, TM - 1, 0))
        zc = wc_ref[0:1, :] * zm1 + wc_ref[1:2, :] * zz + wc_ref[2:3, :] * zp1
        a = (gb_ref[...] * zc).astype(BF16)
    else:
        a = a_ref[...]
    y = jnp.dot(a, w_ref[...], preferred_element_type=F32)
    x1 = _layer_norm(ALPHA * x_ref[...] + _mod_row(ga_ref) * y, lg_ref[...], lb_ref[...])
    x1_ref[...] = x1
    h2 = x1 * (1.0 + _mod_row(scf_ref)) + _mod_row(shf_ref)
    _route(h2, rw_ref, rb_ref, tri_ref, carry_ref, xe_ref, meta_ref, cnt_ref)


def _post(mix, w_out, x, mods, layer, ln_g, ln_b, rw, rb, conv):
    row = lambda i: (i, 0)
    fix2 = lambda i: (0, 0)
    tile = pl.BlockSpec((TM, D_MODEL), row)
    if conv:
        gb, zz, wc = mix
        sub = TM // 8
        mix_args = (gb, zz, zz, zz, wc)
        mix_specs = [tile, tile,
                     pl.BlockSpec((8, D_MODEL), lambda i: (jnp.maximum(i * sub - 1, 0), 0)),
                     pl.BlockSpec((8, D_MODEL), lambda i: (jnp.minimum((i + 1) * sub, T_TOK // 8 - 1), 0)),
                     pl.BlockSpec((8, D_MODEL), fix2)]
    else:
        mix_args = (mix,)
        mix_specs = [tile]
    vec = pl.BlockSpec((1, D_MODEL), fix2)
    return pl.pallas_call(
        functools.partial(_post_kernel, conv=conv),
        grid=(N_TILES,),
        in_specs=mix_specs + [
            pl.BlockSpec((D_MODEL, D_MODEL), fix2), tile,
            _mod_spec(layer, 2), _mod_spec(layer, 3), _mod_spec(layer, 4),
            vec, vec,
            pl.BlockSpec((D_MODEL, LANES), fix2), pl.BlockSpec((1, LANES), fix2),
        ],
        out_specs=[tile, pl.BlockSpec((TM, XE_W), row), pl.BlockSpec((None, 8, TM), lambda i: (i, 0, 0)),
                   pl.BlockSpec((8, LANES), fix2)],
        out_shape=[jax.ShapeDtypeStruct((T_TOK, D_MODEL), F32),
                   jax.ShapeDtypeStruct((T_TOK, XE_W), F32),
                   jax.ShapeDtypeStruct((N_TILES, 8, TM), jnp.int32),
                   jax.ShapeDtypeStruct((8, LANES), F32)],
        scratch_shapes=[pltpu.VMEM((TM, TM), BF16), pltpu.VMEM((8, LANES), F32)],
        compiler_params=_cparams("arbitrary"),
        name="post_conv" if conv else "post",
    )(*mix_args, w_out, x, mods, mods, mods, ln_g, ln_b, rw, rb)


def _scatter_kernel(pstart_ref, cnt_ref, meta_ref, xe_ref, xs_ref, dest_ref, zero_ref, sem):
    def row_copy(src, d):
        return pltpu.make_async_copy(src, xs_ref.at[pl.ds(d, 1)], sem)

    for g in range(SCAT_TILES):
        def issue(j, carry, g=g):
            d = pstart_ref[meta_ref[g, 0, j]] + meta_ref[g, 1, j]
            dest_ref[g, 0, j] = d
            row_copy(xe_ref.at[pl.ds(g * TM + j, 1)], d).start()
            return carry

        lax.fori_loop(0, TM, issue, 0, unroll=8)

    def drain(j, carry):
        row_copy(xe_ref.at[pl.ds(0, 1)], 0).wait()
        return carry

    lax.fori_loop(0, SCAT_TILES * TM, drain, 0, unroll=8)

    @pl.when(pl.program_id(0) == pl.num_programs(0) - 1)
    def _():
        zero_ref[...] = jnp.zeros_like(zero_ref)

        def per_class(c, carry):
            n = cnt_ref[c]
            first = pstart_ref[c] + n
            n_pad = jnp.bitwise_and(n + (MOE_BLOCK - 1), -MOE_BLOCK) - n

            def fill(k, cc):
                row_copy(zero_ref.at[pl.ds(0, 1)], first + k).start()
                return cc

            def fill_wait(k, cc):
                row_copy(zero_ref.at[pl.ds(0, 1)], 0).wait()
                return cc

            lax.fori_loop(0, n_pad, fill, 0)
            lax.fori_loop(0, n_pad, fill_wait, 0)
            return carry

        lax.fori_loop(0, N_CLASSES, per_class, 0)

        last = N_CLASSES - 1
        rows_used = pstart_ref[last] + jnp.bitwise_and(cnt_ref[last] + (MOE_BLOCK - 1), -MOE_BLOCK)
        first_free = lax.shift_right_logical(rows_used, MOE_BLOCK.bit_length() - 1)

        def block_copy(blk):
            return pltpu.make_async_copy(zero_ref, xs_ref.at[pl.ds(blk * MOE_BLOCK, MOE_BLOCK)], sem)

        def fill_block(blk, cc):
            block_copy(blk).start()
            return cc

        def fill_block_wait(blk, cc):
            block_copy(0).wait()
            return cc

        lax.fori_loop(first_free, N_MOE_BLOCKS, fill_block, 0)
        lax.fori_loop(first_free, N_MOE_BLOCKS, fill_block_wait, 0)


def _scatter_rows(pstart, cnt, meta, xe):
    grid_spec = pltpu.PrefetchScalarGridSpec(
        num_scalar_prefetch=2,
        grid=(N_TILES // SCAT_TILES,),
        in_specs=[pl.BlockSpec((SCAT_TILES, 8, TM), lambda i, ps, cn: (i, 0, 0), memory_space=pltpu.SMEM),
                  pl.BlockSpec((SCAT_TILES * TM, XE_W), lambda i, ps, cn: (i, 0))],
        out_specs=[pl.BlockSpec(memory_space=pl.ANY),
                   pl.BlockSpec((SCAT_TILES, 1, TM), lambda i, ps, cn: (i, 0, 0), memory_space=pltpu.SMEM)],
        scratch_shapes=[pltpu.VMEM((MOE_BLOCK, XE_W), F32), pltpu.SemaphoreType.DMA(())],
    )
    return pl.pallas_call(
        _scatter_kernel,
        grid_spec=grid_spec,
        out_shape=[jax.ShapeDtypeStruct((P_ROWS, XE_W), F32),
                   jax.ShapeDtypeStruct((N_TILES, 1, TM), jnp.int32)],
        compiler_params=_cparams("arbitrary"),
        name="scatter_rows",
    )(pstart, cnt, meta, xe)


def _expert_kernel(ea_ref, eb_ref, nused_ref, x_ref, w1a_ref, w3a_ref, w2a_ref, w1b_ref, w3b_ref, w2b_ref,
                   o_ref, w1a_s, w3a_s, w2a_s, w1b_s, w3b_s, w2b_s):
    b = pl.program_id(0)
    prev = jnp.maximum(b - 1, 0)
    first = b == 0

    @pl.when(jnp.logical_or(first, ea_ref[b] != ea_ref[prev]))
    def _():
        w1a_s[...] = w1a_ref[...].astype(BF16)
        w3a_s[...] = w3a_ref[...].astype(BF16)
        w2a_s[...] = w2a_ref[...].astype(BF16)

    @pl.when(jnp.logical_or(first, eb_ref[b] != eb_ref[prev]))
    def _():
        w1b_s[...] = w1b_ref[...].astype(BF16)
        w3b_s[...] = w3b_ref[...].astype(BF16)
        w2b_s[...] = w2b_ref[...].astype(BF16)

    used = b < nused_ref[0]

    @pl.when(jnp.logical_not(used))
    def _():
        o_ref[...] = jnp.zeros_like(o_ref)

    @pl.when(used)
    def _():
        xb = x_ref[:, :D_MODEL].astype(BF16)
        gates = x_ref[:, D_MODEL:]

        def expert(w1, w3, w2):
            h1 = jnp.dot(xb, w1[...], preferred_element_type=F32)
            h3 = jnp.dot(xb, w3[...], preferred_element_type=F32)
            act = (h1 * _sigmoid(h1) * h3).astype(BF16)
            return jnp.dot(act, w2[...], preferred_element_type=F32)

        ya = expert(w1a_s, w3a_s, w2a_s)
        yb = expert(w1b_s, w3b_s, w2b_s)
        o_ref[...] = gates[:, 0:1] * ya + gates[:, LANES // 2:LANES // 2 + 1] * yb


def _experts(ea, eb, nused, xs, w1, w3, w2, layer):
    up = lambda sel: pl.BlockSpec((None, None, D_MODEL, D_EXPERT),
                                  lambda b, ea, eb, nu: (layer, (ea, eb)[sel][b], 0, 0))
    down = lambda sel: pl.BlockSpec((None, None, D_EXPERT, D_MODEL),
                                    lambda b, ea, eb, nu: (layer, (ea, eb)[sel][b], 0, 0))
    rows = lambda b, ea, eb, nu: (jnp.minimum(b, nu[0] - 1), 0)
    grid_spec = pltpu.PrefetchScalarGridSpec(
        num_scalar_prefetch=3,
        grid=(N_MOE_BLOCKS,),
        in_specs=[pl.BlockSpec((MOE_BLOCK, XE_W), rows),
                  up(0), up(0), down(0), up(1), up(1), down(1)],
        out_specs=pl.BlockSpec((MOE_BLOCK, D_MODEL), lambda b, ea, eb, nu: (b, 0)),
        scratch_shapes=[pltpu.VMEM((D_MODEL, D_EXPERT), BF16), pltpu.VMEM((D_MODEL, D_EXPERT), BF16),
                        pltpu.VMEM((D_EXPERT, D_MODEL), BF16),
                        pltpu.VMEM((D_MODEL, D_EXPERT), BF16), pltpu.VMEM((D_MODEL, D_EXPERT), BF16),
                        pltpu.VMEM((D_EXPERT, D_MODEL), BF16)],
    )
    return pl.pallas_call(
        _expert_kernel,
        grid_spec=grid_spec,
        out_shape=jax.ShapeDtypeStruct((P_ROWS, D_MODEL), F32),
        compiler_params=_cparams("arbitrary"),
        name="expert_pairs",
    )(ea, eb, nused, xs, w1, w3, w2, w1, w3, w2)


def _ln2_kernel(dest_ref, dest_next_ref, ys_ref, x_ref, gf_ref, lg_ref, lb_ref, o_ref, ybuf, sems):
    i = pl.program_id(0)
    slot = i % 2

    def row_copy(src_row, dst_slot, j):
        return pltpu.make_async_copy(ys_ref.at[pl.ds(src_row, 1)], ybuf.at[dst_slot, pl.ds(j, 1)],
                                     sems.at[dst_slot])

    def issue_tile(idx_ref, dst_slot):
        def issue(j, carry):
            row_copy(idx_ref[0, j], dst_slot, j).start()
            return carry

        lax.fori_loop(0, TM, issue, 0, unroll=8)

    @pl.when(i == 0)
    def _():
        issue_tile(dest_ref, 0)

    @pl.when(i + 1 < pl.num_programs(0))
    def _():
        issue_tile(dest_next_ref, 1 - slot)

    def drain(j, carry):
        row_copy(0, slot, 0).wait()
        return carry

    lax.fori_loop(0, TM, drain, 0, unroll=8)
    o_ref[...] = _layer_norm(ALPHA * x_ref[...] + _mod_row(gf_ref) * ybuf[slot], lg_ref[...], lb_ref[...])


def _ln2(x1, ys, dest, mods, layer, ln_g, ln_b):
    row = lambda i: (i, 0)
    tile = pl.BlockSpec((TM, D_MODEL), row)
    vec = pl.BlockSpec((1, D_MODEL), lambda i: (0, 0))
    idx = lambda f: pl.BlockSpec((None, 1, TM), f, memory_space=pltpu.SMEM)
    return pl.pallas_call(
        _ln2_kernel,
        grid=(N_TILES,),
        in_specs=[idx(lambda i: (i, 0, 0)), idx(lambda i: (jnp.minimum(i + 1, N_TILES - 1), 0, 0)),
                  pl.BlockSpec(memory_space=pl.ANY), tile, _mod_spec(layer, 5), vec, vec],
        out_specs=tile,
        out_shape=jax.ShapeDtypeStruct((T_TOK, D_MODEL), F32),
        scratch_shapes=[pltpu.VMEM((2, TM, D_MODEL), F32), pltpu.SemaphoreType.DMA((2,))],
        compiler_params=_cparams("arbitrary"),
        name="moe_gather_residual_ln",
    )(dest, dest, ys, x1, mods, ln_g, ln_b)


def _class_tables():
    lo, hi = [], []
    for g in range(N_EXPERT_GROUPS):
        for a in range(EXPERTS_PER_GROUP):
            for b in range(a + 1, EXPERTS_PER_GROUP):
                lo.append(g * EXPERTS_PER_GROUP + a)
                hi.append(g * EXPERTS_PER_GROUP + b)
    return jnp.array(lo, jnp.int32), jnp.array(hi, jnp.int32)


def _dispatch_plan(counts):
    cnt = counts[0, :N_CLASSES].astype(jnp.int32)
    padded = (cnt + MOE_BLOCK - 1) // MOE_BLOCK * MOE_BLOCK
    pad_end = jnp.cumsum(padded)
    pad_start = pad_end - padded
    nused = pad_end[-1:] // MOE_BLOCK
    blk_row = jnp.minimum(jnp.arange(N_MOE_BLOCKS, dtype=jnp.int32), nused - 1) * MOE_BLOCK
    blk_cls = jnp.sum((pad_end[None, :] <= blk_row[:, None]).astype(jnp.int32), axis=1)
    cls_lo, cls_hi = _class_tables()
    return pad_start, cnt, cls_lo[blk_cls], cls_hi[blk_cls], nused


def _rope_tables():
    pos = jnp.arange(SEQ)
    row = (pos // GRID_W).astype(F32)
    col = (pos % GRID_W).astype(F32)
    freqs = jnp.power(ROPE_BASE, -jnp.arange(ROPE_FREQS, dtype=F32) / ROPE_FREQS)
    ar = row[:, None] * freqs[None, :]
    ac = col[:, None] * freqs[None, :]
    ang = jnp.concatenate([ar, ar, ac, ac], axis=-1)
    cos, sin = jnp.cos(ang), jnp.sin(ang)
    cos = jnp.concatenate([jnp.ones((CTX_LEN, HEAD_DIM), F32), cos], axis=0)
    sin = jnp.concatenate([jnp.zeros((CTX_LEN, HEAD_DIM), F32), sin], axis=0)
    first_half = (jnp.arange(HEAD_DIM) % (2 * ROPE_FREQS)) < ROPE_FREQS
    sa = jnp.where(first_half[None, :], -sin, 0.0)
    sb = jnp.where(first_half[None, :], 0.0, sin)
    two = lambda t: jnp.concatenate([t, t], axis=-1)
    return two(cos), two(sa), two(sb)


def _dup_heads(w):
    w4 = w.reshape(D_MODEL, N_KV_HEADS, 1, HEAD_DIM)
    return jnp.broadcast_to(w4, (D_MODEL, N_KV_HEADS, LANES // HEAD_DIM, HEAD_DIM)).reshape(D_MODEL, -1)


def kernel(x, c, ctx, c_ctx, w_mod, b_mod, ln1_g, ln1_b, ln2_g, ln2_b, router_w, router_bias, moe_w1, moe_w3, moe_w2, a_w_qkv, a_w_o, a_sink, b_w_in, b_b_in, b_ln_g, b_ln_b, b_w_s, b_b_s, b_w_out, c_w_in, c_w_conv, c_w_out):
    assert x.shape == (1, SEQ, D_MODEL) and ctx.shape == (1, CTX_LEN, D_MODEL)
    tok = jnp.concatenate([ctx[0], x[0]], axis=0)
    cc = jnp.zeros((8, D_MODEL), F32).at[0].set(c[0]).at[1].set(c_ctx)
    mods = _modulation(cc, w_mod, b_mod)
    cos, sa, sb = _rope_tables()
    rw = jnp.pad(router_w, ((0, 0), (0, LANES - N_EXPERTS)))
    rb = jnp.pad(router_bias, (0, LANES - N_EXPERTS)).reshape(1, LANES)
    qd = N_HEADS * HEAD_DIM
    kd = N_KV_HEADS * HEAD_DIM

    for i in range(DEPTH):
        kind, j = i % N_MIXERS, i // N_MIXERS
        if kind == 0:
            w = a_w_qkv[j]
            w_all = jnp.concatenate([w[:, :qd], _dup_heads(w[:, qd:qd + kd]), _dup_heads(w[:, qd + kd:])],
                                    axis=1).astype(BF16)
            q, k2, v2 = _qkv_project(tok, mods, i, w_all, cos, sa, sb)
            mix = _attention(a_sink[j], q, k2, v2)
            w_out = a_w_o[j]
        elif kind == 1:
            bs = jnp.repeat(b_b_s[j], D_MODEL // GMLP_GROUPS, axis=1)
            mix = _gmlp_mixer(tok, mods, i, b_w_in[j].astype(BF16), b_b_in[j].reshape(1, -1),
                              b_ln_g[j].reshape(1, -1), b_ln_b[j].reshape(1, -1), b_w_s[j].astype(BF16), bs)
            w_out = b_w_out[j]
        else:
            gb, zz = _conv_in(tok, mods, i, c_w_in[j].astype(BF16))
            mix = (gb, zz, jnp.pad(c_w_conv[j], ((0, 5), (0, 0))))
            w_out = c_w_out[j]
        x1, xe, meta, counts = _post(mix, w_out.astype(BF16), tok, mods, i,
                                     ln1_g[i].reshape(1, -1), ln1_b[i].reshape(1, -1), rw, rb, conv=(kind == 2))
        pad_start, cnt, ea, eb, nused = _dispatch_plan(counts)
        xs, dest = _scatter_rows(pad_start, cnt, meta, xe)
        ys = _experts(ea, eb, nused, xs, moe_w1, moe_w3, moe_w2, i)
        tok = _ln2(x1, ys, dest, mods, i, ln2_g[i].reshape(1, -1), ln2_b[i].reshape(1, -1))
    return tok[CTX_LEN:].reshape(1, SEQ, D_MODEL)
```

```python
import functools

import jax
import jax.numpy as jnp
from jax import lax
from jax.experimental import pallas as pl
from jax.experimental.pallas import tpu as pltpu

F32 = jnp.float32
BF16 = jnp.bfloat16

D_MODEL = 1024
SEQ = 16384
DEPTH = 4
GRID_W = 64
CTX_LEN = 256
N_MIXERS = 3
N_HEADS = 16
N_KV_HEADS = 4
HEAD_DIM = 64
ATTN_BLOCK = 128
ATTN_SCALE = HEAD_DIM ** -0.5
ROPE_BASE = 10000.0
ROPE_FREQS = HEAD_DIM // 4
CHUNK = 128
GMLP_GROUPS = 8
N_EXPERTS = 16
N_EXPERT_GROUPS = 4
EXPERTS_PER_GROUP = 4
D_EXPERT = 512
ALPHA = (2 * DEPTH) ** 0.25
LN_EPS = 1e-5

LANES = 128
T_TOK = CTX_LEN + SEQ
TM = 256
N_TILES = T_TOK // TM
CTX_TILES = CTX_LEN // TM
N_ABLK = T_TOK // ATTN_BLOCK
CTX_ABLK = CTX_LEN // ATTN_BLOCK
PAIRS_PER_GROUP = 6
N_CLASSES = N_EXPERT_GROUPS * PAIRS_PER_GROUP
CLASS_ROWS = 32
MOE_BLOCK = 128
N_STAGING = 2
N_MOE_BLOCKS = T_TOK // MOE_BLOCK + N_CLASSES
P_ROWS = N_MOE_BLOCKS * MOE_BLOCK
XE_W = D_MODEL + LANES
SCAT_TILES = 5
NEG_BIG = -1e30
VMEM_LIMIT = 52 * 1024 * 1024


def _cparams(sem="arbitrary"):
    return pltpu.CompilerParams(dimension_semantics=(sem,), vmem_limit_bytes=VMEM_LIMIT)


def _mod_row(ref):
    is_ctx = pl.program_id(0) < CTX_TILES
    return jnp.where(is_ctx, ref[1:2, :], ref[0:1, :])


def _layer_norm(x, g, b):
    mu = jnp.mean(x, axis=-1, keepdims=True)
    xc = x - mu
    var = jnp.mean(xc * xc, axis=-1, keepdims=True)
    return xc * lax.rsqrt(var + LN_EPS) * g + b


def _sigmoid(x):
    return 1.0 / (1.0 + jnp.exp(-x))


def _mod_kernel(cc_ref, w_ref, b_ref, o_ref):
    cc = cc_ref[...]
    act = cc * _sigmoid(cc)
    o_ref[...] = jnp.dot(act.astype(BF16), w_ref[...].astype(BF16), preferred_element_type=F32) + b_ref[...]


def _modulation(cc, w_mod, b_mod):
    nt = 1536
    return pl.pallas_call(
        _mod_kernel,
        grid=(DEPTH, 6 * D_MODEL // nt),
        in_specs=[
            pl.BlockSpec((8, D_MODEL), lambda l, n: (0, 0)),
            pl.BlockSpec((None, D_MODEL, nt), lambda l, n: (l, 0, n)),
            pl.BlockSpec((None, 1, nt), lambda l, n: (l, 0, n)),
        ],
        out_specs=pl.BlockSpec((None, 8, nt), lambda l, n: (l, 0, n)),
        out_shape=jax.ShapeDtypeStruct((DEPTH, 8, 6 * D_MODEL), F32),
        compiler_params=pltpu.CompilerParams(
            dimension_semantics=("arbitrary", "arbitrary"), vmem_limit_bytes=VMEM_LIMIT),
        name="modulation",
    )(cc, w_mod, b_mod.reshape(DEPTH, 1, 6 * D_MODEL))


def _mod_spec(layer, chunk):
    return pl.BlockSpec((None, 8, D_MODEL), lambda i: (layer, 0, chunk))


def _store_split_heads(ref, tile, pair):
    low = lax.broadcasted_iota(jnp.int32, tile.shape, 1) < HEAD_DIM
    swapped = pltpu.roll(tile, HEAD_DIM, 1)
    parts = ((jnp.where(low, tile, 0.0), jnp.where(low, 0.0, swapped)),
             (jnp.where(low, swapped, 0.0), jnp.where(low, 0.0, tile)))
    blk = ATTN_BLOCK
    for h, (in_low, in_high) in enumerate(parts):
        cols = slice((2 * pair + h) * LANES, (2 * pair + h + 1) * LANES)
        for n in range(TM // blk):
            rows = slice(n * blk, (n + 1) * blk)
            ref[2 * n * blk:(2 * n + 1) * blk, cols] = in_low[rows].astype(BF16)
            ref[(2 * n + 1) * blk:(2 * n + 2) * blk, cols] = in_high[rows].astype(BF16)


def _qkv_kernel(x_ref, sh_ref, sc_ref, w_ref, cos_ref, sa_ref, sb_ref, q_ref, k_ref, v_ref):
    h = (x_ref[...] * (1.0 + _mod_row(sc_ref)) + _mod_row(sh_ref)).astype(BF16)
    y = jnp.dot(h, w_ref[...], preferred_element_type=F32)
    cos, sa, sb = cos_ref[...], sa_ref[...], sb_ref[...]
    n_q = D_MODEL // LANES
    n_k = N_KV_HEADS * HEAD_DIM // LANES
    for t in range(n_q + n_k):
        yt = y[:, t * LANES:(t + 1) * LANES]
        r = yt * cos + pltpu.roll(yt, LANES - ROPE_FREQS, 1) * sa + pltpu.roll(yt, ROPE_FREQS, 1) * sb
        if t < n_q:
            q_ref[:, t * LANES:(t + 1) * LANES] = (r * ATTN_SCALE).astype(BF16)
        else:
            _store_split_heads(k_ref, r, t - n_q)
    for t in range(n_k):
        _store_split_heads(v_ref, y[:, (n_q + n_k + t) * LANES:(n_q + n_k + t + 1) * LANES], t)


def _qkv_project(x, mods, layer, w, cos, sa, sb):
    kvw = N_KV_HEADS * LANES
    row = lambda i: (i, 0)
    return pl.pallas_call(
        _qkv_kernel,
        grid=(N_TILES,),
        in_specs=[
            pl.BlockSpec((TM, D_MODEL), row),
            _mod_spec(layer, 0), _mod_spec(layer, 1),
            pl.BlockSpec(w.shape, lambda i: (0, 0)),
            pl.BlockSpec((TM, LANES), row), pl.BlockSpec((TM, LANES), row), pl.BlockSpec((TM, LANES), row),
        ],
        out_specs=[pl.BlockSpec((TM, D_MODEL), row), pl.BlockSpec((2 * TM, kvw), row),
                   pl.BlockSpec((2 * TM, kvw), row)],
        out_shape=[jax.ShapeDtypeStruct((T_TOK, D_MODEL), BF16),
                   jax.ShapeDtypeStruct((2 * T_TOK, kvw), BF16),
                   jax.ShapeDtypeStruct((2 * T_TOK, kvw), BF16)],
        compiler_params=_cparams("parallel"),
        name="qkv_rope",
    )(x, mods, mods, w, cos, sa, sb)


def _attn_kernel(sink_ref, q_ref, kp_ref, kc_ref, kn_ref, kx_ref, vp_ref, vc_ref, vn_ref, vx_ref, o_ref):
    i = pl.program_id(0)
    blk = ATTN_BLOCK
    prev_ok = i >= CTX_ABLK + 1
    cur_ok = i >= CTX_ABLK
    next_ok = jnp.logical_and(i >= CTX_ABLK, i <= N_ABLK - 2)
    r = lax.broadcasted_iota(jnp.int32, (2 * blk, 2 * blk), 0) & (blk - 1)
    c = lax.broadcasted_iota(jnp.int32, (2 * blk, 2 * blk), 1) & (blk - 1)
    m_prev = jnp.logical_and(c >= r, prev_ok)
    m_cur = jnp.logical_and(c >= 0, cur_ok)
    m_next = jnp.logical_and(c <= r, next_ok)
    top_rows = lax.broadcasted_iota(jnp.int32, (2 * blk, 1), 0) < blk
    even_cols = lax.broadcasted_iota(jnp.int32, (2 * blk, 2 * blk), 1) < blk
    even_lanes = lax.broadcasted_iota(jnp.int32, (2 * blk, LANES), 1) < HEAD_DIM
    nt_dims = (((1,), (1,)), ((), ()))

    def halves(s):
        return s[:, :blk], s[:, blk:]

    for j in range(N_KV_HEADS):
        ks = slice(j * LANES, (j + 1) * LANES)
        lhs = jnp.concatenate([q_ref[:, 2 * j * LANES:(2 * j + 1) * LANES],
                               q_ref[:, (2 * j + 1) * LANES:(2 * j + 2) * LANES]], axis=0)
        s_p = jnp.where(m_prev, lax.dot_general(lhs, kp_ref[:, ks], nt_dims, preferred_element_type=F32), NEG_BIG)
        s_c = jnp.where(m_cur, lax.dot_general(lhs, kc_ref[:, ks], nt_dims, preferred_element_type=F32), NEG_BIG)
        s_n = jnp.where(m_next, lax.dot_general(lhs, kn_ref[:, ks], nt_dims, preferred_element_type=F32), NEG_BIG)
        s_x = lax.dot_general(lhs, kx_ref[:, ks], nt_dims, preferred_element_type=F32)
        s_x0, s_x1 = s_x[:, :2 * blk], s_x[:, 2 * blk:]
        pieces = (s_p, s_c, s_n, s_x0, s_x1)
        sink_e = jnp.where(top_rows, sink_ref[4 * j], sink_ref[4 * j + 2])
        sink_o = jnp.where(top_rows, sink_ref[4 * j + 1], sink_ref[4 * j + 3])
        tile_max = functools.reduce(jnp.maximum, pieces)
        tm_e, tm_o = halves(tile_max)
        m_e = jnp.maximum(jnp.max(tm_e, axis=1, keepdims=True), sink_e)
        m_o = jnp.maximum(jnp.max(tm_o, axis=1, keepdims=True), sink_o)
        m = jnp.where(even_cols, m_e, m_o)
        probs = [jnp.exp(s - m) for s in pieces]
        ts_e, ts_o = halves(functools.reduce(jnp.add, probs))
        l_e = jnp.sum(ts_e, axis=1, keepdims=True) + jnp.exp(sink_e - m_e)
        l_o = jnp.sum(ts_o, axis=1, keepdims=True) + jnp.exp(sink_o - m_o)
        p_p, p_c, p_n, p_x0, p_x1 = [p.astype(BF16) for p in probs]
        p_x = jnp.concatenate([p_x0, p_x1], axis=1)
        o = (jnp.dot(p_p, vp_ref[:, ks], preferred_element_type=F32)
             + jnp.dot(p_c, vc_ref[:, ks], preferred_element_type=F32)
             + jnp.dot(p_n, vn_ref[:, ks], preferred_element_type=F32)
             + jnp.dot(p_x, vx_ref[:, ks], preferred_element_type=F32))
        o = (o / jnp.where(even_lanes, l_e, l_o)).astype(BF16)
        o_ref[:, 2 * j * LANES:(2 * j + 1) * LANES] = o[:blk]
        o_ref[:, (2 * j + 1) * LANES:(2 * j + 2) * LANES] = o[blk:]


def _attention(sink, q, kb, vb):
    kvw = N_KV_HEADS * LANES
    blk = ATTN_BLOCK
    lo, hi = CTX_ABLK, N_ABLK - 1
    prev_map = lambda i: (jnp.clip(i - 1, lo, hi), 0)
    cur_map = lambda i: (i, 0)
    next_map = lambda i: (jnp.clip(i + 1, lo, hi), 0)
    ctx_map = lambda i: (0, 0)
    kv_specs = [pl.BlockSpec((2 * blk, kvw), prev_map), pl.BlockSpec((2 * blk, kvw), cur_map),
                pl.BlockSpec((2 * blk, kvw), next_map), pl.BlockSpec((2 * CTX_LEN, kvw), ctx_map)]
    return pl.pallas_call(
        _attn_kernel,
        grid=(N_ABLK,),
        in_specs=[pl.BlockSpec(memory_space=pltpu.SMEM), pl.BlockSpec((blk, D_MODEL), cur_map)]
        + kv_specs + kv_specs,
        out_specs=pl.BlockSpec((blk, D_MODEL), cur_map),
        out_shape=jax.ShapeDtypeStruct((T_TOK, D_MODEL), BF16),
        compiler_params=_cparams("parallel"),
        name="window_attention",
    )(sink, q, kb, kb, kb, kb, vb, vb, vb, vb)


def _gmlp_kernel(x_ref, sh_ref, sc_ref, w_ref, b_ref, g_ref, be_ref, ws_ref, bs_ref, o_ref):
    h = (x_ref[...] * (1.0 + _mod_row(sc_ref)) + _mod_row(sh_ref)).astype(BF16)
    z = jax.nn.gelu(jnp.dot(h, w_ref[...], preferred_element_type=F32) + b_ref[...], approximate=True)
    u = z[:, :D_MODEL]
    v = _layer_norm(z[:, D_MODEL:], g_ref[...], be_ref[...]).astype(BF16)
    gd = D_MODEL // GMLP_GROUPS
    for n in range(TM // CHUNK):
        rows = slice(n * CHUNK, (n + 1) * CHUNK)
        for g in range(GMLP_GROUPS):
            cols = slice(g * gd, (g + 1) * gd)
            s = jnp.dot(ws_ref[g], v[rows, cols], preferred_element_type=F32) + bs_ref[:, cols]
            o_ref[rows, cols] = (u[rows, cols] * s).astype(BF16)


def _gmlp_mixer(x, mods, layer, w_in, b_in, ln_g, ln_b, w_s, b_s):
    row = lambda i: (i, 0)
    fix2 = lambda i: (0, 0)
    return pl.pallas_call(
        _gmlp_kernel,
        grid=(N_TILES,),
        in_specs=[
            pl.BlockSpec((TM, D_MODEL), row),
            _mod_spec(layer, 0), _mod_spec(layer, 1),
            pl.BlockSpec((D_MODEL, 2 * D_MODEL), fix2),
            pl.BlockSpec((1, 2 * D_MODEL), fix2),
            pl.BlockSpec((1, D_MODEL), fix2), pl.BlockSpec((1, D_MODEL), fix2),
            pl.BlockSpec((GMLP_GROUPS, CHUNK, CHUNK), lambda i: (0, 0, 0)),
            pl.BlockSpec((CHUNK, D_MODEL), fix2),
        ],
        out_specs=pl.BlockSpec((TM, D_MODEL), row),
        out_shape=jax.ShapeDtypeStruct((T_TOK, D_MODEL), BF16),
        compiler_params=_cparams("parallel"),
        name="gmlp_mixer",
    )(x, mods, mods, w_in, b_in, ln_g, ln_b, w_s, b_s)


def _conv_in_kernel(x_ref, sh_ref, sc_ref, w_ref, gb_ref, zz_ref):
    h = (x_ref[...] * (1.0 + _mod_row(sc_ref)) + _mod_row(sh_ref)).astype(BF16)
    proj = jnp.dot(h, w_ref[...], preferred_element_type=F32)
    gb_ref[...] = proj[:, :D_MODEL]
    zz_ref[...] = proj[:, D_MODEL:2 * D_MODEL] * proj[:, 2 * D_MODEL:]


def _conv_in(x, mods, layer, w_in):
    row = lambda i: (i, 0)
    return pl.pallas_call(
        _conv_in_kernel,
        grid=(N_TILES,),
        in_specs=[pl.BlockSpec((TM, D_MODEL), row), _mod_spec(layer, 0), _mod_spec(layer, 1),
                  pl.BlockSpec((D_MODEL, 3 * D_MODEL), lambda i: (0, 0))],
        out_specs=[pl.BlockSpec((TM, D_MODEL), row), pl.BlockSpec((TM, D_MODEL), row)],
        out_shape=[jax.ShapeDtypeStruct((T_TOK, D_MODEL), F32), jax.ShapeDtypeStruct((T_TOK, D_MODEL), F32)],
        compiler_params=_cparams("parallel"),
        name="conv_in",
    )(x, mods, mods, w_in)


def _route(h2, rw_ref, rb_ref, tri_ref, carry_ref, xe_ref, meta_ref, cnt_ref):
    logits = jnp.dot(h2.astype(BF16), rw_ref[...], preferred_element_type=F32)
    s = _sigmoid(jnp.transpose(logits)[:N_EXPERTS, :])
    ssel = s + rb_ref[...]
    row = lambda v, k: v[k:k + 1, :]
    npg = EXPERTS_PER_GROUP

    def first_argmax(vals):
        best, idx = vals[0], jnp.zeros_like(vals[0])
        for k in range(1, len(vals)):
            better = vals[k] > best
            best = jnp.where(better, vals[k], best)
            idx = jnp.where(better, float(k), idx)
        return best, idx

    def pick(vals, idx):
        out = vals[-1]
        for k in range(len(vals) - 2, -1, -1):
            out = jnp.where(idx == float(k), vals[k], out)
        return out

    group_scores = []
    for g in range(N_EXPERT_GROUPS):
        a, b, c, d = (row(ssel, npg * g + k) for k in range(npg))
        group_scores.append(jnp.maximum(jnp.maximum(jnp.maximum(a + b, a + c), jnp.maximum(a + d, b + c)),
                                        jnp.maximum(b + d, c + d)))
    _, gsel = first_argmax(group_scores)
    biased = [pick([row(ssel, npg * g + k) for g in range(N_EXPERT_GROUPS)], gsel) for k in range(npg)]
    plain = [pick([row(s, npg * g + k) for g in range(N_EXPERT_GROUPS)], gsel) for k in range(npg)]
    _, i1 = first_argmax(biased)
    _, i2 = first_argmax([jnp.where(i1 == float(k), -jnp.inf, biased[k]) for k in range(npg)])
    lo = jnp.minimum(i1, i2)
    hi = jnp.maximum(i1, i2)
    s_lo = pick(plain, lo)
    s_hi = pick(plain, hi)
    den = s_lo + s_hi
    g_lo = s_lo / den
    g_hi = s_hi / den
    pair = lo * (7.0 - lo) * 0.5 + hi - lo - 1.0
    cls = gsel * PAIRS_PER_GROUP + pair
    cls_rows = lax.broadcasted_iota(jnp.int32, (CLASS_ROWS, TM), 0).astype(F32)
    onehot = cls_rows == cls
    oh_f = jnp.where(onehot, 1.0, 0.0)
    before = jnp.dot(oh_f.astype(BF16), tri_ref[...], preferred_element_type=F32)
    carry = carry_ref[...]
    rank = jnp.sum(jnp.where(onehot, before + carry[:, 0:1], 0.0), axis=0, keepdims=True)
    carry = carry + jnp.sum(oh_f, axis=1, keepdims=True)
    carry_ref[...] = carry
    cnt_ref[...] = carry
    xe_ref[:, :D_MODEL] = h2
    gate_rows = lax.broadcasted_iota(jnp.int32, (LANES, TM), 0) < LANES // 2
    xe_ref[:, D_MODEL:] = jnp.transpose(jnp.where(gate_rows, g_lo, g_hi))
    meta_rows = lax.broadcasted_iota(jnp.int32, (8, TM), 0)
    meta_ref[...] = jnp.where(meta_rows == 0, cls, jnp.where(meta_rows == 1, rank, 0.0)).astype(jnp.int32)


def _post_kernel(*refs, conv):
    if conv:
        (gb_ref, zz_ref, zp_ref, zn_ref, wc_ref, *refs) = refs
    else:
        (a_ref, *refs) = refs
    (w_ref, x_ref, ga_ref, shf_ref, scf_ref, lg_ref, lb_ref, rw_ref, rb_ref,
     x1_ref, xe_ref, meta_ref, cnt_ref, tri_ref, carry_ref) = refs
    i = pl.program_id(0)

    @pl.when(i == 0)
    def _():
        carry_ref[...] = jnp.zeros_like(carry_ref)
        rr = lax.broadcasted_iota(jnp.int32, (TM, TM), 0)
        cc = lax.broadcasted_iota(jnp.int32, (TM, TM), 1)
        tri_ref[...] = jnp.where(rr < cc, 1.0, 0.0).astype(BF16)

    if conv:
        zz = zz_ref[...]
        prev_ok = i >= CTX_TILES + 1
        next_ok = jnp.logical_and(i >= CTX_TILES, i <= N_TILES - 2)
        z_before = jnp.where(prev_ok, zp_ref[7:8, :], 0.0)
        z_after = jnp.where(next_ok, zn_ref[0:1, :], 0.0)
        rows = lax.broadcasted_iota(jnp.int32, (TM, 1), 0)
        zm1 = jnp.where(rows == 0, z_before, pltpu.roll(zz, 1, 0))
        zp1 = jnp.where(rows == TM - 1, z_after, pltpu.roll(zz, TM - 1, 0))
        zc = wc_ref[0:1, :] * zm1 + wc_ref[1:2, :] * zz + wc_ref[2:3, :] * zp1
        a = (gb_ref[...] * zc).astype(BF16)
    else:
        a = a_ref[...]
    y = jnp.dot(a, w_ref[...], preferred_element_type=F32)
    x1 = _layer_norm(ALPHA * x_ref[...] + _mod_row(ga_ref) * y, lg_ref[...], lb_ref[...])
    x1_ref[...] = x1
    h2 = x1 * (1.0 + _mod_row(scf_ref)) + _mod_row(shf_ref)
    _route(h2, rw_ref, rb_ref, tri_ref, carry_ref, xe_ref, meta_ref, cnt_ref)


def _post(mix, w_out, x, mods, layer, ln_g, ln_b, rw, rb, conv):
    row = lambda i: (i, 0)
    fix2 = lambda i: (0, 0)
    tile = pl.BlockSpec((TM, D_MODEL), row)
    if conv:
        gb, zz, wc = mix
        sub = TM // 8
        mix_args = (gb, zz, zz, zz, wc)
        mix_specs = [tile, tile,
                     pl.BlockSpec((8, D_MODEL), lambda i: (jnp.maximum(i * sub - 1, 0), 0)),
                     pl.BlockSpec((8, D_MODEL), lambda i: (jnp.minimum((i + 1) * sub, T_TOK // 8 - 1), 0)),
                     pl.BlockSpec((8, D_MODEL), fix2)]
    else:
        mix_args = (mix,)
        mix_specs = [tile]
    vec = pl.BlockSpec((1, D_MODEL), fix2)
    return pl.pallas_call(
        functools.partial(_post_kernel, conv=conv),
        grid=(N_TILES,),
        in_specs=mix_specs + [
            pl.BlockSpec((D_MODEL, D_MODEL), fix2), tile,
            _mod_spec(layer, 2), _mod_spec(layer, 3), _mod_spec(layer, 4),
            vec, vec,
            pl.BlockSpec((D_MODEL, LANES), fix2), pl.BlockSpec((N_EXPERTS, TM), fix2),
        ],
        out_specs=[tile, pl.BlockSpec((TM, XE_W), row), pl.BlockSpec((None, 8, TM), lambda i: (i, 0, 0)),
                   pl.BlockSpec((CLASS_ROWS, LANES), fix2)],
        out_shape=[jax.ShapeDtypeStruct((T_TOK, D_MODEL), F32),
                   jax.ShapeDtypeStruct((T_TOK, XE_W), F32),
                   jax.ShapeDtypeStruct((N_TILES, 8, TM), jnp.int32),
                   jax.ShapeDtypeStruct((CLASS_ROWS, LANES), F32)],
        scratch_shapes=[pltpu.VMEM((TM, TM), BF16), pltpu.VMEM((CLASS_ROWS, LANES), F32)],
        compiler_params=_cparams("arbitrary"),
        name="post_conv" if conv else "post",
    )(*mix_args, w_out, x, mods, mods, mods, ln_g, ln_b, rw, rb)


def _scatter_kernel(pstart_ref, cnt_ref, dest_ref, xe_ref, xs_ref, zero_ref, sem):
    def row_copy(src, d):
        return pltpu.make_async_copy(src, xs_ref.at[pl.ds(d, 1)], sem)

    for g in range(SCAT_TILES):
        def issue(jo, carry, g=g):
            base = pl.multiple_of(jo * 8, 8)
            for k in range(8):
                row_copy(xe_ref.at[pl.ds(g * TM + base + k, 1)], dest_ref[g, 0, base + k]).start()
            return carry

        lax.fori_loop(0, TM // 8, issue, 0)

    def drain(j, carry):
        row_copy(xe_ref.at[pl.ds(0, 1)], 0).wait()
        return carry

    lax.fori_loop(0, SCAT_TILES * TM, drain, 0, unroll=8)

    @pl.when(pl.program_id(0) == pl.num_programs(0) - 1)
    def _():
        zero_ref[...] = jnp.zeros_like(zero_ref)

        def per_class(c, carry):
            n = cnt_ref[c]
            first = pstart_ref[c] + n
            n_pad = jnp.bitwise_and(n + (MOE_BLOCK - 1), -MOE_BLOCK) - n

            def fill(k, cc):
                row_copy(zero_ref.at[pl.ds(0, 1)], first + k).start()
                return cc

            def fill_wait(k, cc):
                row_copy(zero_ref.at[pl.ds(0, 1)], 0).wait()
                return cc

            lax.fori_loop(0, n_pad, fill, 0)
            lax.fori_loop(0, n_pad, fill_wait, 0)
            return carry

        lax.fori_loop(0, N_CLASSES, per_class, 0)

        last = N_CLASSES - 1
        rows_used = pstart_ref[last] + jnp.bitwise_and(cnt_ref[last] + (MOE_BLOCK - 1), -MOE_BLOCK)
        first_free = lax.shift_right_logical(rows_used, MOE_BLOCK.bit_length() - 1)

        def block_copy(blk):
            return pltpu.make_async_copy(zero_ref, xs_ref.at[pl.ds(blk * MOE_BLOCK, MOE_BLOCK)], sem)

        def fill_block(blk, cc):
            block_copy(blk).start()
            return cc

        def fill_block_wait(blk, cc):
            block_copy(0).wait()
            return cc

        lax.fori_loop(first_free, N_MOE_BLOCKS, fill_block, 0)
        lax.fori_loop(first_free, N_MOE_BLOCKS, fill_block_wait, 0)


def _scatter_rows(pstart, cnt, dest, xe):
    grid_spec = pltpu.PrefetchScalarGridSpec(
        num_scalar_prefetch=2,
        grid=(N_TILES // SCAT_TILES,),
        in_specs=[pl.BlockSpec((SCAT_TILES, 1, TM), lambda i, ps, cn: (i, 0, 0), memory_space=pltpu.SMEM),
                  pl.BlockSpec((SCAT_TILES * TM, XE_W), lambda i, ps, cn: (i, 0))],
        out_specs=pl.BlockSpec(memory_space=pl.ANY),
        scratch_shapes=[pltpu.VMEM((MOE_BLOCK, XE_W), F32), pltpu.SemaphoreType.DMA(())],
    )
    return pl.pallas_call(
        _scatter_kernel,
        grid_spec=grid_spec,
        out_shape=jax.ShapeDtypeStruct((P_ROWS, XE_W), F32),
        compiler_params=_cparams("arbitrary"),
        name="scatter_rows",
    )(pstart, cnt, dest, xe)


def _expert_kernel(lo_ref, hi_ref, need_ref, nused_ref, x_ref, w1_hbm, w3_hbm, w2_hbm, o_ref,
                   st1, st3, st2, w1s, w3s, w2s, sems, done_ref, *, layer):
    b = pl.program_id(0)

    def fetch(k, buf):
        return (pltpu.make_async_copy(w1_hbm.at[layer, k], st1.at[buf], sems.at[buf]),
                pltpu.make_async_copy(w3_hbm.at[layer, k], st3.at[buf], sems.at[buf]),
                pltpu.make_async_copy(w2_hbm.at[layer, k], st2.at[buf], sems.at[buf]))

    @pl.when(b == 0)
    def _():
        done_ref[0] = 0
        for k in range(N_STAGING):
            for cp in fetch(k, k):
                cp.start()

    used = b < nused_ref[0]

    @pl.when(jnp.logical_not(used))
    def _():
        o_ref[...] = jnp.zeros_like(o_ref)

    @pl.when(used)
    def _():
        def install(k, carry):
            buf = k % N_STAGING
            slot = k % EXPERTS_PER_GROUP
            for cp in fetch(k, buf):
                cp.wait()
            w1s[slot] = st1[buf].astype(BF16)
            w3s[slot] = st3[buf].astype(BF16)
            w2s[slot] = st2[buf].astype(BF16)

            @pl.when(k + N_STAGING < N_EXPERTS)
            def _():
                for cp in fetch(k + N_STAGING, buf):
                    cp.start()

            return carry

        done = done_ref[0]
        lax.fori_loop(done, need_ref[b], install, 0)
        done_ref[0] = jnp.maximum(done, need_ref[b])

        xb = x_ref[:, :D_MODEL].astype(BF16)
        gates = x_ref[:, D_MODEL:]

        def expert(slot):
            h1 = jnp.dot(xb, w1s[slot], preferred_element_type=F32)
            h3 = jnp.dot(xb, w3s[slot], preferred_element_type=F32)
            act = (h1 * _sigmoid(h1) * h3).astype(BF16)
            return jnp.dot(act, w2s[slot], preferred_element_type=F32)

        ya = expert(lo_ref[b])
        yb = expert(hi_ref[b])
        o_ref[...] = gates[:, 0:1] * ya + gates[:, LANES // 2:LANES // 2 + 1] * yb

    @pl.when(b == pl.num_programs(0) - 1)
    def _():
        def drain(k, carry):
            for cp in fetch(k, k % N_STAGING):
                cp.wait()
            return carry

        done = done_ref[0]
        lax.fori_loop(done, jnp.minimum(done + N_STAGING, N_EXPERTS), drain, 0)


def _experts(lo_slot, hi_slot, need, nused, xs, w1, w3, w2, layer):
    rows = lambda b, lo, hi, nd, nu: (jnp.maximum(jnp.minimum(b, nu[0] - 1), 0), 0)
    any_spec = pl.BlockSpec(memory_space=pl.ANY)
    up, down = (D_MODEL, D_EXPERT), (D_EXPERT, D_MODEL)
    grid_spec = pltpu.PrefetchScalarGridSpec(
        num_scalar_prefetch=4,
        grid=(N_MOE_BLOCKS,),
        in_specs=[pl.BlockSpec((MOE_BLOCK, XE_W), rows), any_spec, any_spec, any_spec],
        out_specs=pl.BlockSpec((MOE_BLOCK, D_MODEL), lambda b, lo, hi, nd, nu: (b, 0)),
        scratch_shapes=[pltpu.VMEM((N_STAGING,) + up, F32), pltpu.VMEM((N_STAGING,) + up, F32),
                        pltpu.VMEM((N_STAGING,) + down, F32),
                        pltpu.VMEM((EXPERTS_PER_GROUP,) + up, BF16), pltpu.VMEM((EXPERTS_PER_GROUP,) + up, BF16),
                        pltpu.VMEM((EXPERTS_PER_GROUP,) + down, BF16),
                        pltpu.SemaphoreType.DMA((N_STAGING,)), pltpu.SMEM((1,), jnp.int32)],
    )
    return pl.pallas_call(
        functools.partial(_expert_kernel, layer=layer),
        grid_spec=grid_spec,
        out_shape=jax.ShapeDtypeStruct((P_ROWS, D_MODEL), F32),
        compiler_params=_cparams("arbitrary"),
        name="expert_pairs",
    )(lo_slot, hi_slot, need, nused, xs, w1, w3, w2)


def _ln2_kernel(dest_ref, dest_next_ref, ys_ref, x_ref, gf_ref, lg_ref, lb_ref, o_ref, ybuf, sems):
    i = pl.program_id(0)
    slot = i % 2

    def row_copy(src_row, dst_slot, j):
        return pltpu.make_async_copy(ys_ref.at[pl.ds(src_row, 1)], ybuf.at[dst_slot, pl.ds(j, 1)],
                                     sems.at[dst_slot])

    def issue_tile(idx_ref, dst_slot):
        def issue(jo, carry):
            base = pl.multiple_of(jo * 8, 8)
            for k in range(8):
                row_copy(idx_ref[0, base + k], dst_slot, base + k).start()
            return carry

        lax.fori_loop(0, TM // 8, issue, 0)

    @pl.when(i == 0)
    def _():
        issue_tile(dest_ref, 0)

    @pl.when(i + 1 < pl.num_programs(0))
    def _():
        issue_tile(dest_next_ref, 1 - slot)

    def drain(j, carry):
        row_copy(0, slot, 0).wait()
        return carry

    lax.fori_loop(0, TM, drain, 0, unroll=8)
    o_ref[...] = _layer_norm(ALPHA * x_ref[...] + _mod_row(gf_ref) * ybuf[slot], lg_ref[...], lb_ref[...])


def _ln2(x1, ys, dest, mods, layer, ln_g, ln_b, latent_only):
    row = lambda i: (i, 0)
    tile = pl.BlockSpec((TM, D_MODEL), row)
    vec = pl.BlockSpec((1, D_MODEL), lambda i: (0, 0))
    idx = lambda f: pl.BlockSpec((None, 1, TM), f, memory_space=pltpu.SMEM)
    if latent_only:
        out_rows = SEQ
        out_spec = pl.BlockSpec((TM, D_MODEL), lambda i: (jnp.maximum(i - CTX_TILES, 0), 0))
    else:
        out_rows = T_TOK
        out_spec = tile
    return pl.pallas_call(
        _ln2_kernel,
        grid=(N_TILES,),
        in_specs=[idx(lambda i: (i, 0, 0)), idx(lambda i: (jnp.minimum(i + 1, N_TILES - 1), 0, 0)),
                  pl.BlockSpec(memory_space=pl.ANY), tile, _mod_spec(layer, 5), vec, vec],
        out_specs=out_spec,
        out_shape=jax.ShapeDtypeStruct((out_rows, D_MODEL), F32),
        scratch_shapes=[pltpu.VMEM((2, TM, D_MODEL), F32), pltpu.SemaphoreType.DMA((2,))],
        compiler_params=_cparams("arbitrary"),
        name="moe_gather_residual_ln",
    )(dest, dest, ys, x1, mods, ln_g, ln_b)


def _pair_tables():
    pairs = [(a, b) for a in range(EXPERTS_PER_GROUP) for b in range(a + 1, EXPERTS_PER_GROUP)]
    return jnp.array([p[0] for p in pairs], jnp.int32), jnp.array([p[1] for p in pairs], jnp.int32)


def _dispatch_plan(meta, counts):
    cnt = counts[:N_CLASSES, 0].astype(jnp.int32)
    padded = (cnt + MOE_BLOCK - 1) // MOE_BLOCK * MOE_BLOCK
    pad_end = jnp.cumsum(padded)
    pad_start = pad_end - padded
    nused = pad_end[-1:] // MOE_BLOCK
    cls, rank = meta[:, 0:1, :], meta[:, 1:2, :]
    before = (cls[..., None] > jnp.arange(N_CLASSES, dtype=jnp.int32)).astype(jnp.int32)
    dest = rank + jnp.sum(before * padded, axis=-1)
    blk_row = jnp.arange(N_MOE_BLOCKS, dtype=jnp.int32) * MOE_BLOCK
    blk_cls = jnp.minimum(jnp.sum((pad_end[None, :] <= blk_row[:, None]).astype(jnp.int32), axis=1),
                          N_CLASSES - 1)
    pair_lo, pair_hi = _pair_tables()
    lo_slot = pair_lo[blk_cls % PAIRS_PER_GROUP]
    hi_slot = pair_hi[blk_cls % PAIRS_PER_GROUP]
    need = (blk_cls // PAIRS_PER_GROUP) * EXPERTS_PER_GROUP + hi_slot + 1
    return pad_start, cnt, dest, lo_slot, hi_slot, need, nused


def _rope_tables():
    n_rows = SEQ // GRID_W
    freqs = jnp.power(ROPE_BASE, -jnp.arange(ROPE_FREQS, dtype=F32) / ROPE_FREQS)
    ar = jnp.arange(n_rows).astype(F32)[:, None] * freqs[None, :]
    ac = jnp.arange(GRID_W).astype(F32)[:, None] * freqs[None, :]

    def table(fn):
        by_row = jnp.broadcast_to(fn(ar)[:, None, :], (n_rows, GRID_W, ROPE_FREQS)).reshape(SEQ, ROPE_FREQS)
        by_col = jnp.broadcast_to(fn(ac)[None, :, :], (n_rows, GRID_W, ROPE_FREQS)).reshape(SEQ, ROPE_FREQS)
        return jnp.concatenate([by_row, by_row, by_col, by_col], axis=-1)

    cos, sin = table(jnp.cos), table(jnp.sin)
    cos = jnp.concatenate([jnp.ones((CTX_LEN, HEAD_DIM), F32), cos], axis=0)
    sin = jnp.concatenate([jnp.zeros((CTX_LEN, HEAD_DIM), F32), sin], axis=0)
    first_half = (jnp.arange(HEAD_DIM) % (2 * ROPE_FREQS)) < ROPE_FREQS
    sa = jnp.where(first_half[None, :], -sin, 0.0)
    sb = jnp.where(first_half[None, :], 0.0, sin)
    two = lambda t: jnp.concatenate([t, t], axis=-1)
    return two(cos), two(sa), two(sb)


def kernel(x, c, ctx, c_ctx, w_mod, b_mod, ln1_g, ln1_b, ln2_g, ln2_b, router_w, router_bias, moe_w1, moe_w3, moe_w2, a_w_qkv, a_w_o, a_sink, b_w_in, b_b_in, b_ln_g, b_ln_b, b_w_s, b_b_s, b_w_out, c_w_in, c_w_conv, c_w_out):
    assert x.shape == (1, SEQ, D_MODEL) and ctx.shape == (1, CTX_LEN, D_MODEL)
    tok = jnp.concatenate([ctx[0], x[0]], axis=0)
    cc = jnp.zeros((8, D_MODEL), F32).at[0].set(c[0]).at[1].set(c_ctx)
    mods = _modulation(cc, w_mod, b_mod)
    cos, sa, sb = _rope_tables()
    rw = jnp.pad(router_w, ((0, 0), (0, LANES - N_EXPERTS))).astype(BF16)
    rb = jnp.broadcast_to(router_bias.astype(F32)[:, None], (N_EXPERTS, TM))

    for i in range(DEPTH):
        kind, j = i % N_MIXERS, i // N_MIXERS
        if kind == 0:
            q, kb, vb = _qkv_project(tok, mods, i, a_w_qkv[j].astype(BF16), cos, sa, sb)
            mix = _attention(a_sink[j], q, kb, vb)
            w_out = a_w_o[j]
        elif kind == 1:
            bs = jnp.repeat(b_b_s[j], D_MODEL // GMLP_GROUPS, axis=1)
            mix = _gmlp_mixer(tok, mods, i, b_w_in[j].astype(BF16), b_b_in[j].reshape(1, -1),
                              b_ln_g[j].reshape(1, -1), b_ln_b[j].reshape(1, -1), b_w_s[j].astype(BF16), bs)
            w_out = b_w_out[j]
        else:
            gb, zz = _conv_in(tok, mods, i, c_w_in[j].astype(BF16))
            mix = (gb, zz, jnp.pad(c_w_conv[j], ((0, 5), (0, 0))))
            w_out = c_w_out[j]
        x1, xe, meta, counts = _post(mix, w_out.astype(BF16), tok, mods, i,
                                     ln1_g[i].reshape(1, -1), ln1_b[i].reshape(1, -1), rw, rb, conv=(kind == 2))
        pad_start, cnt, dest, lo_slot, hi_slot, need, nused = _dispatch_plan(meta, counts)
        xs = _scatter_rows(pad_start, cnt, dest, xe)
        ys = _experts(lo_slot, hi_slot, need, nused, xs, moe_w1, moe_w3, moe_w2, i)
        tok = _ln2(x1, ys, dest, mods, i, ln2_g[i].reshape(1, -1), ln2_b[i].reshape(1, -1),
                   latent_only=(i == DEPTH - 1))
    return tok.reshape(1, SEQ, D_MODEL)
```

```python
import functools

import jax
import jax.numpy as jnp
from jax import lax
from jax.experimental import pallas as pl
from jax.experimental.pallas import tpu as pltpu

F32 = jnp.float32
BF16 = jnp.bfloat16

D_MODEL = 1024
SEQ = 16384
DEPTH = 4
GRID_W = 64
CTX_LEN = 256
N_MIXERS = 3
N_HEADS = 16
N_KV_HEADS = 4
HEAD_DIM = 64
ATTN_BLOCK = 128
ATTN_SCALE = HEAD_DIM ** -0.5
ROPE_BASE = 10000.0
ROPE_FREQS = HEAD_DIM // 4
CHUNK = 128
GMLP_GROUPS = 8
N_EXPERTS = 16
N_EXPERT_GROUPS = 4
EXPERTS_PER_GROUP = 4
D_EXPERT = 512
ALPHA = (2 * DEPTH) ** 0.25
LN_EPS = 1e-5

LANES = 128
T_TOK = CTX_LEN + SEQ
TM = 256
N_TILES = T_TOK // TM
CTX_TILES = CTX_LEN // TM
N_ABLK = T_TOK // ATTN_BLOCK
CTX_ABLK = CTX_LEN // ATTN_BLOCK
PAIRS_PER_GROUP = 6
N_CLASSES = N_EXPERT_GROUPS * PAIRS_PER_GROUP
CLASS_ROWS = 32
MOE_BLOCK = 256
N_STAGING = 2
N_MOE_BLOCKS = T_TOK // MOE_BLOCK + N_CLASSES
P_ROWS = N_MOE_BLOCKS * MOE_BLOCK
XE_W = D_MODEL + LANES
SCAT_ROWS = 5 * TM
NEG_BIG = -1e30
VMEM_LIMIT = 52 * 1024 * 1024


def _cparams(sem="arbitrary"):
    return pltpu.CompilerParams(dimension_semantics=(sem,), vmem_limit_bytes=VMEM_LIMIT)


def _mod_row(ref):
    is_ctx = pl.program_id(0) < CTX_TILES
    return jnp.where(is_ctx, ref[1:2, :], ref[0:1, :])


def _layer_norm(x, g, b):
    mu = jnp.mean(x, axis=-1, keepdims=True)
    xc = x - mu
    var = jnp.mean(xc * xc, axis=-1, keepdims=True)
    return xc * lax.rsqrt(var + LN_EPS) * g + b


def _sigmoid(x):
    return 1.0 / (1.0 + jnp.exp(-x))


def _mod_kernel(cc_ref, w_ref, b_ref, o_ref):
    cc = cc_ref[...]
    act = cc * _sigmoid(cc)
    o_ref[...] = jnp.dot(act.astype(BF16), w_ref[...].astype(BF16), preferred_element_type=F32) + b_ref[...]


def _modulation(cc, w_mod, b_mod):
    nt = 1536
    return pl.pallas_call(
        _mod_kernel,
        grid=(DEPTH, 6 * D_MODEL // nt),
        in_specs=[
            pl.BlockSpec((8, D_MODEL), lambda l, n: (0, 0)),
            pl.BlockSpec((None, D_MODEL, nt), lambda l, n: (l, 0, n)),
            pl.BlockSpec((None, 1, nt), lambda l, n: (l, 0, n)),
        ],
        out_specs=pl.BlockSpec((None, 8, nt), lambda l, n: (l, 0, n)),
        out_shape=jax.ShapeDtypeStruct((DEPTH, 8, 6 * D_MODEL), F32),
        compiler_params=pltpu.CompilerParams(
            dimension_semantics=("arbitrary", "arbitrary"), vmem_limit_bytes=VMEM_LIMIT),
        name="modulation",
    )(cc, w_mod, b_mod.reshape(DEPTH, 1, 6 * D_MODEL))


def _mod_spec(layer, chunk):
    return pl.BlockSpec((None, 8, D_MODEL), lambda i: (layer, 0, chunk))


def _store_split_heads(ref, tile, pair):
    low = lax.broadcasted_iota(jnp.int32, tile.shape, 1) < HEAD_DIM
    swapped = pltpu.roll(tile, HEAD_DIM, 1)
    parts = ((jnp.where(low, tile, 0.0), jnp.where(low, 0.0, swapped)),
             (jnp.where(low, swapped, 0.0), jnp.where(low, 0.0, tile)))
    blk = ATTN_BLOCK
    for h, (in_low, in_high) in enumerate(parts):
        cols = slice((2 * pair + h) * LANES, (2 * pair + h + 1) * LANES)
        for n in range(TM // blk):
            rows = slice(n * blk, (n + 1) * blk)
            ref[2 * n * blk:(2 * n + 1) * blk, cols] = in_low[rows].astype(BF16)
            ref[(2 * n + 1) * blk:(2 * n + 2) * blk, cols] = in_high[rows].astype(BF16)


def _qkv_kernel(x_ref, sh_ref, sc_ref, w_ref, cos_ref, sa_ref, sb_ref, q_ref, k_ref, v_ref):
    h = (x_ref[...] * (1.0 + _mod_row(sc_ref)) + _mod_row(sh_ref)).astype(BF16)
    y = jnp.dot(h, w_ref[...], preferred_element_type=F32)
    cos, sa, sb = cos_ref[...], sa_ref[...], sb_ref[...]
    n_q = D_MODEL // LANES
    n_k = N_KV_HEADS * HEAD_DIM // LANES
    for t in range(n_q + n_k):
        yt = y[:, t * LANES:(t + 1) * LANES]
        r = yt * cos + pltpu.roll(yt, LANES - ROPE_FREQS, 1) * sa + pltpu.roll(yt, ROPE_FREQS, 1) * sb
        if t < n_q:
            q_ref[:, t * LANES:(t + 1) * LANES] = (r * ATTN_SCALE).astype(BF16)
        else:
            _store_split_heads(k_ref, r, t - n_q)
    for t in range(n_k):
        _store_split_heads(v_ref, y[:, (n_q + n_k + t) * LANES:(n_q + n_k + t + 1) * LANES], t)


def _qkv_project(x, mods, layer, w, cos, sa, sb):
    kvw = N_KV_HEADS * LANES
    row = lambda i: (i, 0)
    return pl.pallas_call(
        _qkv_kernel,
        grid=(N_TILES,),
        in_specs=[
            pl.BlockSpec((TM, D_MODEL), row),
            _mod_spec(layer, 0), _mod_spec(layer, 1),
            pl.BlockSpec(w.shape, lambda i: (0, 0)),
            pl.BlockSpec((TM, LANES), row), pl.BlockSpec((TM, LANES), row), pl.BlockSpec((TM, LANES), row),
        ],
        out_specs=[pl.BlockSpec((TM, D_MODEL), row), pl.BlockSpec((2 * TM, kvw), row),
                   pl.BlockSpec((2 * TM, kvw), row)],
        out_shape=[jax.ShapeDtypeStruct((T_TOK, D_MODEL), BF16),
                   jax.ShapeDtypeStruct((2 * T_TOK, kvw), BF16),
                   jax.ShapeDtypeStruct((2 * T_TOK, kvw), BF16)],
        compiler_params=_cparams("parallel"),
        name="qkv_rope",
    )(x, mods, mods, w, cos, sa, sb)


def _attn_kernel(sink_ref, q_ref, kp_ref, km_ref, kn_ref, kx_ref, vp_ref, vm_ref, vn_ref, vx_ref, o_ref):
    blk = ATTN_BLOCK
    r = lax.broadcasted_iota(jnp.int32, (2 * blk, 2 * blk), 0) & (blk - 1)
    c = lax.broadcasted_iota(jnp.int32, (2 * blk, 2 * blk), 1) & (blk - 1)
    top_rows = lax.broadcasted_iota(jnp.int32, (2 * blk, 1), 0) < blk
    even_cols = lax.broadcasted_iota(jnp.int32, (2 * blk, 2 * blk), 1) < blk
    even_lanes = lax.broadcasted_iota(jnp.int32, (2 * blk, LANES), 1) < HEAD_DIM
    nt_dims = (((1,), (1,)), ((), ()))

    def halves(s):
        return s[:, :blk], s[:, blk:]

    windows = ((kp_ref, vp_ref, 0), (km_ref, vm_ref, 0), (km_ref, vm_ref, 2 * blk), (kn_ref, vn_ref, 0))
    for sub, j in [(sub, j) for sub in range(2) for j in range(N_KV_HEADS)]:
        i = 2 * pl.program_id(0) + sub
        prev_ok = i >= CTX_ABLK + 1
        cur_ok = i >= CTX_ABLK
        next_ok = jnp.logical_and(i >= CTX_ABLK, i <= N_ABLK - 2)
        m_prev = jnp.logical_and(c >= r, prev_ok)
        m_cur = jnp.logical_and(c >= 0, cur_ok)
        m_next = jnp.logical_and(c <= r, next_ok)
        q_rows = slice(sub * blk, (sub + 1) * blk)
        o_rows = q_rows
        (kp, vp, op), (kc, vc, oc), (kn, vn, on) = windows[sub:sub + 3]
        ks = slice(j * LANES, (j + 1) * LANES)
        k_blk = lambda ref, off: ref[off:off + 2 * blk, ks]
        lhs = jnp.concatenate([q_ref[q_rows, 2 * j * LANES:(2 * j + 1) * LANES],
                               q_ref[q_rows, (2 * j + 1) * LANES:(2 * j + 2) * LANES]], axis=0)
        s_p = jnp.where(m_prev, lax.dot_general(lhs, k_blk(kp, op), nt_dims, preferred_element_type=F32), NEG_BIG)
        s_c = jnp.where(m_cur, lax.dot_general(lhs, k_blk(kc, oc), nt_dims, preferred_element_type=F32), NEG_BIG)
        s_n = jnp.where(m_next, lax.dot_general(lhs, k_blk(kn, on), nt_dims, preferred_element_type=F32), NEG_BIG)
        s_x = lax.dot_general(lhs, kx_ref[:, ks], nt_dims, preferred_element_type=F32)
        s_x0, s_x1 = s_x[:, :2 * blk], s_x[:, 2 * blk:]
        pieces = (s_p, s_c, s_n, s_x0, s_x1)
        sink_e = jnp.where(top_rows, sink_ref[4 * j], sink_ref[4 * j + 2])
        sink_o = jnp.where(top_rows, sink_ref[4 * j + 1], sink_ref[4 * j + 3])
        tile_max = functools.reduce(jnp.maximum, pieces)
        tm_e, tm_o = halves(tile_max)
        m_e = jnp.maximum(jnp.max(tm_e, axis=1, keepdims=True), sink_e)
        m_o = jnp.maximum(jnp.max(tm_o, axis=1, keepdims=True), sink_o)
        m = jnp.where(even_cols, m_e, m_o)
        probs = [jnp.exp(s - m) for s in pieces]
        ts_e, ts_o = halves(functools.reduce(jnp.add, probs))
        l_e = jnp.sum(ts_e, axis=1, keepdims=True) + jnp.exp(sink_e - m_e)
        l_o = jnp.sum(ts_o, axis=1, keepdims=True) + jnp.exp(sink_o - m_o)
        p_p, p_c, p_n, p_x0, p_x1 = [p.astype(BF16) for p in probs]
        p_x = jnp.concatenate([p_x0, p_x1], axis=1)
        o = (jnp.dot(p_p, k_blk(vp, op), preferred_element_type=F32)
             + jnp.dot(p_c, k_blk(vc, oc), preferred_element_type=F32)
             + jnp.dot(p_n, k_blk(vn, on), preferred_element_type=F32)
             + jnp.dot(p_x, vx_ref[:, ks], preferred_element_type=F32))
        o = (o / jnp.where(even_lanes, l_e, l_o)).astype(BF16)
        o_ref[o_rows, 2 * j * LANES:(2 * j + 1) * LANES] = o[:blk]
        o_ref[o_rows, (2 * j + 1) * LANES:(2 * j + 2) * LANES] = o[blk:]


def _attention(sink, q, kb, vb):
    kvw = N_KV_HEADS * LANES
    blk = ATTN_BLOCK
    lo, hi = CTX_ABLK, N_ABLK - 1
    prev_map = lambda s: (jnp.clip(2 * s - 1, lo, hi), 0)
    pair_map = lambda s: (s, 0)
    next_map = lambda s: (jnp.clip(2 * s + 2, lo, hi), 0)
    ctx_map = lambda s: (0, 0)
    kv_specs = [pl.BlockSpec((2 * blk, kvw), prev_map), pl.BlockSpec((4 * blk, kvw), pair_map),
                pl.BlockSpec((2 * blk, kvw), next_map), pl.BlockSpec((2 * CTX_LEN, kvw), ctx_map)]
    return pl.pallas_call(
        _attn_kernel,
        grid=(N_ABLK // 2,),
        in_specs=[pl.BlockSpec(memory_space=pltpu.SMEM), pl.BlockSpec((2 * blk, D_MODEL), pair_map)]
        + kv_specs + kv_specs,
        out_specs=pl.BlockSpec((2 * blk, D_MODEL), pair_map),
        out_shape=jax.ShapeDtypeStruct((T_TOK, D_MODEL), BF16),
        compiler_params=_cparams("parallel"),
        name="window_attention",
    )(sink, q, kb, kb, kb, kb, vb, vb, vb, vb)


def _gmlp_kernel(x_ref, sh_ref, sc_ref, w_ref, b_ref, g_ref, be_ref, ws_ref, bs_ref, o_ref):
    h = (x_ref[...] * (1.0 + _mod_row(sc_ref)) + _mod_row(sh_ref)).astype(BF16)
    z = jax.nn.gelu(jnp.dot(h, w_ref[...], preferred_element_type=F32) + b_ref[...], approximate=True)
    u = z[:, :D_MODEL]
    v = _layer_norm(z[:, D_MODEL:], g_ref[...], be_ref[...]).astype(BF16)
    gd = D_MODEL // GMLP_GROUPS
    for n in range(TM // CHUNK):
        rows = slice(n * CHUNK, (n + 1) * CHUNK)
        for g in range(GMLP_GROUPS):
            cols = slice(g * gd, (g + 1) * gd)
            s = jnp.dot(ws_ref[g], v[rows, cols], preferred_element_type=F32) + bs_ref[:, cols]
            o_ref[rows, cols] = (u[rows, cols] * s).astype(BF16)


def _gmlp_mixer(x, mods, layer, w_in, b_in, ln_g, ln_b, w_s, b_s):
    row = lambda i: (i, 0)
    fix2 = lambda i: (0, 0)
    return pl.pallas_call(
        _gmlp_kernel,
        grid=(N_TILES,),
        in_specs=[
            pl.BlockSpec((TM, D_MODEL), row),
            _mod_spec(layer, 0), _mod_spec(layer, 1),
            pl.BlockSpec((D_MODEL, 2 * D_MODEL), fix2),
            pl.BlockSpec((1, 2 * D_MODEL), fix2),
            pl.BlockSpec((1, D_MODEL), fix2), pl.BlockSpec((1, D_MODEL), fix2),
            pl.BlockSpec((GMLP_GROUPS, CHUNK, CHUNK), lambda i: (0, 0, 0)),
            pl.BlockSpec((CHUNK, D_MODEL), fix2),
        ],
        out_specs=pl.BlockSpec((TM, D_MODEL), row),
        out_shape=jax.ShapeDtypeStruct((T_TOK, D_MODEL), BF16),
        compiler_params=_cparams("parallel"),
        name="gmlp_mixer",
    )(x, mods, mods, w_in, b_in, ln_g, ln_b, w_s, b_s)


def _conv_in_kernel(x_ref, sh_ref, sc_ref, w_ref, gb_ref, zz_ref):
    h = (x_ref[...] * (1.0 + _mod_row(sc_ref)) + _mod_row(sh_ref)).astype(BF16)
    proj = jnp.dot(h, w_ref[...], preferred_element_type=F32)
    gb_ref[...] = proj[:, :D_MODEL]
    zz_ref[...] = proj[:, D_MODEL:2 * D_MODEL] * proj[:, 2 * D_MODEL:]


def _conv_in(x, mods, layer, w_in):
    row = lambda i: (i, 0)
    return pl.pallas_call(
        _conv_in_kernel,
        grid=(N_TILES,),
        in_specs=[pl.BlockSpec((TM, D_MODEL), row), _mod_spec(layer, 0), _mod_spec(layer, 1),
                  pl.BlockSpec((D_MODEL, 3 * D_MODEL), lambda i: (0, 0))],
        out_specs=[pl.BlockSpec((TM, D_MODEL), row), pl.BlockSpec((TM, D_MODEL), row)],
        out_shape=[jax.ShapeDtypeStruct((T_TOK, D_MODEL), F32), jax.ShapeDtypeStruct((T_TOK, D_MODEL), F32)],
        compiler_params=_cparams("parallel"),
        name="conv_in",
    )(x, mods, mods, w_in)


def _route(h2, rw_ref, rb_ref, tri_ref, carry_ref, xe_ref, meta_ref, cnt_ref):
    logits = jnp.dot(h2.astype(BF16), rw_ref[...], preferred_element_type=F32)
    s = _sigmoid(jnp.transpose(logits)[:N_EXPERTS, :])
    ssel = s + rb_ref[...]
    row = lambda v, k: v[k:k + 1, :]
    npg = EXPERTS_PER_GROUP

    def first_argmax(vals):
        best, idx = vals[0], jnp.zeros_like(vals[0])
        for k in range(1, len(vals)):
            better = vals[k] > best
            best = jnp.where(better, vals[k], best)
            idx = jnp.where(better, float(k), idx)
        return best, idx

    def pick(vals, idx):
        out = vals[-1]
        for k in range(len(vals) - 2, -1, -1):
            out = jnp.where(idx == float(k), vals[k], out)
        return out

    group_scores = []
    for g in range(N_EXPERT_GROUPS):
        a, b, c, d = (row(ssel, npg * g + k) for k in range(npg))
        group_scores.append(jnp.maximum(jnp.maximum(jnp.maximum(a + b, a + c), jnp.maximum(a + d, b + c)),
                                        jnp.maximum(b + d, c + d)))
    _, gsel = first_argmax(group_scores)
    biased = [pick([row(ssel, npg * g + k) for g in range(N_EXPERT_GROUPS)], gsel) for k in range(npg)]
    plain = [pick([row(s, npg * g + k) for g in range(N_EXPERT_GROUPS)], gsel) for k in range(npg)]
    _, i1 = first_argmax(biased)
    _, i2 = first_argmax([jnp.where(i1 == float(k), -jnp.inf, biased[k]) for k in range(npg)])
    lo = jnp.minimum(i1, i2)
    hi = jnp.maximum(i1, i2)
    s_lo = pick(plain, lo)
    s_hi = pick(plain, hi)
    den = s_lo + s_hi
    g_lo = s_lo / den
    g_hi = s_hi / den
    pair = lo * (7.0 - lo) * 0.5 + hi - lo - 1.0
    cls = gsel * PAIRS_PER_GROUP + pair
    cls_rows = lax.broadcasted_iota(jnp.int32, (CLASS_ROWS, TM), 0).astype(F32)
    onehot = cls_rows == cls
    oh_f = jnp.where(onehot, 1.0, 0.0)
    before = jnp.dot(oh_f.astype(BF16), tri_ref[...], preferred_element_type=F32)
    carry = carry_ref[...]
    rank = jnp.sum(jnp.where(onehot, before + carry[:, 0:1], 0.0), axis=0, keepdims=True)
    carry = carry + jnp.sum(oh_f, axis=1, keepdims=True)
    carry_ref[...] = carry
    cnt_ref[...] = carry
    xe_ref[:, :D_MODEL] = h2
    gate_rows = lax.broadcasted_iota(jnp.int32, (LANES, TM), 0) < LANES // 2
    xe_ref[:, D_MODEL:] = jnp.transpose(jnp.where(gate_rows, g_lo, g_hi))
    meta_rows = lax.broadcasted_iota(jnp.int32, (8, TM), 0)
    meta_ref[...] = jnp.where(meta_rows == 0, cls, jnp.where(meta_rows == 1, rank, 0.0)).astype(jnp.int32)


def _post_kernel(*refs, conv):
    if conv:
        (gb_ref, zz_ref, zp_ref, zn_ref, wc_ref, *refs) = refs
    else:
        (a_ref, *refs) = refs
    (w_ref, x_ref, ga_ref, shf_ref, scf_ref, lg_ref, lb_ref, rw_ref, rb_ref,
     x1_ref, xe_ref, meta_ref, cnt_ref, tri_ref, carry_ref) = refs
    i = pl.program_id(0)

    @pl.when(i == 0)
    def _():
        carry_ref[...] = jnp.zeros_like(carry_ref)
        rr = lax.broadcasted_iota(jnp.int32, (TM, TM), 0)
        cc = lax.broadcasted_iota(jnp.int32, (TM, TM), 1)
        tri_ref[...] = jnp.where(rr < cc, 1.0, 0.0).astype(BF16)

    if conv:
        zz = zz_ref[...]
        prev_ok = i >= CTX_TILES + 1
        next_ok = jnp.logical_and(i >= CTX_TILES, i <= N_TILES - 2)
        z_before = jnp.where(prev_ok, zp_ref[7:8, :], 0.0)
        z_after = jnp.where(next_ok, zn_ref[0:1, :], 0.0)
        rows = lax.broadcasted_iota(jnp.int32, (TM, 1), 0)
        zm1 = jnp.where(rows == 0, z_before, pltpu.roll(zz, 1, 0))
        zp1 = jnp.where(rows == TM - 1, z_after, pltpu.roll(zz, TM - 1, 0))
        zc = wc_ref[0:1, :] * zm1 + wc_ref[1:2, :] * zz + wc_ref[2:3, :] * zp1
        a = (gb_ref[...] * zc).astype(BF16)
    else:
        a = a_ref[...]
    y = jnp.dot(a, w_ref[...], preferred_element_type=F32)
    x1 = _layer_norm(ALPHA * x_ref[...] + _mod_row(ga_ref) * y, lg_ref[...], lb_ref[...])
    x1_ref[...] = x1
    h2 = x1 * (1.0 + _mod_row(scf_ref)) + _mod_row(shf_ref)
    _route(h2, rw_ref, rb_ref, tri_ref, carry_ref, xe_ref, meta_ref, cnt_ref)


def _post(mix, w_out, x, mods, layer, ln_g, ln_b, rw, rb, conv):
    row = lambda i: (i, 0)
    fix2 = lambda i: (0, 0)
    tile = pl.BlockSpec((TM, D_MODEL), row)
    if conv:
        gb, zz, wc = mix
        sub = TM // 8
        mix_args = (gb, zz, zz, zz, wc)
        mix_specs = [tile, tile,
                     pl.BlockSpec((8, D_MODEL), lambda i: (jnp.maximum(i * sub - 1, 0), 0)),
                     pl.BlockSpec((8, D_MODEL), lambda i: (jnp.minimum((i + 1) * sub, T_TOK // 8 - 1), 0)),
                     pl.BlockSpec((8, D_MODEL), fix2)]
    else:
        mix_args = (mix,)
        mix_specs = [tile]
    vec = pl.BlockSpec((1, D_MODEL), fix2)
    return pl.pallas_call(
        functools.partial(_post_kernel, conv=conv),
        grid=(N_TILES,),
        in_specs=mix_specs + [
            pl.BlockSpec((D_MODEL, D_MODEL), fix2), tile,
            _mod_spec(layer, 2), _mod_spec(layer, 3), _mod_spec(layer, 4),
            vec, vec,
            pl.BlockSpec((D_MODEL, LANES), fix2), pl.BlockSpec((N_EXPERTS, TM), fix2),
        ],
        out_specs=[tile, pl.BlockSpec((TM, XE_W), row), pl.BlockSpec((None, 8, TM), lambda i: (i, 0, 0)),
                   pl.BlockSpec((CLASS_ROWS, LANES), fix2)],
        out_shape=[jax.ShapeDtypeStruct((T_TOK, D_MODEL), F32),
                   jax.ShapeDtypeStruct((T_TOK, XE_W), F32),
                   jax.ShapeDtypeStruct((N_TILES, 8, TM), jnp.int32),
                   jax.ShapeDtypeStruct((CLASS_ROWS, LANES), F32)],
        scratch_shapes=[pltpu.VMEM((TM, TM), BF16), pltpu.VMEM((CLASS_ROWS, LANES), F32)],
        compiler_params=_cparams("arbitrary"),
        name="post_conv" if conv else "post",
    )(*mix_args, w_out, x, mods, mods, mods, ln_g, ln_b, rw, rb)


def _scatter_kernel(pstart_ref, cnt_ref, dest_ref, xe_ref, xs_ref, zero_ref, sem):
    def row_copy(src, d):
        return pltpu.make_async_copy(src, xs_ref.at[pl.ds(d, 1)], sem)

    first_row = pl.program_id(0) * SCAT_ROWS

    def issue(jo, carry):
        base = pl.multiple_of(jo * 8, 8)
        group = xe_ref.at[pl.ds(base, 8)]
        for k in range(8):
            row_copy(group.at[pl.ds(k, 1)], dest_ref[first_row + base + k]).start()
        return carry

    lax.fori_loop(0, SCAT_ROWS // 8, issue, 0)

    def drain(j, carry):
        row_copy(xe_ref.at[pl.ds(0, 1)], 0).wait()
        return carry

    lax.fori_loop(0, SCAT_ROWS, drain, 0, unroll=8)

    @pl.when(pl.program_id(0) == pl.num_programs(0) - 1)
    def _():
        zero_ref[...] = jnp.zeros_like(zero_ref)

        def per_class(c, carry):
            n = cnt_ref[c]
            first = pstart_ref[c] + n
            last = pstart_ref[c] + jnp.bitwise_and(n + (MOE_BLOCK - 1), -MOE_BLOCK)
            aligned = jnp.minimum(jnp.bitwise_and(first + 7, -8), last)

            def group_copy(k):
                row = pl.multiple_of(aligned + 8 * k, 8)
                return pltpu.make_async_copy(zero_ref.at[pl.ds(0, 8)], xs_ref.at[pl.ds(row, 8)], sem)

            def fill(k, cc):
                row_copy(zero_ref.at[pl.ds(0, 1)], first + k).start()
                return cc

            def fill_wait(k, cc):
                row_copy(zero_ref.at[pl.ds(0, 1)], 0).wait()
                return cc

            def fill_group(k, cc):
                group_copy(k).start()
                return cc

            def fill_group_wait(k, cc):
                group_copy(k).wait()
                return cc

            n_groups = lax.shift_right_logical(last - aligned, 3)
            lax.fori_loop(0, aligned - first, fill, 0)
            lax.fori_loop(0, n_groups, fill_group, 0)
            lax.fori_loop(0, aligned - first, fill_wait, 0)
            lax.fori_loop(0, n_groups, fill_group_wait, 0)
            return carry

        lax.fori_loop(0, N_CLASSES, per_class, 0)

        last = N_CLASSES - 1
        rows_used = pstart_ref[last] + jnp.bitwise_and(cnt_ref[last] + (MOE_BLOCK - 1), -MOE_BLOCK)
        first_free = lax.shift_right_logical(rows_used, MOE_BLOCK.bit_length() - 1)

        def block_copy(blk):
            return pltpu.make_async_copy(zero_ref, xs_ref.at[pl.ds(blk * MOE_BLOCK, MOE_BLOCK)], sem)

        def fill_block(blk, cc):
            block_copy(blk).start()
            return cc

        def fill_block_wait(blk, cc):
            block_copy(0).wait()
            return cc

        lax.fori_loop(first_free, N_MOE_BLOCKS, fill_block, 0)
        lax.fori_loop(first_free, N_MOE_BLOCKS, fill_block_wait, 0)


def _scatter_rows(pstart, cnt, dest, xe):
    grid_spec = pltpu.PrefetchScalarGridSpec(
        num_scalar_prefetch=2,
        grid=(T_TOK // SCAT_ROWS,),
        in_specs=[pl.BlockSpec((T_TOK,), lambda i, ps, cn: (0,), memory_space=pltpu.SMEM),
                  pl.BlockSpec((SCAT_ROWS, XE_W), lambda i, ps, cn: (i, 0))],
        out_specs=pl.BlockSpec(memory_space=pl.ANY),
        scratch_shapes=[pltpu.VMEM((MOE_BLOCK, XE_W), F32), pltpu.SemaphoreType.DMA(())],
    )
    return pl.pallas_call(
        _scatter_kernel,
        grid_spec=grid_spec,
        out_shape=jax.ShapeDtypeStruct((P_ROWS, XE_W), F32),
        compiler_params=_cparams("arbitrary"),
        name="scatter_rows",
    )(pstart, cnt, dest.reshape(T_TOK), xe)


def _expert_kernel(lo_ref, hi_ref, need_ref, nused_ref, x_ref, w1_hbm, w3_hbm, w2_hbm, o_ref,
                   st1, st3, st2, w1s, w3s, w2s, sems, done_ref, *, layer):
    b = pl.program_id(0)

    def fetch(k, buf):
        return (pltpu.make_async_copy(w1_hbm.at[layer, k], st1.at[buf], sems.at[buf]),
                pltpu.make_async_copy(w3_hbm.at[layer, k], st3.at[buf], sems.at[buf]),
                pltpu.make_async_copy(w2_hbm.at[layer, k], st2.at[buf], sems.at[buf]))

    @pl.when(b == 0)
    def _():
        done_ref[0] = 0
        for k in range(N_STAGING):
            for cp in fetch(k, k):
                cp.start()

    used = b < nused_ref[0]

    @pl.when(jnp.logical_not(used))
    def _():
        o_ref[...] = jnp.zeros_like(o_ref)

    @pl.when(used)
    def _():
        def install(k, carry):
            buf = k % N_STAGING
            slot = k % EXPERTS_PER_GROUP
            for cp in fetch(k, buf):
                cp.wait()
            w1s[slot] = st1[buf].astype(BF16)
            w3s[slot] = st3[buf].astype(BF16)
            w2s[slot] = st2[buf].astype(BF16)

            @pl.when(k + N_STAGING < N_EXPERTS)
            def _():
                for cp in fetch(k + N_STAGING, buf):
                    cp.start()

            return carry

        done = done_ref[0]
        lax.fori_loop(done, need_ref[b], install, 0)
        done_ref[0] = jnp.maximum(done, need_ref[b])

        xb = x_ref[:, :D_MODEL].astype(BF16)
        gates = x_ref[:, D_MODEL:]

        def expert(slot):
            h1 = jnp.dot(xb, w1s[slot], preferred_element_type=F32)
            h3 = jnp.dot(xb, w3s[slot], preferred_element_type=F32)
            act = (h1 * _sigmoid(h1) * h3).astype(BF16)
            return jnp.dot(act, w2s[slot], preferred_element_type=F32)

        ya = expert(lo_ref[b])
        yb = expert(hi_ref[b])
        o_ref[...] = gates[:, 0:1] * ya + gates[:, LANES // 2:LANES // 2 + 1] * yb

    @pl.when(b == pl.num_programs(0) - 1)
    def _():
        def drain(k, carry):
            for cp in fetch(k, k % N_STAGING):
                cp.wait()
            return carry

        done = done_ref[0]
        lax.fori_loop(done, jnp.minimum(done + N_STAGING, N_EXPERTS), drain, 0)


def _experts(lo_slot, hi_slot, need, nused, xs, w1, w3, w2, layer):
    rows = lambda b, lo, hi, nd, nu: (jnp.maximum(jnp.minimum(b, nu[0] - 1), 0), 0)
    any_spec = pl.BlockSpec(memory_space=pl.ANY)
    up, down = (D_MODEL, D_EXPERT), (D_EXPERT, D_MODEL)
    grid_spec = pltpu.PrefetchScalarGridSpec(
        num_scalar_prefetch=4,
        grid=(N_MOE_BLOCKS,),
        in_specs=[pl.BlockSpec((MOE_BLOCK, XE_W), rows), any_spec, any_spec, any_spec],
        out_specs=pl.BlockSpec((MOE_BLOCK, D_MODEL), lambda b, lo, hi, nd, nu: (b, 0)),
        scratch_shapes=[pltpu.VMEM((N_STAGING,) + up, F32), pltpu.VMEM((N_STAGING,) + up, F32),
                        pltpu.VMEM((N_STAGING,) + down, F32),
                        pltpu.VMEM((EXPERTS_PER_GROUP,) + up, BF16), pltpu.VMEM((EXPERTS_PER_GROUP,) + up, BF16),
                        pltpu.VMEM((EXPERTS_PER_GROUP,) + down, BF16),
                        pltpu.SemaphoreType.DMA((N_STAGING,)), pltpu.SMEM((1,), jnp.int32)],
    )
    return pl.pallas_call(
        functools.partial(_expert_kernel, layer=layer),
        grid_spec=grid_spec,
        out_shape=jax.ShapeDtypeStruct((P_ROWS, D_MODEL), F32),
        compiler_params=_cparams("arbitrary"),
        name="expert_pairs",
    )(lo_slot, hi_slot, need, nused, xs, w1, w3, w2)


def _ln2_kernel(dest_ref, dest_next_ref, ys_ref, x_ref, gf_ref, lg_ref, lb_ref, o_ref, ybuf, sems):
    i = pl.program_id(0)
    slot = i % 2

    def row_copy(src_row, dst_rows, dst_slot):
        return pltpu.make_async_copy(ys_ref.at[pl.ds(src_row, 1)], dst_rows, sems.at[dst_slot])

    def issue_tile(idx_ref, dst_slot):
        def issue(jo, carry):
            base = pl.multiple_of(jo * 8, 8)
            group = ybuf.at[dst_slot, pl.ds(base, 8)]
            for k in range(8):
                row_copy(idx_ref[base + k], group.at[pl.ds(k, 1)], dst_slot).start()
            return carry

        lax.fori_loop(0, TM // 8, issue, 0)

    @pl.when(i == 0)
    def _():
        issue_tile(dest_ref, 0)

    @pl.when(i + 1 < pl.num_programs(0))
    def _():
        issue_tile(dest_next_ref, 1 - slot)

    def drain(j, carry):
        row_copy(0, ybuf.at[slot, pl.ds(0, 1)], slot).wait()
        return carry

    lax.fori_loop(0, TM, drain, 0, unroll=8)
    o_ref[...] = _layer_norm(ALPHA * x_ref[...] + _mod_row(gf_ref) * ybuf[slot], lg_ref[...], lb_ref[...])


def _ln2(x1, ys, dest, mods, layer, ln_g, ln_b, latent_only):
    row = lambda i: (i, 0)
    tile = pl.BlockSpec((TM, D_MODEL), row)
    vec = pl.BlockSpec((1, D_MODEL), lambda i: (0, 0))
    idx = lambda f: pl.BlockSpec((TM,), f, memory_space=pltpu.SMEM)
    if latent_only:
        out_rows = SEQ
        out_spec = pl.BlockSpec((TM, D_MODEL), lambda i: (jnp.maximum(i - CTX_TILES, 0), 0))
    else:
        out_rows = T_TOK
        out_spec = tile
    return pl.pallas_call(
        _ln2_kernel,
        grid=(N_TILES,),
        in_specs=[idx(lambda i: (i,)), idx(lambda i: (jnp.minimum(i + 1, N_TILES - 1),)),
                  pl.BlockSpec(memory_space=pl.ANY), tile, _mod_spec(layer, 5), vec, vec],
        out_specs=out_spec,
        out_shape=jax.ShapeDtypeStruct((out_rows, D_MODEL), F32),
        scratch_shapes=[pltpu.VMEM((2, TM, D_MODEL), F32), pltpu.SemaphoreType.DMA((2,))],
        compiler_params=_cparams("arbitrary"),
        name="moe_gather_residual_ln",
    )(dest.reshape(T_TOK), dest.reshape(T_TOK), ys, x1, mods, ln_g, ln_b)


def _pair_tables():
    pairs = [(a, b) for a in range(EXPERTS_PER_GROUP) for b in range(a + 1, EXPERTS_PER_GROUP)]
    return jnp.array([p[0] for p in pairs], jnp.int32), jnp.array([p[1] for p in pairs], jnp.int32)


def _dispatch_plan(meta, counts):
    cnt = counts[:N_CLASSES, 0].astype(jnp.int32)
    padded = (cnt + MOE_BLOCK - 1) // MOE_BLOCK * MOE_BLOCK
    pad_end = jnp.cumsum(padded)
    pad_start = pad_end - padded
    nused = pad_end[-1:] // MOE_BLOCK
    cls, rank = meta[:, 0:1, :], meta[:, 1:2, :]
    before = (cls[..., None] > jnp.arange(N_CLASSES, dtype=jnp.int32)).astype(jnp.int32)
    dest = rank + jnp.sum(before * padded, axis=-1)
    blk_row = jnp.arange(N_MOE_BLOCKS, dtype=jnp.int32) * MOE_BLOCK
    blk_cls = jnp.minimum(jnp.sum((pad_end[None, :] <= blk_row[:, None]).astype(jnp.int32), axis=1),
                          N_CLASSES - 1)
    pair_lo, pair_hi = _pair_tables()
    lo_slot = pair_lo[blk_cls % PAIRS_PER_GROUP]
    hi_slot = pair_hi[blk_cls % PAIRS_PER_GROUP]
    need = (blk_cls // PAIRS_PER_GROUP) * EXPERTS_PER_GROUP + hi_slot + 1
    return pad_start, cnt, dest, lo_slot, hi_slot, need, nused


def _rope_tables():
    n_rows = SEQ // GRID_W
    freqs = jnp.power(ROPE_BASE, -jnp.arange(ROPE_FREQS, dtype=F32) / ROPE_FREQS)
    ar = jnp.arange(n_rows).astype(F32)[:, None] * freqs[None, :]
    ac = jnp.arange(GRID_W).astype(F32)[:, None] * freqs[None, :]

    def table(fn):
        by_row = jnp.broadcast_to(fn(ar)[:, None, :], (n_rows, GRID_W, ROPE_FREQS)).reshape(SEQ, ROPE_FREQS)
        by_col = jnp.broadcast_to(fn(ac)[None, :, :], (n_rows, GRID_W, ROPE_FREQS)).reshape(SEQ, ROPE_FREQS)
        return jnp.concatenate([by_row, by_row, by_col, by_col], axis=-1)

    cos, sin = table(jnp.cos), table(jnp.sin)
    cos = jnp.concatenate([jnp.ones((CTX_LEN, HEAD_DIM), F32), cos], axis=0)
    sin = jnp.concatenate([jnp.zeros((CTX_LEN, HEAD_DIM), F32), sin], axis=0)
    first_half = (jnp.arange(HEAD_DIM) % (2 * ROPE_FREQS)) < ROPE_FREQS
    sa = jnp.where(first_half[None, :], -sin, 0.0)
    sb = jnp.where(first_half[None, :], 0.0, sin)
    two = lambda t: jnp.concatenate([t, t], axis=-1)
    return two(cos), two(sa), two(sb)


def kernel(x, c, ctx, c_ctx, w_mod, b_mod, ln1_g, ln1_b, ln2_g, ln2_b, router_w, router_bias, moe_w1, moe_w3, moe_w2, a_w_qkv, a_w_o, a_sink, b_w_in, b_b_in, b_ln_g, b_ln_b, b_w_s, b_b_s, b_w_out, c_w_in, c_w_conv, c_w_out):
    assert x.shape == (1, SEQ, D_MODEL) and ctx.shape == (1, CTX_LEN, D_MODEL)
    tok = jnp.concatenate([ctx[0], x[0]], axis=0)
    cc = jnp.zeros((8, D_MODEL), F32).at[0].set(c[0]).at[1].set(c_ctx)
    mods = _modulation(cc, w_mod, b_mod)
    cos, sa, sb = _rope_tables()
    rw = jnp.pad(router_w, ((0, 0), (0, LANES - N_EXPERTS))).astype(BF16)
    rb = jnp.broadcast_to(router_bias.astype(F32)[:, None], (N_EXPERTS, TM))

    for i in range(DEPTH):
        kind, j = i % N_MIXERS, i // N_MIXERS
        if kind == 0:
            q, kb, vb = _qkv_project(tok, mods, i, a_w_qkv[j].astype(BF16), cos, sa, sb)
            mix = _attention(a_sink[j], q, kb, vb)
            w_out = a_w_o[j]
        elif kind == 1:
            bs = jnp.repeat(b_b_s[j], D_MODEL // GMLP_GROUPS, axis=1)
            mix = _gmlp_mixer(tok, mods, i, b_w_in[j].astype(BF16), b_b_in[j].reshape(1, -1),
                              b_ln_g[j].reshape(1, -1), b_ln_b[j].reshape(1, -1), b_w_s[j].astype(BF16), bs)
            w_out = b_w_out[j]
        else:
            gb, zz = _conv_in(tok, mods, i, c_w_in[j].astype(BF16))
            mix = (gb, zz, jnp.pad(c_w_conv[j], ((0, 5), (0, 0))))
            w_out = c_w_out[j]
        x1, xe, meta, counts = _post(mix, w_out.astype(BF16), tok, mods, i,
                                     ln1_g[i].reshape(1, -1), ln1_b[i].reshape(1, -1), rw, rb, conv=(kind == 2))
        pad_start, cnt, dest, lo_slot, hi_slot, need, nused = _dispatch_plan(meta, counts)
        xs = _scatter_rows(pad_start, cnt, dest, xe)
        ys = _experts(lo_slot, hi_slot, need, nused, xs, moe_w1, moe_w3, moe_w2, i)
        tok = _ln2(x1, ys, dest, mods, i, ln2_g[i].reshape(1, -1), ln2_b[i].reshape(1, -1),
                   latent_only=(i == DEPTH - 1))
    return tok.reshape(1, SEQ, D_MODEL)
```

```python
import functools

import jax
import jax.numpy as jnp
from jax import lax
from jax.experimental import pallas as pl
from jax.experimental.pallas import tpu as pltpu

F32 = jnp.float32
BF16 = jnp.bfloat16

D_MODEL = 1024
SEQ = 16384
DEPTH = 4
GRID_W = 64
CTX_LEN = 256
N_MIXERS = 3
N_HEADS = 16
N_KV_HEADS = 4
HEAD_DIM = 64
ATTN_BLOCK = 128
ATTN_SCALE = HEAD_DIM ** -0.5
ROPE_BASE = 10000.0
ROPE_FREQS = HEAD_DIM // 4
CHUNK = 128
GMLP_GROUPS = 8
N_EXPERTS = 16
N_EXPERT_GROUPS = 4
EXPERTS_PER_GROUP = 4
D_EXPERT = 512
ALPHA = (2 * DEPTH) ** 0.25
LN_EPS = 1e-5

LANES = 128
T_TOK = CTX_LEN + SEQ
TM = 256
N_TILES = T_TOK // TM
CTX_TILES = CTX_LEN // TM
N_ABLK = T_TOK // ATTN_BLOCK
CTX_ABLK = CTX_LEN // ATTN_BLOCK
PAIRS_PER_GROUP = 6
N_CLASSES = N_EXPERT_GROUPS * PAIRS_PER_GROUP
CLASS_ROWS = 32
MOE_BLOCK = 256
N_STAGING = 2
N_MOE_BLOCKS = T_TOK // MOE_BLOCK + N_CLASSES
P_ROWS = N_MOE_BLOCKS * MOE_BLOCK
XE_W = D_MODEL // 2 + LANES
SCAT_ROWS = 5 * TM
NEG_BIG = -1e30
VMEM_LIMIT = 52 * 1024 * 1024


def _cparams(sem="arbitrary"):
    return pltpu.CompilerParams(dimension_semantics=(sem,), vmem_limit_bytes=VMEM_LIMIT)


def _mod_row(ref):
    is_ctx = pl.program_id(0) < CTX_TILES
    return jnp.where(is_ctx, ref[1:2, :], ref[0:1, :])


def _layer_norm(x, g, b):
    mu = jnp.mean(x, axis=-1, keepdims=True)
    xc = x - mu
    var = jnp.mean(xc * xc, axis=-1, keepdims=True)
    return xc * lax.rsqrt(var + LN_EPS) * g + b


def _sigmoid(x):
    return 1.0 / (1.0 + jnp.exp(-x))


def _mod_kernel(cc_ref, w_ref, b_ref, o_ref):
    cc = cc_ref[...]
    act = cc * _sigmoid(cc)
    o_ref[...] = jnp.dot(act.astype(BF16), w_ref[...].astype(BF16), preferred_element_type=F32) + b_ref[...]


def _modulation(cc, w_mod, b_mod):
    nt = 1536
    return pl.pallas_call(
        _mod_kernel,
        grid=(DEPTH, 6 * D_MODEL // nt),
        in_specs=[
            pl.BlockSpec((8, D_MODEL), lambda l, n: (0, 0)),
            pl.BlockSpec((None, D_MODEL, nt), lambda l, n: (l, 0, n)),
            pl.BlockSpec((None, 1, nt), lambda l, n: (l, 0, n)),
        ],
        out_specs=pl.BlockSpec((None, 8, nt), lambda l, n: (l, 0, n)),
        out_shape=jax.ShapeDtypeStruct((DEPTH, 8, 6 * D_MODEL), F32),
        compiler_params=pltpu.CompilerParams(
            dimension_semantics=("arbitrary", "arbitrary"), vmem_limit_bytes=VMEM_LIMIT),
        name="modulation",
    )(cc, w_mod, b_mod.reshape(DEPTH, 1, 6 * D_MODEL))


def _mod_spec(layer, chunk):
    return pl.BlockSpec((None, 8, D_MODEL), lambda i: (layer, 0, chunk))


def _store_split_heads(ref, tile, pair):
    low = lax.broadcasted_iota(jnp.int32, tile.shape, 1) < HEAD_DIM
    swapped = pltpu.roll(tile, HEAD_DIM, 1)
    parts = ((jnp.where(low, tile, 0.0), jnp.where(low, 0.0, swapped)),
             (jnp.where(low, swapped, 0.0), jnp.where(low, 0.0, tile)))
    blk = ATTN_BLOCK
    for h, (in_low, in_high) in enumerate(parts):
        cols = slice((2 * pair + h) * LANES, (2 * pair + h + 1) * LANES)
        for n in range(TM // blk):
            rows = slice(n * blk, (n + 1) * blk)
            ref[2 * n * blk:(2 * n + 1) * blk, cols] = in_low[rows].astype(BF16)
            ref[(2 * n + 1) * blk:(2 * n + 2) * blk, cols] = in_high[rows].astype(BF16)


def _token_operands(tokens):
    if isinstance(tokens, tuple):
        specs = [pl.BlockSpec((TM, D_MODEL), lambda i: (jnp.minimum(i, CTX_TILES - 1), 0)),
                 pl.BlockSpec((TM, D_MODEL), lambda i: (jnp.maximum(i - CTX_TILES, 0), 0))]
        return list(tokens), specs
    return [tokens], [pl.BlockSpec((TM, D_MODEL), lambda i: (i, 0))]


def _token_tile(tok_refs):
    if len(tok_refs) == 2:
        return jnp.where(pl.program_id(0) < CTX_TILES, tok_refs[0][...], tok_refs[1][...])
    return tok_refs[0][...]


def _qkv_kernel(*refs, n_tok):
    tok_refs = refs[:n_tok]
    sh_ref, sc_ref, w_ref, cos_ref, sa_ref, sb_ref, q_ref, k_ref, v_ref = refs[n_tok:]
    h = (_token_tile(tok_refs) * (1.0 + _mod_row(sc_ref)) + _mod_row(sh_ref)).astype(BF16)
    y = jnp.dot(h, w_ref[...], preferred_element_type=F32)
    cos, sa, sb = cos_ref[...], sa_ref[...], sb_ref[...]
    n_q = D_MODEL // LANES
    n_k = N_KV_HEADS * HEAD_DIM // LANES
    for t in range(n_q + n_k):
        yt = y[:, t * LANES:(t + 1) * LANES]
        r = yt * cos + pltpu.roll(yt, LANES - ROPE_FREQS, 1) * sa + pltpu.roll(yt, ROPE_FREQS, 1) * sb
        if t < n_q:
            q_ref[:, t * LANES:(t + 1) * LANES] = (r * ATTN_SCALE).astype(BF16)
        else:
            _store_split_heads(k_ref, r, t - n_q)
    for t in range(n_k):
        _store_split_heads(v_ref, y[:, (n_q + n_k + t) * LANES:(n_q + n_k + t + 1) * LANES], t)


def _qkv_project(tokens, mods, layer, w, cos, sa, sb):
    kvw = N_KV_HEADS * LANES
    row = lambda i: (i, 0)
    tok_args, tok_specs = _token_operands(tokens)
    return pl.pallas_call(
        functools.partial(_qkv_kernel, n_tok=len(tok_args)),
        grid=(N_TILES,),
        in_specs=tok_specs + [
            _mod_spec(layer, 0), _mod_spec(layer, 1),
            pl.BlockSpec(w.shape, lambda i: (0, 0)),
            pl.BlockSpec((TM, LANES), row), pl.BlockSpec((TM, LANES), row), pl.BlockSpec((TM, LANES), row),
        ],
        out_specs=[pl.BlockSpec((TM, D_MODEL), row), pl.BlockSpec((2 * TM, kvw), row),
                   pl.BlockSpec((2 * TM, kvw), row)],
        out_shape=[jax.ShapeDtypeStruct((T_TOK, D_MODEL), BF16),
                   jax.ShapeDtypeStruct((2 * T_TOK, kvw), BF16),
                   jax.ShapeDtypeStruct((2 * T_TOK, kvw), BF16)],
        compiler_params=_cparams("parallel"),
        name="qkv_rope",
    )(*tok_args, mods, mods, w, cos, sa, sb)


def _attn_kernel(sink_ref, q_ref, kp_ref, km_ref, kn_ref, kx_ref, vp_ref, vm_ref, vn_ref, vx_ref, o_ref):
    blk = ATTN_BLOCK
    r = lax.broadcasted_iota(jnp.int32, (2 * blk, 2 * blk), 0) & (blk - 1)
    c = lax.broadcasted_iota(jnp.int32, (2 * blk, 2 * blk), 1) & (blk - 1)
    top_rows = lax.broadcasted_iota(jnp.int32, (2 * blk, 1), 0) < blk
    even_cols = lax.broadcasted_iota(jnp.int32, (2 * blk, 2 * blk), 1) < blk
    even_lanes = lax.broadcasted_iota(jnp.int32, (2 * blk, LANES), 1) < HEAD_DIM
    nt_dims = (((1,), (1,)), ((), ()))

    def halves(s):
        return s[:, :blk], s[:, blk:]

    windows = ((kp_ref, vp_ref, 0), (km_ref, vm_ref, 0), (km_ref, vm_ref, 2 * blk), (kn_ref, vn_ref, 0))
    for sub, j in [(sub, j) for sub in range(2) for j in range(N_KV_HEADS)]:
        i = 2 * pl.program_id(0) + sub
        prev_ok = i >= CTX_ABLK + 1
        cur_ok = i >= CTX_ABLK
        next_ok = jnp.logical_and(i >= CTX_ABLK, i <= N_ABLK - 2)
        m_prev = jnp.logical_and(c >= r, prev_ok)
        m_cur = jnp.logical_and(c >= 0, cur_ok)
        m_next = jnp.logical_and(c <= r, next_ok)
        q_rows = slice(sub * blk, (sub + 1) * blk)
        o_rows = q_rows
        (kp, vp, op), (kc, vc, oc), (kn, vn, on) = windows[sub:sub + 3]
        ks = slice(j * LANES, (j + 1) * LANES)
        k_blk = lambda ref, off: ref[off:off + 2 * blk, ks]
        lhs = jnp.concatenate([q_ref[q_rows, 2 * j * LANES:(2 * j + 1) * LANES],
                               q_ref[q_rows, (2 * j + 1) * LANES:(2 * j + 2) * LANES]], axis=0)
        s_p = jnp.where(m_prev, lax.dot_general(lhs, k_blk(kp, op), nt_dims, preferred_element_type=F32), NEG_BIG)
        s_c = jnp.where(m_cur, lax.dot_general(lhs, k_blk(kc, oc), nt_dims, preferred_element_type=F32), NEG_BIG)
        s_n = jnp.where(m_next, lax.dot_general(lhs, k_blk(kn, on), nt_dims, preferred_element_type=F32), NEG_BIG)
        s_x = lax.dot_general(lhs, kx_ref[:, ks], nt_dims, preferred_element_type=F32)
        s_x0, s_x1 = s_x[:, :2 * blk], s_x[:, 2 * blk:]
        pieces = (s_p, s_c, s_n, s_x0, s_x1)
        sink_e = jnp.where(top_rows, sink_ref[4 * j], sink_ref[4 * j + 2])
        sink_o = jnp.where(top_rows, sink_ref[4 * j + 1], sink_ref[4 * j + 3])
        tile_max = functools.reduce(jnp.maximum, pieces)
        tm_e, tm_o = halves(tile_max)
        m_e = jnp.maximum(jnp.max(tm_e, axis=1, keepdims=True), sink_e)
        m_o = jnp.maximum(jnp.max(tm_o, axis=1, keepdims=True), sink_o)
        m = jnp.where(even_cols, m_e, m_o)
        probs = [jnp.exp(s - m) for s in pieces]
        ts_e, ts_o = halves(functools.reduce(jnp.add, probs))
        l_e = jnp.sum(ts_e, axis=1, keepdims=True) + jnp.exp(sink_e - m_e)
        l_o = jnp.sum(ts_o, axis=1, keepdims=True) + jnp.exp(sink_o - m_o)
        p_p, p_c, p_n, p_x0, p_x1 = [p.astype(BF16) for p in probs]
        p_x = jnp.concatenate([p_x0, p_x1], axis=1)
        o = (jnp.dot(p_p, k_blk(vp, op), preferred_element_type=F32)
             + jnp.dot(p_c, k_blk(vc, oc), preferred_element_type=F32)
             + jnp.dot(p_n, k_blk(vn, on), preferred_element_type=F32)
             + jnp.dot(p_x, vx_ref[:, ks], preferred_element_type=F32))
        o = (o / jnp.where(even_lanes, l_e, l_o)).astype(BF16)
        o_ref[o_rows, 2 * j * LANES:(2 * j + 1) * LANES] = o[:blk]
        o_ref[o_rows, (2 * j + 1) * LANES:(2 * j + 2) * LANES] = o[blk:]


def _attention(sink, q, kb, vb):
    kvw = N_KV_HEADS * LANES
    blk = ATTN_BLOCK
    lo, hi = CTX_ABLK, N_ABLK - 1
    prev_map = lambda s: (jnp.clip(2 * s - 1, lo, hi), 0)
    pair_map = lambda s: (s, 0)
    next_map = lambda s: (jnp.clip(2 * s + 2, lo, hi), 0)
    ctx_map = lambda s: (0, 0)
    kv_specs = [pl.BlockSpec((2 * blk, kvw), prev_map), pl.BlockSpec((4 * blk, kvw), pair_map),
                pl.BlockSpec((2 * blk, kvw), next_map), pl.BlockSpec((2 * CTX_LEN, kvw), ctx_map)]
    return pl.pallas_call(
        _attn_kernel,
        grid=(N_ABLK // 2,),
        in_specs=[pl.BlockSpec(memory_space=pltpu.SMEM), pl.BlockSpec((2 * blk, D_MODEL), pair_map)]
        + kv_specs + kv_specs,
        out_specs=pl.BlockSpec((2 * blk, D_MODEL), pair_map),
        out_shape=jax.ShapeDtypeStruct((T_TOK, D_MODEL), BF16),
        compiler_params=_cparams("parallel"),
        name="window_attention",
    )(sink, q, kb, kb, kb, kb, vb, vb, vb, vb)


def _gmlp_kernel(x_ref, sh_ref, sc_ref, w_ref, b_ref, g_ref, be_ref, ws_ref, bs_ref, o_ref):
    h = (x_ref[...] * (1.0 + _mod_row(sc_ref)) + _mod_row(sh_ref)).astype(BF16)
    z = jax.nn.gelu(jnp.dot(h, w_ref[...], preferred_element_type=F32) + b_ref[...], approximate=True)
    u = z[:, :D_MODEL]
    v = _layer_norm(z[:, D_MODEL:], g_ref[...], be_ref[...]).astype(BF16)
    gd = D_MODEL // GMLP_GROUPS
    for n in range(TM // CHUNK):
        rows = slice(n * CHUNK, (n + 1) * CHUNK)
        for g in range(GMLP_GROUPS):
            cols = slice(g * gd, (g + 1) * gd)
            s = jnp.dot(ws_ref[g], v[rows, cols], preferred_element_type=F32) + bs_ref[:, cols]
            o_ref[rows, cols] = (u[rows, cols] * s).astype(BF16)


def _gmlp_mixer(x, mods, layer, w_in, b_in, ln_g, ln_b, w_s, b_s):
    row = lambda i: (i, 0)
    fix2 = lambda i: (0, 0)
    return pl.pallas_call(
        _gmlp_kernel,
        grid=(N_TILES,),
        in_specs=[
            pl.BlockSpec((TM, D_MODEL), row),
            _mod_spec(layer, 0), _mod_spec(layer, 1),
            pl.BlockSpec((D_MODEL, 2 * D_MODEL), fix2),
            pl.BlockSpec((1, 2 * D_MODEL), fix2),
            pl.BlockSpec((1, D_MODEL), fix2), pl.BlockSpec((1, D_MODEL), fix2),
            pl.BlockSpec((GMLP_GROUPS, CHUNK, CHUNK), lambda i: (0, 0, 0)),
            pl.BlockSpec((CHUNK, D_MODEL), fix2),
        ],
        out_specs=pl.BlockSpec((TM, D_MODEL), row),
        out_shape=jax.ShapeDtypeStruct((T_TOK, D_MODEL), BF16),
        compiler_params=_cparams("parallel"),
        name="gmlp_mixer",
    )(x, mods, mods, w_in, b_in, ln_g, ln_b, w_s, b_s)


def _route(h2, rw_ref, rb_ref, tri_ref, carry_ref, xe_ref, meta_ref, cnt_ref):
    logits = jnp.dot(h2.astype(BF16), rw_ref[...], preferred_element_type=F32)
    s = _sigmoid(jnp.transpose(logits)[:N_EXPERTS, :])
    ssel = s + rb_ref[...]
    row = lambda v, k: v[k:k + 1, :]
    npg = EXPERTS_PER_GROUP

    def first_argmax(vals):
        best, idx = vals[0], jnp.zeros_like(vals[0])
        for k in range(1, len(vals)):
            better = vals[k] > best
            best = jnp.where(better, vals[k], best)
            idx = jnp.where(better, float(k), idx)
        return best, idx

    def pick(vals, idx):
        out = vals[-1]
        for k in range(len(vals) - 2, -1, -1):
            out = jnp.where(idx == float(k), vals[k], out)
        return out

    group_scores = []
    for g in range(N_EXPERT_GROUPS):
        a, b, c, d = (row(ssel, npg * g + k) for k in range(npg))
        group_scores.append(jnp.maximum(jnp.maximum(jnp.maximum(a + b, a + c), jnp.maximum(a + d, b + c)),
                                        jnp.maximum(b + d, c + d)))
    _, gsel = first_argmax(group_scores)
    biased = [pick([row(ssel, npg * g + k) for g in range(N_EXPERT_GROUPS)], gsel) for k in range(npg)]
    plain = [pick([row(s, npg * g + k) for g in range(N_EXPERT_GROUPS)], gsel) for k in range(npg)]
    _, i1 = first_argmax(biased)
    _, i2 = first_argmax([jnp.where(i1 == float(k), -jnp.inf, biased[k]) for k in range(npg)])
    lo = jnp.minimum(i1, i2)
    hi = jnp.maximum(i1, i2)
    s_lo = pick(plain, lo)
    s_hi = pick(plain, hi)
    den = s_lo + s_hi
    g_lo = s_lo / den
    g_hi = s_hi / den
    pair = lo * (7.0 - lo) * 0.5 + hi - lo - 1.0
    cls = gsel * PAIRS_PER_GROUP + pair
    cls_rows = lax.broadcasted_iota(jnp.int32, (CLASS_ROWS, TM), 0).astype(F32)
    onehot = cls_rows == cls
    oh_f = jnp.where(onehot, 1.0, 0.0)
    before = jnp.dot(oh_f.astype(BF16), tri_ref[...], preferred_element_type=F32)
    carry = carry_ref[...]
    rank = jnp.sum(jnp.where(onehot, before + carry[:, 0:1], 0.0), axis=0, keepdims=True)
    carry = carry + jnp.sum(oh_f, axis=1, keepdims=True)
    carry_ref[...] = carry
    cnt_ref[...] = carry
    as_bits = lambda v: lax.bitcast_convert_type(v.astype(BF16).astype(F32), jnp.uint32)
    half = D_MODEL // 2
    xe_ref[:, :half] = as_bits(h2[:, :half]) | lax.shift_right_logical(as_bits(h2[:, half:]), jnp.uint32(16))
    gate_rows = lax.broadcasted_iota(jnp.int32, (LANES, TM), 0) < LANES // 2
    xe_ref[:, half:] = lax.bitcast_convert_type(jnp.transpose(jnp.where(gate_rows, g_lo, g_hi)), jnp.uint32)
    meta_rows = lax.broadcasted_iota(jnp.int32, (8, TM), 0)
    meta_ref[...] = jnp.where(meta_rows == 0, cls, jnp.where(meta_rows == 1, rank, 0.0)).astype(jnp.int32)


def _gated_conv(x, xp_ref, xn_ref, sha_ref, sca_ref, win_ref, wc_ref):
    i = pl.program_id(0)
    halo = jnp.concatenate([x, xp_ref[...], xn_ref[...]], axis=0)
    h = (halo * (1.0 + _mod_row(sca_ref)) + _mod_row(sha_ref)).astype(BF16)
    proj = jnp.dot(h, win_ref[...], preferred_element_type=F32)
    gate = proj[:TM, :D_MODEL]
    zz_all = proj[:, D_MODEL:2 * D_MODEL] * proj[:, 2 * D_MODEL:]
    zz = zz_all[:TM]
    prev_ok = i >= CTX_TILES + 1
    next_ok = jnp.logical_and(i >= CTX_TILES, i <= N_TILES - 2)
    z_before = jnp.where(prev_ok, zz_all[TM + 7:TM + 8], 0.0)
    z_after = jnp.where(next_ok, zz_all[TM + 8:TM + 9], 0.0)
    rows = lax.broadcasted_iota(jnp.int32, (TM, 1), 0)
    zm1 = jnp.where(rows == 0, z_before, pltpu.roll(zz, 1, 0))
    zp1 = jnp.where(rows == TM - 1, z_after, pltpu.roll(zz, TM - 1, 0))
    zc = wc_ref[0:1, :] * zm1 + wc_ref[1:2, :] * zz + wc_ref[2:3, :] * zp1
    return (gate * zc).astype(BF16)


def _post_kernel(*refs, conv, n_tok):
    tok_refs, refs = refs[:n_tok], refs[n_tok:]
    if conv:
        (xp_ref, xn_ref, sha_ref, sca_ref, win_ref, wc_ref, *refs) = refs
    else:
        (a_ref, *refs) = refs
    (w_ref, ga_ref, shf_ref, scf_ref, lg_ref, lb_ref, rw_ref, rb_ref,
     x1_ref, xe_ref, meta_ref, cnt_ref, tri_ref, carry_ref) = refs

    @pl.when(pl.program_id(0) == 0)
    def _():
        carry_ref[...] = jnp.zeros_like(carry_ref)
        rr = lax.broadcasted_iota(jnp.int32, (TM, TM), 0)
        cc = lax.broadcasted_iota(jnp.int32, (TM, TM), 1)
        tri_ref[...] = jnp.where(rr < cc, 1.0, 0.0).astype(BF16)

    x = _token_tile(tok_refs)
    if conv:
        a = _gated_conv(x, xp_ref, xn_ref, sha_ref, sca_ref, win_ref, wc_ref)
    else:
        a = a_ref[...]
    y = jnp.dot(a, w_ref[...], preferred_element_type=F32)
    x1 = _layer_norm(ALPHA * x + _mod_row(ga_ref) * y, lg_ref[...], lb_ref[...])
    x1_ref[...] = x1
    h2 = x1 * (1.0 + _mod_row(scf_ref)) + _mod_row(shf_ref)
    _route(h2, rw_ref, rb_ref, tri_ref, carry_ref, xe_ref, meta_ref, cnt_ref)


def _post(mix, w_out, tokens, mods, layer, ln_g, ln_b, rw, rb, conv):
    row = lambda i: (i, 0)
    fix2 = lambda i: (0, 0)
    tile = pl.BlockSpec((TM, D_MODEL), row)
    tok_args, tok_specs = _token_operands(tokens)
    if conv:
        assert len(tok_args) == 1
        w_in, wc = mix
        sub = TM // 8
        mix_args = (tok_args[0], tok_args[0], mods, mods, w_in, wc)
        mix_specs = [pl.BlockSpec((8, D_MODEL), lambda i: (jnp.maximum(i * sub - 1, 0), 0)),
                     pl.BlockSpec((8, D_MODEL), lambda i: (jnp.minimum((i + 1) * sub, T_TOK // 8 - 1), 0)),
                     _mod_spec(layer, 0), _mod_spec(layer, 1),
                     pl.BlockSpec(w_in.shape, fix2), pl.BlockSpec((8, D_MODEL), fix2)]
    else:
        mix_args = (mix,)
        mix_specs = [tile]
    vec = pl.BlockSpec((1, D_MODEL), fix2)
    return pl.pallas_call(
        functools.partial(_post_kernel, conv=conv, n_tok=len(tok_args)),
        grid=(N_TILES,),
        in_specs=tok_specs + mix_specs + [
            pl.BlockSpec((D_MODEL, D_MODEL), fix2),
            _mod_spec(layer, 2), _mod_spec(layer, 3), _mod_spec(layer, 4),
            vec, vec,
            pl.BlockSpec((D_MODEL, LANES), fix2), pl.BlockSpec((N_EXPERTS, TM), fix2),
        ],
        out_specs=[tile, pl.BlockSpec((TM, XE_W), row), pl.BlockSpec((None, 8, TM), lambda i: (i, 0, 0)),
                   pl.BlockSpec((CLASS_ROWS, LANES), fix2)],
        out_shape=[jax.ShapeDtypeStruct((T_TOK, D_MODEL), F32),
                   jax.ShapeDtypeStruct((T_TOK, XE_W), jnp.uint32),
                   jax.ShapeDtypeStruct((N_TILES, 8, TM), jnp.int32),
                   jax.ShapeDtypeStruct((CLASS_ROWS, LANES), F32)],
        scratch_shapes=[pltpu.VMEM((TM, TM), BF16), pltpu.VMEM((CLASS_ROWS, LANES), F32)],
        compiler_params=_cparams("arbitrary"),
        name="post_conv" if conv else "post",
    )(*tok_args, *mix_args, w_out, mods, mods, mods, ln_g, ln_b, rw, rb)


def _scatter_kernel(pstart_ref, cnt_ref, dest_ref, xe_ref, xs_ref, zero_ref, sem):
    def row_copy(src, d):
        return pltpu.make_async_copy(src, xs_ref.at[pl.ds(d, 1)], sem)

    first_row = pl.program_id(0) * SCAT_ROWS

    def issue(jo, carry):
        base = pl.multiple_of(jo * 8, 8)
        group = xe_ref.at[pl.ds(base, 8)]
        for k in range(8):
            row_copy(group.at[pl.ds(k, 1)], dest_ref[first_row + base + k]).start()
        return carry

    lax.fori_loop(0, SCAT_ROWS // 8, issue, 0)

    def drain(j, carry):
        row_copy(xe_ref.at[pl.ds(0, 1)], 0).wait()
        return carry

    lax.fori_loop(0, SCAT_ROWS, drain, 0, unroll=8)

    @pl.when(pl.program_id(0) == pl.num_programs(0) - 1)
    def _():
        zero_ref[...] = jnp.zeros_like(zero_ref)

        def per_class(c, carry):
            n = cnt_ref[c]
            first = pstart_ref[c] + n
            last = pstart_ref[c] + jnp.bitwise_and(n + (MOE_BLOCK - 1), -MOE_BLOCK)
            aligned = jnp.minimum(jnp.bitwise_and(first + 7, -8), last)

            def group_copy(k):
                row = pl.multiple_of(aligned + 8 * k, 8)
                return pltpu.make_async_copy(zero_ref.at[pl.ds(0, 8)], xs_ref.at[pl.ds(row, 8)], sem)

            def fill(k, cc):
                row_copy(zero_ref.at[pl.ds(0, 1)], first + k).start()
                return cc

            def fill_wait(k, cc):
                row_copy(zero_ref.at[pl.ds(0, 1)], 0).wait()
                return cc

            def fill_group(k, cc):
                group_copy(k).start()
                return cc

            def fill_group_wait(k, cc):
                group_copy(k).wait()
                return cc

            n_groups = lax.shift_right_logical(last - aligned, 3)
            lax.fori_loop(0, aligned - first, fill, 0)
            lax.fori_loop(0, n_groups, fill_group, 0)
            lax.fori_loop(0, aligned - first, fill_wait, 0)
            lax.fori_loop(0, n_groups, fill_group_wait, 0)
            return carry

        lax.fori_loop(0, N_CLASSES, per_class, 0)

        last = N_CLASSES - 1
        rows_used = pstart_ref[last] + jnp.bitwise_and(cnt_ref[last] + (MOE_BLOCK - 1), -MOE_BLOCK)
        first_free = lax.shift_right_logical(rows_used, MOE_BLOCK.bit_length() - 1)

        def block_copy(blk):
            return pltpu.make_async_copy(zero_ref, xs_ref.at[pl.ds(blk * MOE_BLOCK, MOE_BLOCK)], sem)

        def fill_block(blk, cc):
            block_copy(blk).start()
            return cc

        def fill_block_wait(blk, cc):
            block_copy(0).wait()
            return cc

        lax.fori_loop(first_free, N_MOE_BLOCKS, fill_block, 0)
        lax.fori_loop(first_free, N_MOE_BLOCKS, fill_block_wait, 0)


def _scatter_rows(pstart, cnt, dest, xe):
    grid_spec = pltpu.PrefetchScalarGridSpec(
        num_scalar_prefetch=2,
        grid=(T_TOK // SCAT_ROWS,),
        in_specs=[pl.BlockSpec((T_TOK,), lambda i, ps, cn: (0,), memory_space=pltpu.SMEM),
                  pl.BlockSpec((SCAT_ROWS, XE_W), lambda i, ps, cn: (i, 0))],
        out_specs=pl.BlockSpec(memory_space=pl.ANY),
        scratch_shapes=[pltpu.VMEM((MOE_BLOCK, XE_W), xe.dtype), pltpu.SemaphoreType.DMA(())],
    )
    return pl.pallas_call(
        _scatter_kernel,
        grid_spec=grid_spec,
        out_shape=jax.ShapeDtypeStruct((P_ROWS, XE_W), xe.dtype),
        compiler_params=_cparams("arbitrary"),
        name="scatter_rows",
    )(pstart, cnt, dest.reshape(T_TOK), xe)


def _expert_kernel(lo_ref, hi_ref, need_ref, nused_ref, x_ref, w1_hbm, w3_hbm, w2_hbm, o_ref,
                   st1, st3, st2, w1s, w3s, w2s, sems, done_ref, *, layer):
    b = pl.program_id(0)

    def fetch(k, buf):
        return (pltpu.make_async_copy(w1_hbm.at[layer, k], st1.at[buf], sems.at[buf]),
                pltpu.make_async_copy(w3_hbm.at[layer, k], st3.at[buf], sems.at[buf]),
                pltpu.make_async_copy(w2_hbm.at[layer, k], st2.at[buf], sems.at[buf]))

    @pl.when(b == 0)
    def _():
        done_ref[0] = 0
        for k in range(N_STAGING):
            for cp in fetch(k, k):
                cp.start()

    used = b < nused_ref[0]

    @pl.when(jnp.logical_not(used))
    def _():
        o_ref[...] = jnp.zeros_like(o_ref)

    @pl.when(used)
    def _():
        def install(k, carry):
            buf = k % N_STAGING
            slot = k % EXPERTS_PER_GROUP
            for cp in fetch(k, buf):
                cp.wait()
            w1s[slot] = st1[buf].astype(BF16)
            w3s[slot] = st3[buf].astype(BF16)
            w2s[slot] = st2[buf].astype(BF16)

            @pl.when(k + N_STAGING < N_EXPERTS)
            def _():
                for cp in fetch(k + N_STAGING, buf):
                    cp.start()

            return carry

        done = done_ref[0]
        lax.fori_loop(done, need_ref[b], install, 0)
        done_ref[0] = jnp.maximum(done, need_ref[b])

        half = D_MODEL // 2
        words = x_ref[:, :half]
        as_f32 = lambda bits: lax.bitcast_convert_type(bits, F32)
        xb = jnp.concatenate([as_f32(words & jnp.uint32(0xFFFF0000)),
                              as_f32(lax.shift_left(words, jnp.uint32(16)))], axis=1).astype(BF16)
        gates = as_f32(x_ref[:, half:])

        def expert(slot):
            h1 = jnp.dot(xb, w1s[slot], preferred_element_type=F32)
            h3 = jnp.dot(xb, w3s[slot], preferred_element_type=F32)
            act = (h1 * _sigmoid(h1) * h3).astype(BF16)
            return jnp.dot(act, w2s[slot], preferred_element_type=F32)

        ya = expert(lo_ref[b])
        yb = expert(hi_ref[b])
        o_ref[...] = gates[:, 0:1] * ya + gates[:, LANES // 2:LANES // 2 + 1] * yb

    @pl.when(b == pl.num_programs(0) - 1)
    def _():
        def drain(k, carry):
            for cp in fetch(k, k % N_STAGING):
                cp.wait()
            return carry

        done = done_ref[0]
        lax.fori_loop(done, jnp.minimum(done + N_STAGING, N_EXPERTS), drain, 0)


def _experts(lo_slot, hi_slot, need, nused, xs, w1, w3, w2, layer):
    rows = lambda b, lo, hi, nd, nu: (jnp.maximum(jnp.minimum(b, nu[0] - 1), 0), 0)
    any_spec = pl.BlockSpec(memory_space=pl.ANY)
    up, down = (D_MODEL, D_EXPERT), (D_EXPERT, D_MODEL)
    grid_spec = pltpu.PrefetchScalarGridSpec(
        num_scalar_prefetch=4,
        grid=(N_MOE_BLOCKS,),
        in_specs=[pl.BlockSpec((MOE_BLOCK, XE_W), rows), any_spec, any_spec, any_spec],
        out_specs=pl.BlockSpec((MOE_BLOCK, D_MODEL), lambda b, lo, hi, nd, nu: (b, 0)),
        scratch_shapes=[pltpu.VMEM((N_STAGING,) + up, F32), pltpu.VMEM((N_STAGING,) + up, F32),
                        pltpu.VMEM((N_STAGING,) + down, F32),
                        pltpu.VMEM((EXPERTS_PER_GROUP,) + up, BF16), pltpu.VMEM((EXPERTS_PER_GROUP,) + up, BF16),
                        pltpu.VMEM((EXPERTS_PER_GROUP,) + down, BF16),
                        pltpu.SemaphoreType.DMA((N_STAGING,)), pltpu.SMEM((1,), jnp.int32)],
    )
    return pl.pallas_call(
        functools.partial(_expert_kernel, layer=layer),
        grid_spec=grid_spec,
        out_shape=jax.ShapeDtypeStruct((P_ROWS, D_MODEL), F32),
        compiler_params=_cparams("arbitrary"),
        name="expert_pairs",
    )(lo_slot, hi_slot, need, nused, xs, w1, w3, w2)


def _ln2_kernel(dest_ref, dest_next_ref, ys_ref, x_ref, gf_ref, lg_ref, lb_ref, o_ref, ybuf, sems):
    i = pl.program_id(0)
    slot = i % 2

    def row_copy(src_row, dst_rows, dst_slot):
        return pltpu.make_async_copy(ys_ref.at[pl.ds(src_row, 1)], dst_rows, sems.at[dst_slot])

    def issue_tile(idx_ref, dst_slot):
        def issue(jo, carry):
            base = pl.multiple_of(jo * 8, 8)
            group = ybuf.at[dst_slot, pl.ds(base, 8)]
            for k in range(8):
                row_copy(idx_ref[base + k], group.at[pl.ds(k, 1)], dst_slot).start()
            return carry

        lax.fori_loop(0, TM // 8, issue, 0)

    @pl.when(i == 0)
    def _():
        issue_tile(dest_ref, 0)

    @pl.when(i + 1 < pl.num_programs(0))
    def _():
        issue_tile(dest_next_ref, 1 - slot)

    def drain(j, carry):
        row_copy(0, ybuf.at[slot, pl.ds(0, 1)], slot).wait()
        return carry

    lax.fori_loop(0, TM, drain, 0, unroll=8)
    o_ref[...] = _layer_norm(ALPHA * x_ref[...] + _mod_row(gf_ref) * ybuf[slot], lg_ref[...], lb_ref[...])


def _ln2(x1, ys, dest, mods, layer, ln_g, ln_b, latent_only):
    row = lambda i: (i, 0)
    tile = pl.BlockSpec((TM, D_MODEL), row)
    vec = pl.BlockSpec((1, D_MODEL), lambda i: (0, 0))
    idx = lambda f: pl.BlockSpec((TM,), f, memory_space=pltpu.SMEM)
    if latent_only:
        out_rows = SEQ
        out_spec = pl.BlockSpec((TM, D_MODEL), lambda i: (jnp.maximum(i - CTX_TILES, 0), 0))
    else:
        out_rows = T_TOK
        out_spec = tile
    return pl.pallas_call(
        _ln2_kernel,
        grid=(N_TILES,),
        in_specs=[idx(lambda i: (i,)), idx(lambda i: (jnp.minimum(i + 1, N_TILES - 1),)),
                  pl.BlockSpec(memory_space=pl.ANY), tile, _mod_spec(layer, 5), vec, vec],
        out_specs=out_spec,
        out_shape=jax.ShapeDtypeStruct((out_rows, D_MODEL), F32),
        scratch_shapes=[pltpu.VMEM((2, TM, D_MODEL), F32), pltpu.SemaphoreType.DMA((2,))],
        compiler_params=_cparams("arbitrary"),
        name="moe_gather_residual_ln",
    )(dest.reshape(T_TOK), dest.reshape(T_TOK), ys, x1, mods, ln_g, ln_b)


def _pair_tables():
    pairs = [(a, b) for a in range(EXPERTS_PER_GROUP) for b in range(a + 1, EXPERTS_PER_GROUP)]
    return jnp.array([p[0] for p in pairs], jnp.int32), jnp.array([p[1] for p in pairs], jnp.int32)


def _dispatch_plan(meta, counts):
    cnt = counts[:N_CLASSES, 0].astype(jnp.int32)
    padded = (cnt + MOE_BLOCK - 1) // MOE_BLOCK * MOE_BLOCK
    pad_end = jnp.cumsum(padded)
    pad_start = pad_end - padded
    nused = pad_end[-1:] // MOE_BLOCK
    cls, rank = meta[:, 0:1, :], meta[:, 1:2, :]
    before = (cls[..., None] > jnp.arange(N_CLASSES, dtype=jnp.int32)).astype(jnp.int32)
    dest = rank + jnp.sum(before * padded, axis=-1)
    blk_row = jnp.arange(N_MOE_BLOCKS, dtype=jnp.int32) * MOE_BLOCK
    blk_cls = jnp.minimum(jnp.sum((pad_end[None, :] <= blk_row[:, None]).astype(jnp.int32), axis=1),
                          N_CLASSES - 1)
    pair_lo, pair_hi = _pair_tables()
    lo_slot = pair_lo[blk_cls % PAIRS_PER_GROUP]
    hi_slot = pair_hi[blk_cls % PAIRS_PER_GROUP]
    need = (blk_cls // PAIRS_PER_GROUP) * EXPERTS_PER_GROUP + hi_slot + 1
    return pad_start, cnt, dest, lo_slot, hi_slot, need, nused


def _rope_tables():
    n_rows = SEQ // GRID_W
    freqs = jnp.power(ROPE_BASE, -jnp.arange(ROPE_FREQS, dtype=F32) / ROPE_FREQS)
    ar = jnp.arange(n_rows).astype(F32)[:, None] * freqs[None, :]
    ac = jnp.arange(GRID_W).astype(F32)[:, None] * freqs[None, :]

    def table(fn):
        by_row = jnp.broadcast_to(fn(ar)[:, None, :], (n_rows, GRID_W, ROPE_FREQS)).reshape(SEQ, ROPE_FREQS)
        by_col = jnp.broadcast_to(fn(ac)[None, :, :], (n_rows, GRID_W, ROPE_FREQS)).reshape(SEQ, ROPE_FREQS)
        return jnp.concatenate([by_row, by_row, by_col, by_col], axis=-1)

    cos, sin = table(jnp.cos), table(jnp.sin)
    cos = jnp.concatenate([jnp.ones((CTX_LEN, HEAD_DIM), F32), cos], axis=0)
    sin = jnp.concatenate([jnp.zeros((CTX_LEN, HEAD_DIM), F32), sin], axis=0)
    first_half = (jnp.arange(HEAD_DIM) % (2 * ROPE_FREQS)) < ROPE_FREQS
    sa = jnp.where(first_half[None, :], -sin, 0.0)
    sb = jnp.where(first_half[None, :], 0.0, sin)
    two = lambda t: jnp.concatenate([t, t], axis=-1)
    return two(cos), two(sa), two(sb)


def kernel(x, c, ctx, c_ctx, w_mod, b_mod, ln1_g, ln1_b, ln2_g, ln2_b, router_w, router_bias, moe_w1, moe_w3, moe_w2, a_w_qkv, a_w_o, a_sink, b_w_in, b_b_in, b_ln_g, b_ln_b, b_w_s, b_b_s, b_w_out, c_w_in, c_w_conv, c_w_out):
    assert x.shape == (1, SEQ, D_MODEL) and ctx.shape == (1, CTX_LEN, D_MODEL)
    tok = (ctx[0], x[0])
    cc = jnp.zeros((8, D_MODEL), F32).at[0].set(c[0]).at[1].set(c_ctx)
    mods = _modulation(cc, w_mod, b_mod)
    cos, sa, sb = _rope_tables()
    rw = jnp.pad(router_w, ((0, 0), (0, LANES - N_EXPERTS))).astype(BF16)
    rb = jnp.broadcast_to(router_bias.astype(F32)[:, None], (N_EXPERTS, TM))

    for i in range(DEPTH):
        kind, j = i % N_MIXERS, i // N_MIXERS
        if kind == 0:
            q, kb, vb = _qkv_project(tok, mods, i, a_w_qkv[j].astype(BF16), cos, sa, sb)
            mix = _attention(a_sink[j], q, kb, vb)
            w_out = a_w_o[j]
        elif kind == 1:
            bs = jnp.repeat(b_b_s[j], D_MODEL // GMLP_GROUPS, axis=1)
            mix = _gmlp_mixer(tok, mods, i, b_w_in[j].astype(BF16), b_b_in[j].reshape(1, -1),
                              b_ln_g[j].reshape(1, -1), b_ln_b[j].reshape(1, -1), b_w_s[j].astype(BF16), bs)
            w_out = b_w_out[j]
        else:
            mix = (c_w_in[j].astype(BF16), jnp.pad(c_w_conv[j], ((0, 5), (0, 0))))
            w_out = c_w_out[j]
        x1, xe, meta, counts = _post(mix, w_out.astype(BF16), tok, mods, i,
                                     ln1_g[i].reshape(1, -1), ln1_b[i].reshape(1, -1), rw, rb, conv=(kind == 2))
        pad_start, cnt, dest, lo_slot, hi_slot, need, nused = _dispatch_plan(meta, counts)
        xs = _scatter_rows(pad_start, cnt, dest, xe)
        ys = _experts(lo_slot, hi_slot, need, nused, xs, moe_w1, moe_w3, moe_w2, i)
        tok = _ln2(x1, ys, dest, mods, i, ln2_g[i].reshape(1, -1), ln2_b[i].reshape(1, -1),
                   latent_only=(i == DEPTH - 1))
    return tok.reshape(1, SEQ, D_MODEL)
```

```python
import functools

import jax
import jax.numpy as jnp
from jax import lax
from jax.experimental import pallas as pl
from jax.experimental.pallas import tpu as pltpu

F32 = jnp.float32
BF16 = jnp.bfloat16

D_MODEL = 1024
SEQ = 16384
DEPTH = 4
GRID_W = 64
CTX_LEN = 256
N_MIXERS = 3
N_HEADS = 16
N_KV_HEADS = 4
HEAD_DIM = 64
ATTN_BLOCK = 128
ATTN_SCALE = HEAD_DIM ** -0.5
ROPE_BASE = 10000.0
ROPE_FREQS = HEAD_DIM // 4
CHUNK = 128
GMLP_GROUPS = 8
N_EXPERTS = 16
N_EXPERT_GROUPS = 4
EXPERTS_PER_GROUP = 4
D_EXPERT = 512
ALPHA = (2 * DEPTH) ** 0.25
LN_EPS = 1e-5

LANES = 128
T_TOK = CTX_LEN + SEQ
TM = 256
N_TILES = T_TOK // TM
CTX_TILES = CTX_LEN // TM
N_ABLK = T_TOK // ATTN_BLOCK
CTX_ABLK = CTX_LEN // ATTN_BLOCK
PAIRS_PER_GROUP = 6
N_CLASSES = N_EXPERT_GROUPS * PAIRS_PER_GROUP
CLASS_ROWS = 32
MOE_BLOCK = 256
N_STAGING = 2
N_MOE_BLOCKS = T_TOK // MOE_BLOCK + N_CLASSES
P_ROWS = N_MOE_BLOCKS * MOE_BLOCK
SCAT_ROWS = 5 * TM
XE_W = D_MODEL // 2 + LANES
NEG_BIG = -1e30
VMEM_LIMIT = 52 * 1024 * 1024


def _cparams(sem="arbitrary"):
    return pltpu.CompilerParams(dimension_semantics=(sem,), vmem_limit_bytes=VMEM_LIMIT)


def _mod_row(ref):
    is_ctx = pl.program_id(0) < CTX_TILES
    return jnp.where(is_ctx, ref[1:2, :], ref[0:1, :])


def _layer_norm(x, g, b):
    mu = jnp.mean(x, axis=-1, keepdims=True)
    xc = x - mu
    var = jnp.mean(xc * xc, axis=-1, keepdims=True)
    return xc * lax.rsqrt(var + LN_EPS) * g + b


def _sigmoid(x):
    return 1.0 / (1.0 + jnp.exp(-x))


def _mod_kernel(cc_ref, w_ref, b_ref, o_ref):
    cc = cc_ref[...]
    act = cc * _sigmoid(cc)
    o_ref[...] = jnp.dot(act.astype(BF16), w_ref[...].astype(BF16), preferred_element_type=F32) + b_ref[...]


def _modulation(cc, w_mod, b_mod):
    nt = 1536
    return pl.pallas_call(
        _mod_kernel,
        grid=(DEPTH, 6 * D_MODEL // nt),
        in_specs=[
            pl.BlockSpec((8, D_MODEL), lambda l, n: (0, 0)),
            pl.BlockSpec((None, D_MODEL, nt), lambda l, n: (l, 0, n)),
            pl.BlockSpec((None, 1, nt), lambda l, n: (l, 0, n)),
        ],
        out_specs=pl.BlockSpec((None, 8, nt), lambda l, n: (l, 0, n)),
        out_shape=jax.ShapeDtypeStruct((DEPTH, 8, 6 * D_MODEL), F32),
        compiler_params=pltpu.CompilerParams(
            dimension_semantics=("arbitrary", "arbitrary"), vmem_limit_bytes=VMEM_LIMIT),
        name="modulation",
    )(cc, w_mod, b_mod.reshape(DEPTH, 1, 6 * D_MODEL))


def _mod_spec(layer, chunk):
    return pl.BlockSpec((None, 8, D_MODEL), lambda i: (layer, 0, chunk))


def _store_split_heads(ref, tile, pair):
    low = lax.broadcasted_iota(jnp.int32, tile.shape, 1) < HEAD_DIM
    swapped = pltpu.roll(tile, HEAD_DIM, 1)
    parts = ((jnp.where(low, tile, 0.0), jnp.where(low, 0.0, swapped)),
             (jnp.where(low, swapped, 0.0), jnp.where(low, 0.0, tile)))
    blk = ATTN_BLOCK
    for h, (in_low, in_high) in enumerate(parts):
        cols = slice((2 * pair + h) * LANES, (2 * pair + h + 1) * LANES)
        for n in range(TM // blk):
            rows = slice(n * blk, (n + 1) * blk)
            ref[2 * n * blk:(2 * n + 1) * blk, cols] = in_low[rows].astype(BF16)
            ref[(2 * n + 1) * blk:(2 * n + 2) * blk, cols] = in_high[rows].astype(BF16)


def _token_operands(tokens):
    if isinstance(tokens, tuple):
        specs = [pl.BlockSpec((TM, D_MODEL), lambda i: (jnp.minimum(i, CTX_TILES - 1), 0)),
                 pl.BlockSpec((TM, D_MODEL), lambda i: (jnp.maximum(i - CTX_TILES, 0), 0))]
        return list(tokens), specs
    return [tokens], [pl.BlockSpec((TM, D_MODEL), lambda i: (i, 0))]


def _token_tile(tok_refs):
    if len(tok_refs) == 2:
        return jnp.where(pl.program_id(0) < CTX_TILES, tok_refs[0][...], tok_refs[1][...])
    return tok_refs[0][...]


def _qkv_kernel(*refs, n_tok):
    tok_refs = refs[:n_tok]
    sh_ref, sc_ref, w_ref, cos_ref, sa_ref, sb_ref, q_ref, k_ref, v_ref = refs[n_tok:]
    h = (_token_tile(tok_refs) * (1.0 + _mod_row(sc_ref)) + _mod_row(sh_ref)).astype(BF16)
    y = jnp.dot(h, w_ref[...], preferred_element_type=F32)
    cos, sa, sb = cos_ref[...], sa_ref[...], sb_ref[...]
    n_q = D_MODEL // LANES
    n_k = N_KV_HEADS * HEAD_DIM // LANES
    for t in range(n_q + n_k):
        yt = y[:, t * LANES:(t + 1) * LANES]
        r = yt * cos + pltpu.roll(yt, LANES - ROPE_FREQS, 1) * sa + pltpu.roll(yt, ROPE_FREQS, 1) * sb
        if t < n_q:
            q_ref[:, t * LANES:(t + 1) * LANES] = (r * ATTN_SCALE).astype(BF16)
        else:
            _store_split_heads(k_ref, r, t - n_q)
    for t in range(n_k):
        _store_split_heads(v_ref, y[:, (n_q + n_k + t) * LANES:(n_q + n_k + t + 1) * LANES], t)


def _qkv_project(tokens, mods, layer, w, cos, sa, sb):
    kvw = N_KV_HEADS * LANES
    row = lambda i: (i, 0)
    tok_args, tok_specs = _token_operands(tokens)
    return pl.pallas_call(
        functools.partial(_qkv_kernel, n_tok=len(tok_args)),
        grid=(N_TILES,),
        in_specs=tok_specs + [
            _mod_spec(layer, 0), _mod_spec(layer, 1),
            pl.BlockSpec(w.shape, lambda i: (0, 0)),
            pl.BlockSpec((TM, LANES), row), pl.BlockSpec((TM, LANES), row), pl.BlockSpec((TM, LANES), row),
        ],
        out_specs=[pl.BlockSpec((TM, D_MODEL), row), pl.BlockSpec((2 * TM, kvw), row),
                   pl.BlockSpec((2 * TM, kvw), row)],
        out_shape=[jax.ShapeDtypeStruct((T_TOK, D_MODEL), BF16),
                   jax.ShapeDtypeStruct((2 * T_TOK, kvw), BF16),
                   jax.ShapeDtypeStruct((2 * T_TOK, kvw), BF16)],
        compiler_params=_cparams("parallel"),
        name="qkv_rope",
    )(*tok_args, mods, mods, w, cos, sa, sb)


def _attn_kernel(sink_ref, q_ref, kp_ref, km_ref, kn_ref, kx_ref, vp_ref, vm_ref, vn_ref, vx_ref, o_ref):
    blk = ATTN_BLOCK
    r = lax.broadcasted_iota(jnp.int32, (2 * blk, 2 * blk), 0) & (blk - 1)
    c = lax.broadcasted_iota(jnp.int32, (2 * blk, 2 * blk), 1) & (blk - 1)
    top_rows = lax.broadcasted_iota(jnp.int32, (2 * blk, 1), 0) < blk
    even_cols = lax.broadcasted_iota(jnp.int32, (2 * blk, 2 * blk), 1) < blk
    even_lanes = lax.broadcasted_iota(jnp.int32, (2 * blk, LANES), 1) < HEAD_DIM
    nt_dims = (((1,), (1,)), ((), ()))

    def halves(s):
        return s[:, :blk], s[:, blk:]

    windows = ((kp_ref, vp_ref, 0), (km_ref, vm_ref, 0), (km_ref, vm_ref, 2 * blk), (kn_ref, vn_ref, 0))
    for sub, j in [(sub, j) for sub in range(2) for j in range(N_KV_HEADS)]:
        i = 2 * pl.program_id(0) + sub
        prev_ok = i >= CTX_ABLK + 1
        cur_ok = i >= CTX_ABLK
        next_ok = jnp.logical_and(i >= CTX_ABLK, i <= N_ABLK - 2)
        m_prev = jnp.logical_and(c >= r, prev_ok)
        m_cur = jnp.logical_and(c >= 0, cur_ok)
        m_next = jnp.logical_and(c <= r, next_ok)
        q_rows = slice(sub * blk, (sub + 1) * blk)
        o_rows = q_rows
        (kp, vp, op), (kc, vc, oc), (kn, vn, on) = windows[sub:sub + 3]
        ks = slice(j * LANES, (j + 1) * LANES)
        k_blk = lambda ref, off: ref[off:off + 2 * blk, ks]
        lhs = jnp.concatenate([q_ref[q_rows, 2 * j * LANES:(2 * j + 1) * LANES],
                               q_ref[q_rows, (2 * j + 1) * LANES:(2 * j + 2) * LANES]], axis=0)
        s_p = jnp.where(m_prev, lax.dot_general(lhs, k_blk(kp, op), nt_dims, preferred_element_type=F32), NEG_BIG)
        s_c = jnp.where(m_cur, lax.dot_general(lhs, k_blk(kc, oc), nt_dims, preferred_element_type=F32), NEG_BIG)
        s_n = jnp.where(m_next, lax.dot_general(lhs, k_blk(kn, on), nt_dims, preferred_element_type=F32), NEG_BIG)
        s_x = lax.dot_general(lhs, kx_ref[:, ks], nt_dims, preferred_element_type=F32)
        s_x0, s_x1 = s_x[:, :2 * blk], s_x[:, 2 * blk:]
        pieces = (s_p, s_c, s_n, s_x0, s_x1)
        sink_e = jnp.where(top_rows, sink_ref[4 * j], sink_ref[4 * j + 2])
        sink_o = jnp.where(top_rows, sink_ref[4 * j + 1], sink_ref[4 * j + 3])
        tile_max = functools.reduce(jnp.maximum, pieces)
        tm_e, tm_o = halves(tile_max)
        m_e = jnp.maximum(jnp.max(tm_e, axis=1, keepdims=True), sink_e)
        m_o = jnp.maximum(jnp.max(tm_o, axis=1, keepdims=True), sink_o)
        m = jnp.where(even_cols, m_e, m_o)
        probs = [jnp.exp(s - m) for s in pieces]
        ts_e, ts_o = halves(functools.reduce(jnp.add, probs))
        l_e = jnp.sum(ts_e, axis=1, keepdims=True) + jnp.exp(sink_e - m_e)
        l_o = jnp.sum(ts_o, axis=1, keepdims=True) + jnp.exp(sink_o - m_o)
        p_p, p_c, p_n, p_x0, p_x1 = [p.astype(BF16) for p in probs]
        p_x = jnp.concatenate([p_x0, p_x1], axis=1)
        o = (jnp.dot(p_p, k_blk(vp, op), preferred_element_type=F32)
             + jnp.dot(p_c, k_blk(vc, oc), preferred_element_type=F32)
             + jnp.dot(p_n, k_blk(vn, on), preferred_element_type=F32)
             + jnp.dot(p_x, vx_ref[:, ks], preferred_element_type=F32))
        o = (o / jnp.where(even_lanes, l_e, l_o)).astype(BF16)
        o_ref[o_rows, 2 * j * LANES:(2 * j + 1) * LANES] = o[:blk]
        o_ref[o_rows, (2 * j + 1) * LANES:(2 * j + 2) * LANES] = o[blk:]


def _attention(sink, q, kb, vb):
    kvw = N_KV_HEADS * LANES
    blk = ATTN_BLOCK
    lo, hi = CTX_ABLK, N_ABLK - 1
    prev_map = lambda s: (jnp.clip(2 * s - 1, lo, hi), 0)
    pair_map = lambda s: (s, 0)
    next_map = lambda s: (jnp.clip(2 * s + 2, lo, hi), 0)
    ctx_map = lambda s: (0, 0)
    kv_specs = [pl.BlockSpec((2 * blk, kvw), prev_map), pl.BlockSpec((4 * blk, kvw), pair_map),
                pl.BlockSpec((2 * blk, kvw), next_map), pl.BlockSpec((2 * CTX_LEN, kvw), ctx_map)]
    return pl.pallas_call(
        _attn_kernel,
        grid=(N_ABLK // 2,),
        in_specs=[pl.BlockSpec(memory_space=pltpu.SMEM), pl.BlockSpec((2 * blk, D_MODEL), pair_map)]
        + kv_specs + kv_specs,
        out_specs=pl.BlockSpec((2 * blk, D_MODEL), pair_map),
        out_shape=jax.ShapeDtypeStruct((T_TOK, D_MODEL), BF16),
        compiler_params=_cparams("parallel"),
        name="window_attention",
    )(sink, q, kb, kb, kb, kb, vb, vb, vb, vb)


def _gmlp_kernel(x_ref, sh_ref, sc_ref, w_ref, b_ref, g_ref, be_ref, ws_ref, bs_ref, o_ref):
    h = (x_ref[...] * (1.0 + _mod_row(sc_ref)) + _mod_row(sh_ref)).astype(BF16)
    z = jax.nn.gelu(jnp.dot(h, w_ref[...], preferred_element_type=F32) + b_ref[...], approximate=True)
    u = z[:, :D_MODEL]
    v = _layer_norm(z[:, D_MODEL:], g_ref[...], be_ref[...]).astype(BF16)
    gd = D_MODEL // GMLP_GROUPS
    for n in range(TM // CHUNK):
        rows = slice(n * CHUNK, (n + 1) * CHUNK)
        for g in range(GMLP_GROUPS):
            cols = slice(g * gd, (g + 1) * gd)
            s = jnp.dot(ws_ref[g], v[rows, cols], preferred_element_type=F32) + bs_ref[:, cols]
            o_ref[rows, cols] = (u[rows, cols] * s).astype(BF16)


def _gmlp_mixer(x, mods, layer, w_in, b_in, ln_g, ln_b, w_s, b_s):
    row = lambda i: (i, 0)
    fix2 = lambda i: (0, 0)
    return pl.pallas_call(
        _gmlp_kernel,
        grid=(N_TILES,),
        in_specs=[
            pl.BlockSpec((TM, D_MODEL), row),
            _mod_spec(layer, 0), _mod_spec(layer, 1),
            pl.BlockSpec((D_MODEL, 2 * D_MODEL), fix2),
            pl.BlockSpec((1, 2 * D_MODEL), fix2),
            pl.BlockSpec((1, D_MODEL), fix2), pl.BlockSpec((1, D_MODEL), fix2),
            pl.BlockSpec((GMLP_GROUPS, CHUNK, CHUNK), lambda i: (0, 0, 0)),
            pl.BlockSpec((CHUNK, D_MODEL), fix2),
        ],
        out_specs=pl.BlockSpec((TM, D_MODEL), row),
        out_shape=jax.ShapeDtypeStruct((T_TOK, D_MODEL), BF16),
        compiler_params=_cparams("parallel"),
        name="gmlp_mixer",
    )(x, mods, mods, w_in, b_in, ln_g, ln_b, w_s, b_s)


def _route(h2, rw_ref, rb_ref, tri_ref, carry_ref, xe_ref, meta_ref, cnt_ref):
    logits = jnp.dot(h2.astype(BF16), rw_ref[...], preferred_element_type=F32)
    s = _sigmoid(jnp.transpose(logits)[:N_EXPERTS, :])
    ssel = s + rb_ref[...]
    row = lambda v, k: v[k:k + 1, :]
    npg = EXPERTS_PER_GROUP

    def first_argmax(vals):
        best, idx = vals[0], jnp.zeros_like(vals[0])
        for k in range(1, len(vals)):
            better = vals[k] > best
            best = jnp.where(better, vals[k], best)
            idx = jnp.where(better, float(k), idx)
        return best, idx

    def pick(vals, idx):
        out = vals[-1]
        for k in range(len(vals) - 2, -1, -1):
            out = jnp.where(idx == float(k), vals[k], out)
        return out

    group_scores = []
    for g in range(N_EXPERT_GROUPS):
        a, b, c, d = (row(ssel, npg * g + k) for k in range(npg))
        group_scores.append(jnp.maximum(jnp.maximum(jnp.maximum(a + b, a + c), jnp.maximum(a + d, b + c)),
                                        jnp.maximum(b + d, c + d)))
    _, gsel = first_argmax(group_scores)
    biased = [pick([row(ssel, npg * g + k) for g in range(N_EXPERT_GROUPS)], gsel) for k in range(npg)]
    plain = [pick([row(s, npg * g + k) for g in range(N_EXPERT_GROUPS)], gsel) for k in range(npg)]
    _, i1 = first_argmax(biased)
    _, i2 = first_argmax([jnp.where(i1 == float(k), -jnp.inf, biased[k]) for k in range(npg)])
    lo = jnp.minimum(i1, i2)
    hi = jnp.maximum(i1, i2)
    s_lo = pick(plain, lo)
    s_hi = pick(plain, hi)
    den = s_lo + s_hi
    g_lo = s_lo / den
    g_hi = s_hi / den
    pair = lo * (7.0 - lo) * 0.5 + hi - lo - 1.0
    cls = gsel * PAIRS_PER_GROUP + pair
    cls_rows = lax.broadcasted_iota(jnp.int32, (CLASS_ROWS, TM), 0).astype(F32)
    onehot = cls_rows == cls
    oh_f = jnp.where(onehot, 1.0, 0.0)
    before = jnp.dot(oh_f.astype(BF16), tri_ref[...], preferred_element_type=F32)
    carry = carry_ref[...]
    rank = jnp.sum(jnp.where(onehot, before + carry[:, 0:1], 0.0), axis=0, keepdims=True)
    carry = carry + jnp.sum(oh_f, axis=1, keepdims=True)
    carry_ref[...] = carry
    cnt_ref[...] = carry
    as_bits = lambda v: lax.bitcast_convert_type(v.astype(BF16).astype(F32), jnp.uint32)
    half = D_MODEL // 2
    xe_ref[:, :half] = as_bits(h2[:, :half]) | lax.shift_right_logical(as_bits(h2[:, half:]), jnp.uint32(16))
    gate_rows = lax.broadcasted_iota(jnp.int32, (LANES, TM), 0) < LANES // 2
    xe_ref[:, half:] = lax.bitcast_convert_type(jnp.transpose(jnp.where(gate_rows, g_lo, g_hi)), jnp.uint32)
    meta_rows = lax.broadcasted_iota(jnp.int32, (8, TM), 0)
    meta_ref[...] = jnp.where(meta_rows == 0, cls, jnp.where(meta_rows == 1, rank, 0.0)).astype(jnp.int32)


def _gated_conv(x, xp_ref, xn_ref, sha_ref, sca_ref, win_ref, wc_ref):
    i = pl.program_id(0)
    halo = jnp.concatenate([x, xp_ref[...], xn_ref[...]], axis=0)
    h = (halo * (1.0 + _mod_row(sca_ref)) + _mod_row(sha_ref)).astype(BF16)
    proj = jnp.dot(h, win_ref[...], preferred_element_type=F32)
    gate = proj[:TM, :D_MODEL]
    zz_all = proj[:, D_MODEL:2 * D_MODEL] * proj[:, 2 * D_MODEL:]
    zz = zz_all[:TM]
    prev_ok = i >= CTX_TILES + 1
    next_ok = jnp.logical_and(i >= CTX_TILES, i <= N_TILES - 2)
    z_before = jnp.where(prev_ok, zz_all[TM + 7:TM + 8], 0.0)
    z_after = jnp.where(next_ok, zz_all[TM + 8:TM + 9], 0.0)
    rows = lax.broadcasted_iota(jnp.int32, (TM, 1), 0)
    zm1 = jnp.where(rows == 0, z_before, pltpu.roll(zz, 1, 0))
    zp1 = jnp.where(rows == TM - 1, z_after, pltpu.roll(zz, TM - 1, 0))
    zc = wc_ref[0:1, :] * zm1 + wc_ref[1:2, :] * zz + wc_ref[2:3, :] * zp1
    return (gate * zc).astype(BF16)


def _post_kernel(*refs, conv, n_tok):
    tok_refs, refs = refs[:n_tok], refs[n_tok:]
    if conv:
        (xp_ref, xn_ref, sha_ref, sca_ref, win_ref, wc_ref, *refs) = refs
    else:
        (a_ref, *refs) = refs
    (w_ref, ga_ref, shf_ref, scf_ref, lg_ref, lb_ref, rw_ref, rb_ref,
     x1_ref, xe_ref, meta_ref, cnt_ref, tri_ref, carry_ref) = refs

    @pl.when(pl.program_id(0) == 0)
    def _():
        carry_ref[...] = jnp.zeros_like(carry_ref)
        rr = lax.broadcasted_iota(jnp.int32, (TM, TM), 0)
        cc = lax.broadcasted_iota(jnp.int32, (TM, TM), 1)
        tri_ref[...] = jnp.where(rr < cc, 1.0, 0.0).astype(BF16)

    x = _token_tile(tok_refs)
    if conv:
        a = _gated_conv(x, xp_ref, xn_ref, sha_ref, sca_ref, win_ref, wc_ref)
    else:
        a = a_ref[...]
    y = jnp.dot(a, w_ref[...], preferred_element_type=F32)
    x1 = _layer_norm(ALPHA * x + _mod_row(ga_ref) * y, lg_ref[...], lb_ref[...])
    x1_ref[...] = x1
    h2 = x1 * (1.0 + _mod_row(scf_ref)) + _mod_row(shf_ref)
    _route(h2, rw_ref, rb_ref, tri_ref, carry_ref, xe_ref, meta_ref, cnt_ref)


def _post(mix, w_out, tokens, mods, layer, ln_g, ln_b, rw, rb, conv):
    row = lambda i: (i, 0)
    fix2 = lambda i: (0, 0)
    tile = pl.BlockSpec((TM, D_MODEL), row)
    tok_args, tok_specs = _token_operands(tokens)
    if conv:
        assert len(tok_args) == 1
        w_in, wc = mix
        sub = TM // 8
        mix_args = (tok_args[0], tok_args[0], mods, mods, w_in, wc)
        mix_specs = [pl.BlockSpec((8, D_MODEL), lambda i: (jnp.maximum(i * sub - 1, 0), 0)),
                     pl.BlockSpec((8, D_MODEL), lambda i: (jnp.minimum((i + 1) * sub, T_TOK // 8 - 1), 0)),
                     _mod_spec(layer, 0), _mod_spec(layer, 1),
                     pl.BlockSpec(w_in.shape, fix2), pl.BlockSpec((8, D_MODEL), fix2)]
    else:
        mix_args = (mix,)
        mix_specs = [tile]
    vec = pl.BlockSpec((1, D_MODEL), fix2)
    return pl.pallas_call(
        functools.partial(_post_kernel, conv=conv, n_tok=len(tok_args)),
        grid=(N_TILES,),
        in_specs=tok_specs + mix_specs + [
            pl.BlockSpec((D_MODEL, D_MODEL), fix2),
            _mod_spec(layer, 2), _mod_spec(layer, 3), _mod_spec(layer, 4),
            vec, vec,
            pl.BlockSpec((D_MODEL, LANES), fix2), pl.BlockSpec((N_EXPERTS, TM), fix2),
        ],
        out_specs=[tile, pl.BlockSpec((TM, XE_W), row), pl.BlockSpec((None, 8, TM), lambda i: (i, 0, 0)),
                   pl.BlockSpec((CLASS_ROWS, LANES), fix2)],
        out_shape=[jax.ShapeDtypeStruct((T_TOK, D_MODEL), F32),
                   jax.ShapeDtypeStruct((T_TOK, XE_W), jnp.uint32),
                   jax.ShapeDtypeStruct((N_TILES, 8, TM), jnp.int32),
                   jax.ShapeDtypeStruct((CLASS_ROWS, LANES), F32)],
        scratch_shapes=[pltpu.VMEM((TM, TM), BF16), pltpu.VMEM((CLASS_ROWS, LANES), F32)],
        compiler_params=_cparams("arbitrary"),
        name="post_conv" if conv else "post",
    )(*tok_args, *mix_args, w_out, mods, mods, mods, ln_g, ln_b, rw, rb)


def _scatter_kernel(pstart_ref, cnt_ref, dest_ref, xe_ref, xs_ref, zero_ref, sem):
    def row_copy(src, d):
        return pltpu.make_async_copy(src, xs_ref.at[pl.ds(d, 1)], sem)

    first_row = pl.program_id(0) * SCAT_ROWS

    def issue(jo, carry):
        base = pl.multiple_of(jo * 8, 8)
        group = xe_ref.at[pl.ds(base, 8)]
        for k in range(8):
            row_copy(group.at[pl.ds(k, 1)], dest_ref[first_row + base + k]).start(priority=k % 2)
        return carry

    lax.fori_loop(0, SCAT_ROWS // 8, issue, 0)

    def drain(j, carry):
        row_copy(xe_ref.at[pl.ds(0, 1)], 0).wait()
        return carry

    lax.fori_loop(0, SCAT_ROWS, drain, 0, unroll=8)

    @pl.when(pl.program_id(0) == pl.num_programs(0) - 1)
    def _():
        zero_ref[...] = jnp.zeros_like(zero_ref)

        def per_class(c, carry):
            n = cnt_ref[c]
            first = pstart_ref[c] + n
            last = pstart_ref[c] + jnp.bitwise_and(n + (MOE_BLOCK - 1), -MOE_BLOCK)
            aligned = jnp.minimum(jnp.bitwise_and(first + 7, -8), last)

            def group_copy(k):
                row = pl.multiple_of(aligned + 8 * k, 8)
                return pltpu.make_async_copy(zero_ref.at[pl.ds(0, 8)], xs_ref.at[pl.ds(row, 8)], sem)

            def fill(k, cc):
                row_copy(zero_ref.at[pl.ds(0, 1)], first + k).start()
                return cc

            def fill_wait(k, cc):
                row_copy(zero_ref.at[pl.ds(0, 1)], 0).wait()
                return cc

            def fill_group(k, cc):
                group_copy(k).start()
                return cc

            def fill_group_wait(k, cc):
                group_copy(k).wait()
                return cc

            n_groups = lax.shift_right_logical(last - aligned, 3)
            lax.fori_loop(0, aligned - first, fill, 0)
            lax.fori_loop(0, n_groups, fill_group, 0)
            lax.fori_loop(0, aligned - first, fill_wait, 0)
            lax.fori_loop(0, n_groups, fill_group_wait, 0)
            return carry

        lax.fori_loop(0, N_CLASSES, per_class, 0)

        last = N_CLASSES - 1
        rows_used = pstart_ref[last] + jnp.bitwise_and(cnt_ref[last] + (MOE_BLOCK - 1), -MOE_BLOCK)
        first_free = lax.shift_right_logical(rows_used, MOE_BLOCK.bit_length() - 1)

        def block_copy(blk):
            return pltpu.make_async_copy(zero_ref, xs_ref.at[pl.ds(blk * MOE_BLOCK, MOE_BLOCK)], sem)

        def fill_block(blk, cc):
            block_copy(blk).start()
            return cc

        def fill_block_wait(blk, cc):
            block_copy(0).wait()
            return cc

        lax.fori_loop(first_free, N_MOE_BLOCKS, fill_block, 0)
        lax.fori_loop(first_free, N_MOE_BLOCKS, fill_block_wait, 0)


def _scatter_rows(pstart, cnt, dest, xe):
    grid_spec = pltpu.PrefetchScalarGridSpec(
        num_scalar_prefetch=2,
        grid=(T_TOK // SCAT_ROWS,),
        in_specs=[pl.BlockSpec((T_TOK,), lambda i, ps, cn: (0,), memory_space=pltpu.SMEM),
                  pl.BlockSpec((SCAT_ROWS, XE_W), lambda i, ps, cn: (i, 0))],
        out_specs=pl.BlockSpec(memory_space=pl.ANY),
        scratch_shapes=[pltpu.VMEM((MOE_BLOCK, XE_W), xe.dtype), pltpu.SemaphoreType.DMA(())],
    )
    return pl.pallas_call(
        _scatter_kernel,
        grid_spec=grid_spec,
        out_shape=jax.ShapeDtypeStruct((P_ROWS, XE_W), xe.dtype),
        compiler_params=_cparams("arbitrary"),
        name="scatter_rows",
    )(pstart, cnt, dest.reshape(T_TOK), xe)


def _expert_kernel(lo_ref, hi_ref, need_ref, nused_ref, x_ref, w1_hbm, w3_hbm, w2_hbm, o_ref,
                   st1, st3, st2, w1s, w3s, w2s, sems, done_ref, *, layer):
    b = pl.program_id(0)

    def fetch(k, buf):
        return (pltpu.make_async_copy(w1_hbm.at[layer, k], st1.at[buf], sems.at[buf]),
                pltpu.make_async_copy(w3_hbm.at[layer, k], st3.at[buf], sems.at[buf]),
                pltpu.make_async_copy(w2_hbm.at[layer, k], st2.at[buf], sems.at[buf]))

    @pl.when(b == 0)
    def _():
        done_ref[0] = 0
        for k in range(N_STAGING):
            for cp in fetch(k, k):
                cp.start()

    used = b < nused_ref[0]

    @pl.when(jnp.logical_not(used))
    def _():
        o_ref[...] = jnp.zeros_like(o_ref)

    @pl.when(used)
    def _():
        def install(k, carry):
            buf = k % N_STAGING
            slot = k % EXPERTS_PER_GROUP
            for cp in fetch(k, buf):
                cp.wait()
            w1s[slot] = st1[buf].astype(BF16)
            w3s[slot] = st3[buf].astype(BF16)
            w2s[slot] = st2[buf].astype(BF16)

            @pl.when(k + N_STAGING < N_EXPERTS)
            def _():
                for cp in fetch(k + N_STAGING, buf):
                    cp.start()

            return carry

        done = done_ref[0]
        lax.fori_loop(done, need_ref[b], install, 0)
        done_ref[0] = jnp.maximum(done, need_ref[b])

        half = D_MODEL // 2
        words = x_ref[:, :half]
        as_f32 = lambda bits: lax.bitcast_convert_type(bits, F32)
        xb = jnp.concatenate([as_f32(words & jnp.uint32(0xFFFF0000)),
                              as_f32(lax.shift_left(words, jnp.uint32(16)))], axis=1).astype(BF16)
        gates = as_f32(x_ref[:, half:])

        def expert(slot):
            h1 = jnp.dot(xb, w1s[slot], preferred_element_type=F32)
            h3 = jnp.dot(xb, w3s[slot], preferred_element_type=F32)
            act = (h1 * _sigmoid(h1) * h3).astype(BF16)
            return jnp.dot(act, w2s[slot], preferred_element_type=F32)

        ya = expert(lo_ref[b])
        yb = expert(hi_ref[b])
        o_ref[...] = gates[:, 0:1] * ya + gates[:, LANES // 2:LANES // 2 + 1] * yb

    @pl.when(b == pl.num_programs(0) - 1)
    def _():
        def drain(k, carry):
            for cp in fetch(k, k % N_STAGING):
                cp.wait()
            return carry

        done = done_ref[0]
        lax.fori_loop(done, jnp.minimum(done + N_STAGING, N_EXPERTS), drain, 0)


def _experts(lo_slot, hi_slot, need, nused, xs, w1, w3, w2, layer):
    rows = lambda b, lo, hi, nd, nu: (jnp.maximum(jnp.minimum(b, nu[0] - 1), 0), 0)
    any_spec = pl.BlockSpec(memory_space=pl.ANY)
    up, down = (D_MODEL, D_EXPERT), (D_EXPERT, D_MODEL)
    grid_spec = pltpu.PrefetchScalarGridSpec(
        num_scalar_prefetch=4,
        grid=(N_MOE_BLOCKS,),
        in_specs=[pl.BlockSpec((MOE_BLOCK, XE_W), rows), any_spec, any_spec, any_spec],
        out_specs=pl.BlockSpec((MOE_BLOCK, D_MODEL), lambda b, lo, hi, nd, nu: (b, 0)),
        scratch_shapes=[pltpu.VMEM((N_STAGING,) + up, F32), pltpu.VMEM((N_STAGING,) + up, F32),
                        pltpu.VMEM((N_STAGING,) + down, F32),
                        pltpu.VMEM((EXPERTS_PER_GROUP,) + up, BF16), pltpu.VMEM((EXPERTS_PER_GROUP,) + up, BF16),
                        pltpu.VMEM((EXPERTS_PER_GROUP,) + down, BF16),
                        pltpu.SemaphoreType.DMA((N_STAGING,)), pltpu.SMEM((1,), jnp.int32)],
    )
    return pl.pallas_call(
        functools.partial(_expert_kernel, layer=layer),
        grid_spec=grid_spec,
        out_shape=jax.ShapeDtypeStruct((P_ROWS, D_MODEL), F32),
        compiler_params=_cparams("arbitrary"),
        name="expert_pairs",
    )(lo_slot, hi_slot, need, nused, xs, w1, w3, w2)


def _ln2_kernel(dest_ref, dest_next_ref, ys_ref, x_ref, gf_ref, lg_ref, lb_ref, o_ref, ybuf, sems):
    i = pl.program_id(0)
    slot = i % 2

    def row_copy(src_row, dst_rows, dst_slot):
        return pltpu.make_async_copy(ys_ref.at[pl.ds(src_row, 1)], dst_rows, sems.at[dst_slot])

    def issue_tile(idx_ref, dst_slot):
        def issue(jo, carry):
            base = pl.multiple_of(jo * 8, 8)
            group = ybuf.at[dst_slot, pl.ds(base, 8)]
            for k in range(8):
                row_copy(idx_ref[base + k], group.at[pl.ds(k, 1)], dst_slot).start(priority=k % 2)
            return carry

        lax.fori_loop(0, TM // 8, issue, 0)

    @pl.when(i == 0)
    def _():
        issue_tile(dest_ref, 0)

    @pl.when(i + 1 < pl.num_programs(0))
    def _():
        issue_tile(dest_next_ref, 1 - slot)

    def drain(j, carry):
        row_copy(0, ybuf.at[slot, pl.ds(0, 1)], slot).wait()
        return carry

    lax.fori_loop(0, TM, drain, 0, unroll=8)
    o_ref[...] = _layer_norm(ALPHA * x_ref[...] + _mod_row(gf_ref) * ybuf[slot], lg_ref[...], lb_ref[...])


def _ln2(x1, ys, dest, mods, layer, ln_g, ln_b, latent_only):
    row = lambda i: (i, 0)
    tile = pl.BlockSpec((TM, D_MODEL), row)
    vec = pl.BlockSpec((1, D_MODEL), lambda i: (0, 0))
    idx = lambda f: pl.BlockSpec((TM,), f, memory_space=pltpu.SMEM)
    if latent_only:
        out_rows = SEQ
        out_spec = pl.BlockSpec((TM, D_MODEL), lambda i: (jnp.maximum(i - CTX_TILES, 0), 0))
    else:
        out_rows = T_TOK
        out_spec = tile
    return pl.pallas_call(
        _ln2_kernel,
        grid=(N_TILES,),
        in_specs=[idx(lambda i: (i,)), idx(lambda i: (jnp.minimum(i + 1, N_TILES - 1),)),
                  pl.BlockSpec(memory_space=pl.ANY), tile, _mod_spec(layer, 5), vec, vec],
        out_specs=out_spec,
        out_shape=jax.ShapeDtypeStruct((out_rows, D_MODEL), F32),
        scratch_shapes=[pltpu.VMEM((2, TM, D_MODEL), F32), pltpu.SemaphoreType.DMA((2,))],
        compiler_params=_cparams("arbitrary"),
        name="moe_gather_residual_ln",
    )(dest.reshape(T_TOK), dest.reshape(T_TOK), ys, x1, mods, ln_g, ln_b)


def _pair_tables():
    pairs = [(a, b) for a in range(EXPERTS_PER_GROUP) for b in range(a + 1, EXPERTS_PER_GROUP)]
    return jnp.array([p[0] for p in pairs], jnp.int32), jnp.array([p[1] for p in pairs], jnp.int32)


def _dispatch_plan(meta, counts):
    cnt = counts[:N_CLASSES, 0].astype(jnp.int32)
    padded = (cnt + MOE_BLOCK - 1) // MOE_BLOCK * MOE_BLOCK
    pad_end = jnp.cumsum(padded)
    pad_start = pad_end - padded
    nused = pad_end[-1:] // MOE_BLOCK
    cls, rank = meta[:, 0:1, :], meta[:, 1:2, :]
    before = (cls[..., None] > jnp.arange(N_CLASSES, dtype=jnp.int32)).astype(jnp.int32)
    dest = rank + jnp.sum(before * padded, axis=-1)
    blk_row = jnp.arange(N_MOE_BLOCKS, dtype=jnp.int32) * MOE_BLOCK
    blk_cls = jnp.minimum(jnp.sum((pad_end[None, :] <= blk_row[:, None]).astype(jnp.int32), axis=1),
                          N_CLASSES - 1)
    pair_lo, pair_hi = _pair_tables()
    lo_slot = pair_lo[blk_cls % PAIRS_PER_GROUP]
    hi_slot = pair_hi[blk_cls % PAIRS_PER_GROUP]
    need = (blk_cls // PAIRS_PER_GROUP) * EXPERTS_PER_GROUP + hi_slot + 1
    return pad_start, cnt, dest, lo_slot, hi_slot, need, nused


def _rope_tables():
    n_rows = SEQ // GRID_W
    freqs = jnp.power(ROPE_BASE, -jnp.arange(ROPE_FREQS, dtype=F32) / ROPE_FREQS)
    ar = jnp.arange(n_rows).astype(F32)[:, None] * freqs[None, :]
    ac = jnp.arange(GRID_W).astype(F32)[:, None] * freqs[None, :]

    def table(fn):
        by_row = jnp.broadcast_to(fn(ar)[:, None, :], (n_rows, GRID_W, ROPE_FREQS)).reshape(SEQ, ROPE_FREQS)
        by_col = jnp.broadcast_to(fn(ac)[None, :, :], (n_rows, GRID_W, ROPE_FREQS)).reshape(SEQ, ROPE_FREQS)
        return jnp.concatenate([by_row, by_row, by_col, by_col], axis=-1)

    cos, sin = table(jnp.cos), table(jnp.sin)
    cos = jnp.concatenate([jnp.ones((CTX_LEN, HEAD_DIM), F32), cos], axis=0)
    sin = jnp.concatenate([jnp.zeros((CTX_LEN, HEAD_DIM), F32), sin], axis=0)
    first_half = (jnp.arange(HEAD_DIM) % (2 * ROPE_FREQS)) < ROPE_FREQS
    sa = jnp.where(first_half[None, :], -sin, 0.0)
    sb = jnp.where(first_half[None, :], 0.0, sin)
    two = lambda t: jnp.concatenate([t, t], axis=-1)
    return two(cos), two(sa), two(sb)


def kernel(x, c, ctx, c_ctx, w_mod, b_mod, ln1_g, ln1_b, ln2_g, ln2_b, router_w, router_bias, moe_w1, moe_w3, moe_w2, a_w_qkv, a_w_o, a_sink, b_w_in, b_b_in, b_ln_g, b_ln_b, b_w_s, b_b_s, b_w_out, c_w_in, c_w_conv, c_w_out):
    assert x.shape == (1, SEQ, D_MODEL) and ctx.shape == (1, CTX_LEN, D_MODEL)
    tok = (ctx[0], x[0])
    cc = jnp.zeros((8, D_MODEL), F32).at[0].set(c[0]).at[1].set(c_ctx)
    mods = _modulation(cc, w_mod, b_mod)
    cos, sa, sb = _rope_tables()
    rw = jnp.pad(router_w, ((0, 0), (0, LANES - N_EXPERTS))).astype(BF16)
    rb = jnp.broadcast_to(router_bias.astype(F32)[:, None], (N_EXPERTS, TM))

    for i in range(DEPTH):
        kind, j = i % N_MIXERS, i // N_MIXERS
        if kind == 0:
            q, kb, vb = _qkv_project(tok, mods, i, a_w_qkv[j].astype(BF16), cos, sa, sb)
            mix = _attention(a_sink[j], q, kb, vb)
            w_out = a_w_o[j]
        elif kind == 1:
            bs = jnp.repeat(b_b_s[j], D_MODEL // GMLP_GROUPS, axis=1)
            mix = _gmlp_mixer(tok, mods, i, b_w_in[j].astype(BF16), b_b_in[j].reshape(1, -1),
                              b_ln_g[j].reshape(1, -1), b_ln_b[j].reshape(1, -1), b_w_s[j].astype(BF16), bs)
            w_out = b_w_out[j]
        else:
            mix = (c_w_in[j].astype(BF16), jnp.pad(c_w_conv[j], ((0, 5), (0, 0))))
            w_out = c_w_out[j]
        x1, xe, meta, counts = _post(mix, w_out.astype(BF16), tok, mods, i,
                                     ln1_g[i].reshape(1, -1), ln1_b[i].reshape(1, -1), rw, rb, conv=(kind == 2))
        pad_start, cnt, dest, lo_slot, hi_slot, need, nused = _dispatch_plan(meta, counts)
        xs = _scatter_rows(pad_start, cnt, dest, xe)
        ys = _experts(lo_slot, hi_slot, need, nused, xs, moe_w1, moe_w3, moe_w2, i)
        tok = _ln2(x1, ys, dest, mods, i, ln2_g[i].reshape(1, -1), ln2_b[i].reshape(1, -1),
                   latent_only=(i == DEPTH - 1))
    return tok.reshape(1, SEQ, D_MODEL)
```

```python
import functools

import jax
import jax.numpy as jnp
from jax import lax
from jax.experimental import pallas as pl
from jax.experimental.pallas import tpu as pltpu

F32 = jnp.float32
BF16 = jnp.bfloat16

D_MODEL = 1024
SEQ = 16384
DEPTH = 4
GRID_W = 64
CTX_LEN = 256
N_MIXERS = 3
N_HEADS = 16
N_KV_HEADS = 4
HEAD_DIM = 64
ATTN_BLOCK = 128
ATTN_SCALE = HEAD_DIM ** -0.5
ROPE_BASE = 10000.0
ROPE_FREQS = HEAD_DIM // 4
CHUNK = 128
GMLP_GROUPS = 8
N_EXPERTS = 16
N_EXPERT_GROUPS = 4
EXPERTS_PER_GROUP = 4
D_EXPERT = 512
ALPHA = (2 * DEPTH) ** 0.25
LN_EPS = 1e-5

LANES = 128
T_TOK = CTX_LEN + SEQ
TM = 256
N_TILES = T_TOK // TM
CTX_TILES = CTX_LEN // TM
N_ABLK = T_TOK // ATTN_BLOCK
CTX_ABLK = CTX_LEN // ATTN_BLOCK
PAIRS_PER_GROUP = 6
N_CLASSES = N_EXPERT_GROUPS * PAIRS_PER_GROUP
CLASS_ROWS = 32
MOE_BLOCK = 256
N_STAGING = 2
N_MOE_BLOCKS = T_TOK // MOE_BLOCK + N_CLASSES
P_ROWS = N_MOE_BLOCKS * MOE_BLOCK
XE_W = D_MODEL // 2 + LANES
SCAT_ROWS = 5 * TM
NEG_BIG = -1e30
VMEM_LIMIT = 52 * 1024 * 1024


def _cparams(sem="arbitrary"):
    return pltpu.CompilerParams(dimension_semantics=(sem,), vmem_limit_bytes=VMEM_LIMIT)


def _mod_row(ref):
    is_ctx = pl.program_id(0) < CTX_TILES
    return jnp.where(is_ctx, ref[1:2, :], ref[0:1, :])


def _layer_norm(x, g, b):
    mu = jnp.mean(x, axis=-1, keepdims=True)
    xc = x - mu
    var = jnp.mean(xc * xc, axis=-1, keepdims=True)
    return xc * lax.rsqrt(var + LN_EPS) * g + b


def _sigmoid(x):
    return 1.0 / (1.0 + jnp.exp(-x))


def _mod_kernel(cc_ref, w_ref, b_ref, o_ref):
    cc = cc_ref[...]
    act = cc * _sigmoid(cc)
    o_ref[...] = jnp.dot(act.astype(BF16), w_ref[...].astype(BF16), preferred_element_type=F32) + b_ref[...]


def _modulation(cc, w_mod, b_mod):
    nt = 1536
    return pl.pallas_call(
        _mod_kernel,
        grid=(DEPTH, 6 * D_MODEL // nt),
        in_specs=[
            pl.BlockSpec((8, D_MODEL), lambda l, n: (0, 0)),
            pl.BlockSpec((None, D_MODEL, nt), lambda l, n: (l, 0, n)),
            pl.BlockSpec((None, 1, nt), lambda l, n: (l, 0, n)),
        ],
        out_specs=pl.BlockSpec((None, 8, nt), lambda l, n: (l, 0, n)),
        out_shape=jax.ShapeDtypeStruct((DEPTH, 8, 6 * D_MODEL), F32),
        compiler_params=pltpu.CompilerParams(
            dimension_semantics=("arbitrary", "arbitrary"), vmem_limit_bytes=VMEM_LIMIT),
        name="modulation",
    )(cc, w_mod, b_mod.reshape(DEPTH, 1, 6 * D_MODEL))


def _mod_spec(layer, chunk):
    return pl.BlockSpec((None, 8, D_MODEL), lambda i: (layer, 0, chunk))


def _store_split_heads(ref, tile, pair):
    low = lax.broadcasted_iota(jnp.int32, tile.shape, 1) < HEAD_DIM
    swapped = pltpu.roll(tile, HEAD_DIM, 1)
    parts = ((jnp.where(low, tile, 0.0), jnp.where(low, 0.0, swapped)),
             (jnp.where(low, swapped, 0.0), jnp.where(low, 0.0, tile)))
    blk = ATTN_BLOCK
    for h, (in_low, in_high) in enumerate(parts):
        cols = slice((2 * pair + h) * LANES, (2 * pair + h + 1) * LANES)
        for n in range(TM // blk):
            rows = slice(n * blk, (n + 1) * blk)
            ref[2 * n * blk:(2 * n + 1) * blk, cols] = in_low[rows].astype(BF16)
            ref[(2 * n + 1) * blk:(2 * n + 2) * blk, cols] = in_high[rows].astype(BF16)


def _token_operands(tokens):
    if isinstance(tokens, tuple):
        specs = [pl.BlockSpec((TM, D_MODEL), lambda i: (jnp.minimum(i, CTX_TILES - 1), 0)),
                 pl.BlockSpec((TM, D_MODEL), lambda i: (jnp.maximum(i - CTX_TILES, 0), 0))]
        return list(tokens), specs
    return [tokens], [pl.BlockSpec((TM, D_MODEL), lambda i: (i, 0))]


def _token_tile(tok_refs):
    if len(tok_refs) == 2:
        return jnp.where(pl.program_id(0) < CTX_TILES, tok_refs[0][...], tok_refs[1][...])
    return tok_refs[0][...]


def _qkv_kernel(*refs, n_tok):
    tok_refs = refs[:n_tok]
    sh_ref, sc_ref, w_ref, cos_ref, sa_ref, sb_ref, q_ref, k_ref, v_ref = refs[n_tok:]
    h = (_token_tile(tok_refs) * (1.0 + _mod_row(sc_ref)) + _mod_row(sh_ref)).astype(BF16)
    y = jnp.dot(h, w_ref[...], preferred_element_type=F32)
    cos, sa, sb = cos_ref[...], sa_ref[...], sb_ref[...]
    n_q = D_MODEL // LANES
    n_k = N_KV_HEADS * HEAD_DIM // LANES
    for t in range(n_q + n_k):
        yt = y[:, t * LANES:(t + 1) * LANES]
        r = yt * cos + pltpu.roll(yt, LANES - ROPE_FREQS, 1) * sa + pltpu.roll(yt, ROPE_FREQS, 1) * sb
        if t < n_q:
            q_ref[:, t * LANES:(t + 1) * LANES] = (r * ATTN_SCALE).astype(BF16)
        else:
            _store_split_heads(k_ref, r, t - n_q)
    for t in range(n_k):
        _store_split_heads(v_ref, y[:, (n_q + n_k + t) * LANES:(n_q + n_k + t + 1) * LANES], t)


def _qkv_project(tokens, mods, layer, w, cos, sa, sb):
    kvw = N_KV_HEADS * LANES
    row = lambda i: (i, 0)
    tok_args, tok_specs = _token_operands(tokens)
    return pl.pallas_call(
        functools.partial(_qkv_kernel, n_tok=len(tok_args)),
        grid=(N_TILES,),
        in_specs=tok_specs + [
            _mod_spec(layer, 0), _mod_spec(layer, 1),
            pl.BlockSpec(w.shape, lambda i: (0, 0)),
            pl.BlockSpec((TM, LANES), row), pl.BlockSpec((TM, LANES), row), pl.BlockSpec((TM, LANES), row),
        ],
        out_specs=[pl.BlockSpec((TM, D_MODEL), row), pl.BlockSpec((2 * TM, kvw), row),
                   pl.BlockSpec((2 * TM, kvw), row)],
        out_shape=[jax.ShapeDtypeStruct((T_TOK, D_MODEL), BF16),
                   jax.ShapeDtypeStruct((2 * T_TOK, kvw), BF16),
                   jax.ShapeDtypeStruct((2 * T_TOK, kvw), BF16)],
        compiler_params=_cparams("parallel"),
        name="qkv_rope",
    )(*tok_args, mods, mods, w, cos, sa, sb)


def _attn_kernel(sink_ref, q_ref, kp_ref, km_ref, kn_ref, kx_ref, vp_ref, vm_ref, vn_ref, vx_ref, o_ref):
    blk = ATTN_BLOCK
    r = lax.broadcasted_iota(jnp.int32, (2 * blk, 2 * blk), 0) & (blk - 1)
    c = lax.broadcasted_iota(jnp.int32, (2 * blk, 2 * blk), 1) & (blk - 1)
    top_rows = lax.broadcasted_iota(jnp.int32, (2 * blk, 1), 0) < blk
    even_cols = lax.broadcasted_iota(jnp.int32, (2 * blk, 2 * blk), 1) < blk
    even_lanes = lax.broadcasted_iota(jnp.int32, (2 * blk, LANES), 1) < HEAD_DIM
    nt_dims = (((1,), (1,)), ((), ()))
    t_rows = lax.broadcasted_iota(jnp.int32, (2 * CTX_LEN, LANES), 0)
    t_cols = lax.broadcasted_iota(jnp.int32, (2 * CTX_LEN, LANES), 1)
    tally = jnp.where(t_cols == ((t_rows >> (blk.bit_length() - 1)) & 1), 1.0, 0.0).astype(BF16)

    def halves(s):
        return s[:, :blk], s[:, blk:]

    windows = ((kp_ref, vp_ref, 0), (km_ref, vm_ref, 0), (km_ref, vm_ref, 2 * blk), (kn_ref, vn_ref, 0))
    for sub, j in [(sub, j) for sub in range(2) for j in range(N_KV_HEADS)]:
        i = 2 * pl.program_id(0) + sub
        prev_ok = i >= CTX_ABLK + 1
        cur_ok = i >= CTX_ABLK
        next_ok = jnp.logical_and(i >= CTX_ABLK, i <= N_ABLK - 2)
        m_prev = jnp.logical_and(c >= r, prev_ok)
        m_cur = jnp.logical_and(c >= 0, cur_ok)
        m_next = jnp.logical_and(c <= r, next_ok)
        q_rows = slice(sub * blk, (sub + 1) * blk)
        o_rows = q_rows
        (kp, vp, op), (kc, vc, oc), (kn, vn, on) = windows[sub:sub + 3]
        ks = slice(j * LANES, (j + 1) * LANES)
        k_blk = lambda ref, off: ref[off:off + 2 * blk, ks]
        lhs = jnp.concatenate([q_ref[q_rows, 2 * j * LANES:(2 * j + 1) * LANES],
                               q_ref[q_rows, (2 * j + 1) * LANES:(2 * j + 2) * LANES]], axis=0)
        s_p = jnp.where(m_prev, lax.dot_general(lhs, k_blk(kp, op), nt_dims, preferred_element_type=F32), NEG_BIG)
        s_c = jnp.where(m_cur, lax.dot_general(lhs, k_blk(kc, oc), nt_dims, preferred_element_type=F32), NEG_BIG)
        s_n = jnp.where(m_next, lax.dot_general(lhs, k_blk(kn, on), nt_dims, preferred_element_type=F32), NEG_BIG)
        s_x = lax.dot_general(lhs, kx_ref[:, ks], nt_dims, preferred_element_type=F32)
        s_x0, s_x1 = s_x[:, :2 * blk], s_x[:, 2 * blk:]
        pieces = (s_p, s_c, s_n, s_x0, s_x1)
        sink_e = jnp.where(top_rows, sink_ref[4 * j], sink_ref[4 * j + 2])
        sink_o = jnp.where(top_rows, sink_ref[4 * j + 1], sink_ref[4 * j + 3])
        tile_max = functools.reduce(jnp.maximum, pieces)
        tm_e, tm_o = halves(tile_max)
        m_e = jnp.maximum(jnp.max(tm_e, axis=1, keepdims=True), sink_e)
        m_o = jnp.maximum(jnp.max(tm_o, axis=1, keepdims=True), sink_o)
        m = jnp.where(even_cols, m_e, m_o)
        p_p, p_c, p_n, p_x0, p_x1 = [jnp.exp(s - m).astype(BF16) for s in pieces]
        p_x = jnp.concatenate([p_x0, p_x1], axis=1)
        with_tally = lambda v: jnp.concatenate([v, tally[:v.shape[0]]], axis=1)
        o = (jnp.dot(p_p, with_tally(k_blk(vp, op)), preferred_element_type=F32)
             + jnp.dot(p_c, with_tally(k_blk(vc, oc)), preferred_element_type=F32)
             + jnp.dot(p_n, with_tally(k_blk(vn, on)), preferred_element_type=F32)
             + jnp.dot(p_x, with_tally(vx_ref[:, ks]), preferred_element_type=F32))
        l_e = o[:, LANES:LANES + 1] + jnp.exp(sink_e - m_e)
        l_o = o[:, LANES + 1:LANES + 2] + jnp.exp(sink_o - m_o)
        o = (o[:, :LANES] / jnp.where(even_lanes, l_e, l_o)).astype(BF16)
        o_ref[o_rows, 2 * j * LANES:(2 * j + 1) * LANES] = o[:blk]
        o_ref[o_rows, (2 * j + 1) * LANES:(2 * j + 2) * LANES] = o[blk:]


def _attention(sink, q, kb, vb):
    kvw = N_KV_HEADS * LANES
    blk = ATTN_BLOCK
    lo, hi = CTX_ABLK, N_ABLK - 1
    prev_map = lambda s: (jnp.clip(2 * s - 1, lo, hi), 0)
    pair_map = lambda s: (s, 0)
    next_map = lambda s: (jnp.clip(2 * s + 2, lo, hi), 0)
    ctx_map = lambda s: (0, 0)
    kv_specs = [pl.BlockSpec((2 * blk, kvw), prev_map), pl.BlockSpec((4 * blk, kvw), pair_map),
                pl.BlockSpec((2 * blk, kvw), next_map), pl.BlockSpec((2 * CTX_LEN, kvw), ctx_map)]
    return pl.pallas_call(
        _attn_kernel,
        grid=(N_ABLK // 2,),
        in_specs=[pl.BlockSpec(memory_space=pltpu.SMEM), pl.BlockSpec((2 * blk, D_MODEL), pair_map)]
        + kv_specs + kv_specs,
        out_specs=pl.BlockSpec((2 * blk, D_MODEL), pair_map),
        out_shape=jax.ShapeDtypeStruct((T_TOK, D_MODEL), BF16),
        compiler_params=_cparams("parallel"),
        name="window_attention",
    )(sink, q, kb, kb, kb, kb, vb, vb, vb, vb)


def _gmlp_kernel(x_ref, sh_ref, sc_ref, w_ref, b_ref, g_ref, be_ref, ws_ref, bs_ref, o_ref):
    h = (x_ref[...] * (1.0 + _mod_row(sc_ref)) + _mod_row(sh_ref)).astype(BF16)
    z = jax.nn.gelu(jnp.dot(h, w_ref[...], preferred_element_type=F32) + b_ref[...], approximate=True)
    u = z[:, :D_MODEL]
    v = _layer_norm(z[:, D_MODEL:], g_ref[...], be_ref[...]).astype(BF16)
    gd = D_MODEL // GMLP_GROUPS
    for n in range(TM // CHUNK):
        rows = slice(n * CHUNK, (n + 1) * CHUNK)
        for g in range(GMLP_GROUPS):
            cols = slice(g * gd, (g + 1) * gd)
            s = jnp.dot(ws_ref[g], v[rows, cols], preferred_element_type=F32) + bs_ref[:, cols]
            o_ref[rows, cols] = (u[rows, cols] * s).astype(BF16)


def _gmlp_mixer(x, mods, layer, w_in, b_in, ln_g, ln_b, w_s, b_s):
    row = lambda i: (i, 0)
    fix2 = lambda i: (0, 0)
    return pl.pallas_call(
        _gmlp_kernel,
        grid=(N_TILES,),
        in_specs=[
            pl.BlockSpec((TM, D_MODEL), row),
            _mod_spec(layer, 0), _mod_spec(layer, 1),
            pl.BlockSpec((D_MODEL, 2 * D_MODEL), fix2),
            pl.BlockSpec((1, 2 * D_MODEL), fix2),
            pl.BlockSpec((1, D_MODEL), fix2), pl.BlockSpec((1, D_MODEL), fix2),
            pl.BlockSpec((GMLP_GROUPS, CHUNK, CHUNK), lambda i: (0, 0, 0)),
            pl.BlockSpec((CHUNK, D_MODEL), fix2),
        ],
        out_specs=pl.BlockSpec((TM, D_MODEL), row),
        out_shape=jax.ShapeDtypeStruct((T_TOK, D_MODEL), BF16),
        compiler_params=_cparams("parallel"),
        name="gmlp_mixer",
    )(x, mods, mods, w_in, b_in, ln_g, ln_b, w_s, b_s)


def _route(h2, rw_ref, rb_ref, tri_ref, carry_ref, xe_ref, meta_ref, cnt_ref):
    logits = jnp.dot(h2.astype(BF16), rw_ref[...], preferred_element_type=F32)
    s = _sigmoid(jnp.transpose(logits)[:N_EXPERTS, :])
    ssel = s + rb_ref[...]
    row = lambda v, k: v[k:k + 1, :]
    npg = EXPERTS_PER_GROUP

    def first_argmax(vals):
        best, idx = vals[0], jnp.zeros_like(vals[0])
        for k in range(1, len(vals)):
            better = vals[k] > best
            best = jnp.where(better, vals[k], best)
            idx = jnp.where(better, float(k), idx)
        return best, idx

    def pick(vals, idx):
        out = vals[-1]
        for k in range(len(vals) - 2, -1, -1):
            out = jnp.where(idx == float(k), vals[k], out)
        return out

    group_scores = []
    for g in range(N_EXPERT_GROUPS):
        a, b, c, d = (row(ssel, npg * g + k) for k in range(npg))
        group_scores.append(jnp.maximum(jnp.maximum(jnp.maximum(a + b, a + c), jnp.maximum(a + d, b + c)),
                                        jnp.maximum(b + d, c + d)))
    _, gsel = first_argmax(group_scores)
    biased = [pick([row(ssel, npg * g + k) for g in range(N_EXPERT_GROUPS)], gsel) for k in range(npg)]
    plain = [pick([row(s, npg * g + k) for g in range(N_EXPERT_GROUPS)], gsel) for k in range(npg)]
    _, i1 = first_argmax(biased)
    _, i2 = first_argmax([jnp.where(i1 == float(k), -jnp.inf, biased[k]) for k in range(npg)])
    lo = jnp.minimum(i1, i2)
    hi = jnp.maximum(i1, i2)
    s_lo = pick(plain, lo)
    s_hi = pick(plain, hi)
    den = s_lo + s_hi
    g_lo = s_lo / den
    g_hi = s_hi / den
    pair = lo * (7.0 - lo) * 0.5 + hi - lo - 1.0
    cls = gsel * PAIRS_PER_GROUP + pair
    cls_rows = lax.broadcasted_iota(jnp.int32, (CLASS_ROWS, TM), 0).astype(F32)
    onehot = cls_rows == cls
    oh_f = jnp.where(onehot, 1.0, 0.0)
    before = jnp.dot(oh_f.astype(BF16), tri_ref[...], preferred_element_type=F32)
    carry = carry_ref[...]
    rank = jnp.sum(jnp.where(onehot, before + carry[:, 0:1], 0.0), axis=0, keepdims=True)
    carry = carry + jnp.sum(oh_f, axis=1, keepdims=True)
    carry_ref[...] = carry
    cnt_ref[...] = carry
    as_bits = lambda v: lax.bitcast_convert_type(v.astype(BF16).astype(F32), jnp.uint32)
    half = D_MODEL // 2
    xe_ref[:, :half] = as_bits(h2[:, :half]) | lax.shift_right_logical(as_bits(h2[:, half:]), jnp.uint32(16))
    gate_rows = lax.broadcasted_iota(jnp.int32, (LANES, TM), 0) < LANES // 2
    xe_ref[:, half:] = lax.bitcast_convert_type(jnp.transpose(jnp.where(gate_rows, g_lo, g_hi)), jnp.uint32)
    meta_rows = lax.broadcasted_iota(jnp.int32, (8, TM), 0)
    meta_ref[...] = jnp.where(meta_rows == 0, cls, jnp.where(meta_rows == 1, rank, 0.0)).astype(jnp.int32)


def _gated_conv(x, xp_ref, xn_ref, sha_ref, sca_ref, win_ref, wc_ref):
    i = pl.program_id(0)
    halo = jnp.concatenate([x, xp_ref[...], xn_ref[...]], axis=0)
    h = (halo * (1.0 + _mod_row(sca_ref)) + _mod_row(sha_ref)).astype(BF16)
    proj = jnp.dot(h, win_ref[...], preferred_element_type=F32)
    gate = proj[:TM, :D_MODEL]
    zz_all = proj[:, D_MODEL:2 * D_MODEL] * proj[:, 2 * D_MODEL:]
    zz = zz_all[:TM]
    prev_ok = i >= CTX_TILES + 1
    next_ok = jnp.logical_and(i >= CTX_TILES, i <= N_TILES - 2)
    z_before = jnp.where(prev_ok, zz_all[TM + 7:TM + 8], 0.0)
    z_after = jnp.where(next_ok, zz_all[TM + 8:TM + 9], 0.0)
    rows = lax.broadcasted_iota(jnp.int32, (TM, 1), 0)
    zm1 = jnp.where(rows == 0, z_before, pltpu.roll(zz, 1, 0))
    zp1 = jnp.where(rows == TM - 1, z_after, pltpu.roll(zz, TM - 1, 0))
    zc = wc_ref[0:1, :] * zm1 + wc_ref[1:2, :] * zz + wc_ref[2:3, :] * zp1
    return (gate * zc).astype(BF16)


def _post_kernel(*refs, conv, n_tok):
    tok_refs, refs = refs[:n_tok], refs[n_tok:]
    if conv:
        (xp_ref, xn_ref, sha_ref, sca_ref, win_ref, wc_ref, *refs) = refs
    else:
        (a_ref, *refs) = refs
    (w_ref, ga_ref, shf_ref, scf_ref, lg_ref, lb_ref, rw_ref, rb_ref,
     x1_ref, xe_ref, meta_ref, cnt_ref, tri_ref, carry_ref) = refs

    @pl.when(pl.program_id(0) == 0)
    def _():
        carry_ref[...] = jnp.zeros_like(carry_ref)
        rr = lax.broadcasted_iota(jnp.int32, (TM, TM), 0)
        cc = lax.broadcasted_iota(jnp.int32, (TM, TM), 1)
        tri_ref[...] = jnp.where(rr < cc, 1.0, 0.0).astype(BF16)

    x = _token_tile(tok_refs)
    if conv:
        a = _gated_conv(x, xp_ref, xn_ref, sha_ref, sca_ref, win_ref, wc_ref)
    else:
        a = a_ref[...]
    y = jnp.dot(a, w_ref[...], preferred_element_type=F32)
    x1 = _layer_norm(ALPHA * x + _mod_row(ga_ref) * y, lg_ref[...], lb_ref[...])
    x1_ref[...] = x1
    h2 = x1 * (1.0 + _mod_row(scf_ref)) + _mod_row(shf_ref)
    _route(h2, rw_ref, rb_ref, tri_ref, carry_ref, xe_ref, meta_ref, cnt_ref)


def _post(mix, w_out, tokens, mods, layer, ln_g, ln_b, rw, rb, conv):
    row = lambda i: (i, 0)
    fix2 = lambda i: (0, 0)
    tile = pl.BlockSpec((TM, D_MODEL), row)
    tok_args, tok_specs = _token_operands(tokens)
    if conv:
        assert len(tok_args) == 1
        w_in, wc = mix
        sub = TM // 8
        mix_args = (tok_args[0], tok_args[0], mods, mods, w_in, wc)
        mix_specs = [pl.BlockSpec((8, D_MODEL), lambda i: (jnp.maximum(i * sub - 1, 0), 0)),
                     pl.BlockSpec((8, D_MODEL), lambda i: (jnp.minimum((i + 1) * sub, T_TOK // 8 - 1), 0)),
                     _mod_spec(layer, 0), _mod_spec(layer, 1),
                     pl.BlockSpec(w_in.shape, fix2), pl.BlockSpec((8, D_MODEL), fix2)]
    else:
        mix_args = (mix,)
        mix_specs = [tile]
    vec = pl.BlockSpec((1, D_MODEL), fix2)
    return pl.pallas_call(
        functools.partial(_post_kernel, conv=conv, n_tok=len(tok_args)),
        grid=(N_TILES,),
        in_specs=tok_specs + mix_specs + [
            pl.BlockSpec((D_MODEL, D_MODEL), fix2),
            _mod_spec(layer, 2), _mod_spec(layer, 3), _mod_spec(layer, 4),
            vec, vec,
            pl.BlockSpec((D_MODEL, LANES), fix2), pl.BlockSpec((N_EXPERTS, TM), fix2),
        ],
        out_specs=[tile, pl.BlockSpec((TM, XE_W), row), pl.BlockSpec((None, 8, TM), lambda i: (i, 0, 0)),
                   pl.BlockSpec((CLASS_ROWS, LANES), fix2)],
        out_shape=[jax.ShapeDtypeStruct((T_TOK, D_MODEL), F32),
                   jax.ShapeDtypeStruct((T_TOK, XE_W), jnp.uint32),
                   jax.ShapeDtypeStruct((N_TILES, 8, TM), jnp.int32),
                   jax.ShapeDtypeStruct((CLASS_ROWS, LANES), F32)],
        scratch_shapes=[pltpu.VMEM((TM, TM), BF16), pltpu.VMEM((CLASS_ROWS, LANES), F32)],
        compiler_params=_cparams("arbitrary"),
        name="post_conv" if conv else "post",
    )(*tok_args, *mix_args, w_out, mods, mods, mods, ln_g, ln_b, rw, rb)


def _scatter_kernel(pstart_ref, cnt_ref, dest_ref, xe_ref, xs_ref, zero_ref, sem):
    def row_copy(src, d):
        return pltpu.make_async_copy(src, xs_ref.at[pl.ds(d, 1)], sem)

    first_row = pl.program_id(0) * SCAT_ROWS

    def issue(jo, carry):
        base = pl.multiple_of(jo * 8, 8)
        group = xe_ref.at[pl.ds(base, 8)]
        for k in range(8):
            row_copy(group.at[pl.ds(k, 1)], dest_ref[first_row + base + k]).start()
        return carry

    lax.fori_loop(0, SCAT_ROWS // 8, issue, 0)

    def drain(j, carry):
        row_copy(xe_ref.at[pl.ds(0, 1)], 0).wait()
        return carry

    lax.fori_loop(0, SCAT_ROWS, drain, 0, unroll=8)

    @pl.when(pl.program_id(0) == pl.num_programs(0) - 1)
    def _():
        zero_ref[...] = jnp.zeros_like(zero_ref)

        def per_class(c, carry):
            n = cnt_ref[c]
            first = pstart_ref[c] + n
            last = pstart_ref[c] + jnp.bitwise_and(n + (MOE_BLOCK - 1), -MOE_BLOCK)
            aligned = jnp.minimum(jnp.bitwise_and(first + 7, -8), last)

            def group_copy(k):
                row = pl.multiple_of(aligned + 8 * k, 8)
                return pltpu.make_async_copy(zero_ref.at[pl.ds(0, 8)], xs_ref.at[pl.ds(row, 8)], sem)

            def fill(k, cc):
                row_copy(zero_ref.at[pl.ds(0, 1)], first + k).start()
                return cc

            def fill_wait(k, cc):
                row_copy(zero_ref.at[pl.ds(0, 1)], 0).wait()
                return cc

            def fill_group(k, cc):
                group_copy(k).start()
                return cc

            def fill_group_wait(k, cc):
                group_copy(k).wait()
                return cc

            n_groups = lax.shift_right_logical(last - aligned, 3)
            lax.fori_loop(0, aligned - first, fill, 0)
            lax.fori_loop(0, n_groups, fill_group, 0)
            lax.fori_loop(0, aligned - first, fill_wait, 0)
            lax.fori_loop(0, n_groups, fill_group_wait, 0)
            return carry

        lax.fori_loop(0, N_CLASSES, per_class, 0)

        last = N_CLASSES - 1
        rows_used = pstart_ref[last] + jnp.bitwise_and(cnt_ref[last] + (MOE_BLOCK - 1), -MOE_BLOCK)
        first_free = lax.shift_right_logical(rows_used, MOE_BLOCK.bit_length() - 1)

        def block_copy(blk):
            return pltpu.make_async_copy(zero_ref, xs_ref.at[pl.ds(blk * MOE_BLOCK, MOE_BLOCK)], sem)

        def fill_block(blk, cc):
            block_copy(blk).start()
            return cc

        def fill_block_wait(blk, cc):
            block_copy(0).wait()
            return cc

        lax.fori_loop(first_free, N_MOE_BLOCKS, fill_block, 0)
        lax.fori_loop(first_free, N_MOE_BLOCKS, fill_block_wait, 0)


def _scatter_rows(pstart, cnt, dest, xe):
    grid_spec = pltpu.PrefetchScalarGridSpec(
        num_scalar_prefetch=2,
        grid=(T_TOK // SCAT_ROWS,),
        in_specs=[pl.BlockSpec((T_TOK,), lambda i, ps, cn: (0,), memory_space=pltpu.SMEM),
                  pl.BlockSpec((SCAT_ROWS, XE_W), lambda i, ps, cn: (i, 0))],
        out_specs=pl.BlockSpec(memory_space=pl.ANY),
        scratch_shapes=[pltpu.VMEM((MOE_BLOCK, XE_W), xe.dtype), pltpu.SemaphoreType.DMA(())],
    )
    return pl.pallas_call(
        _scatter_kernel,
        grid_spec=grid_spec,
        out_shape=jax.ShapeDtypeStruct((P_ROWS, XE_W), xe.dtype),
        compiler_params=_cparams("arbitrary"),
        name="scatter_rows",
    )(pstart, cnt, dest.reshape(T_TOK), xe)


def _expert_kernel(lo_ref, hi_ref, need_ref, nused_ref, x_ref, w1_hbm, w3_hbm, w2_hbm, o_ref,
                   st1, st3, st2, w1s, w3s, w2s, sems, done_ref, *, layer):
    b = pl.program_id(0)

    def fetch(k, buf):
        return (pltpu.make_async_copy(w1_hbm.at[layer, k], st1.at[buf], sems.at[buf]),
                pltpu.make_async_copy(w3_hbm.at[layer, k], st3.at[buf], sems.at[buf]),
                pltpu.make_async_copy(w2_hbm.at[layer, k], st2.at[buf], sems.at[buf]))

    @pl.when(b == 0)
    def _():
        done_ref[0] = 0
        for k in range(N_STAGING):
            for cp in fetch(k, k):
                cp.start()

    used = b < nused_ref[0]

    @pl.when(jnp.logical_not(used))
    def _():
        o_ref[...] = jnp.zeros_like(o_ref)

    @pl.when(used)
    def _():
        def install(k, carry):
            buf = k % N_STAGING
            slot = k % EXPERTS_PER_GROUP
            for cp in fetch(k, buf):
                cp.wait()
            w1s[slot] = st1[buf].astype(BF16)
            w3s[slot] = st3[buf].astype(BF16)
            w2s[slot] = st2[buf].astype(BF16)

            @pl.when(k + N_STAGING < N_EXPERTS)
            def _():
                for cp in fetch(k + N_STAGING, buf):
                    cp.start()

            return carry

        done = done_ref[0]
        lax.fori_loop(done, need_ref[b], install, 0)
        done_ref[0] = jnp.maximum(done, need_ref[b])

        half = D_MODEL // 2
        words = x_ref[:, :half]
        as_f32 = lambda bits: lax.bitcast_convert_type(bits, F32)
        xb = jnp.concatenate([as_f32(words & jnp.uint32(0xFFFF0000)),
                              as_f32(lax.shift_left(words, jnp.uint32(16)))], axis=1).astype(BF16)
        gates = as_f32(x_ref[:, half:])

        def expert(slot):
            h1 = jnp.dot(xb, w1s[slot], preferred_element_type=F32)
            h3 = jnp.dot(xb, w3s[slot], preferred_element_type=F32)
            act = (h1 * _sigmoid(h1) * h3).astype(BF16)
            return jnp.dot(act, w2s[slot], preferred_element_type=F32)

        ya = expert(lo_ref[b])
        yb = expert(hi_ref[b])
        o_ref[...] = gates[:, 0:1] * ya + gates[:, LANES // 2:LANES // 2 + 1] * yb

    @pl.when(b == pl.num_programs(0) - 1)
    def _():
        def drain(k, carry):
            for cp in fetch(k, k % N_STAGING):
                cp.wait()
            return carry

        done = done_ref[0]
        lax.fori_loop(done, jnp.minimum(done + N_STAGING, N_EXPERTS), drain, 0)


def _experts(lo_slot, hi_slot, need, nused, xs, w1, w3, w2, layer):
    rows = lambda b, lo, hi, nd, nu: (jnp.maximum(jnp.minimum(b, nu[0] - 1), 0), 0)
    any_spec = pl.BlockSpec(memory_space=pl.ANY)
    up, down = (D_MODEL, D_EXPERT), (D_EXPERT, D_MODEL)
    grid_spec = pltpu.PrefetchScalarGridSpec(
        num_scalar_prefetch=4,
        grid=(N_MOE_BLOCKS,),
        in_specs=[pl.BlockSpec((MOE_BLOCK, XE_W), rows), any_spec, any_spec, any_spec],
        out_specs=pl.BlockSpec((MOE_BLOCK, D_MODEL), lambda b, lo, hi, nd, nu: (b, 0)),
        scratch_shapes=[pltpu.VMEM((N_STAGING,) + up, F32), pltpu.VMEM((N_STAGING,) + up, F32),
                        pltpu.VMEM((N_STAGING,) + down, F32),
                        pltpu.VMEM((EXPERTS_PER_GROUP,) + up, BF16), pltpu.VMEM((EXPERTS_PER_GROUP,) + up, BF16),
                        pltpu.VMEM((EXPERTS_PER_GROUP,) + down, BF16),
                        pltpu.SemaphoreType.DMA((N_STAGING,)), pltpu.SMEM((1,), jnp.int32)],
    )
    return pl.pallas_call(
        functools.partial(_expert_kernel, layer=layer),
        grid_spec=grid_spec,
        out_shape=jax.ShapeDtypeStruct((P_ROWS, D_MODEL), F32),
        compiler_params=_cparams("arbitrary"),
        name="expert_pairs",
    )(lo_slot, hi_slot, need, nused, xs, w1, w3, w2)


def _ln2_kernel(dest_ref, dest_next_ref, ys_ref, x_ref, gf_ref, lg_ref, lb_ref, o_ref, ybuf, sems):
    i = pl.program_id(0)
    slot = i % 2

    def row_copy(src_row, dst_rows, dst_slot):
        return pltpu.make_async_copy(ys_ref.at[pl.ds(src_row, 1)], dst_rows, sems.at[dst_slot])

    def issue_tile(idx_ref, dst_slot):
        def issue(jo, carry):
            base = pl.multiple_of(jo * 8, 8)
            group = ybuf.at[dst_slot, pl.ds(base, 8)]
            for k in range(8):
                row_copy(idx_ref[base + k], group.at[pl.ds(k, 1)], dst_slot).start()
            return carry

        lax.fori_loop(0, TM // 8, issue, 0)

    @pl.when(i == 0)
    def _():
        issue_tile(dest_ref, 0)

    @pl.when(i + 1 < pl.num_programs(0))
    def _():
        issue_tile(dest_next_ref, 1 - slot)

    def drain(j, carry):
        row_copy(0, ybuf.at[slot, pl.ds(0, 1)], slot).wait()
        return carry

    lax.fori_loop(0, TM, drain, 0, unroll=8)
    o_ref[...] = _layer_norm(ALPHA * x_ref[...] + _mod_row(gf_ref) * ybuf[slot], lg_ref[...], lb_ref[...])


def _ln2(x1, ys, dest, mods, layer, ln_g, ln_b, latent_only):
    row = lambda i: (i, 0)
    tile = pl.BlockSpec((TM, D_MODEL), row)
    vec = pl.BlockSpec((1, D_MODEL), lambda i: (0, 0))
    idx = lambda f: pl.BlockSpec((TM,), f, memory_space=pltpu.SMEM)
    if latent_only:
        out_rows = SEQ
        out_spec = pl.BlockSpec((TM, D_MODEL), lambda i: (jnp.maximum(i - CTX_TILES, 0), 0))
    else:
        out_rows = T_TOK
        out_spec = tile
    return pl.pallas_call(
        _ln2_kernel,
        grid=(N_TILES,),
        in_specs=[idx(lambda i: (i,)), idx(lambda i: (jnp.minimum(i + 1, N_TILES - 1),)),
                  pl.BlockSpec(memory_space=pl.ANY), tile, _mod_spec(layer, 5), vec, vec],
        out_specs=out_spec,
        out_shape=jax.ShapeDtypeStruct((out_rows, D_MODEL), F32),
        scratch_shapes=[pltpu.VMEM((2, TM, D_MODEL), F32), pltpu.SemaphoreType.DMA((2,))],
        compiler_params=_cparams("arbitrary"),
        name="moe_gather_residual_ln",
    )(dest.reshape(T_TOK), dest.reshape(T_TOK), ys, x1, mods, ln_g, ln_b)


def _pair_tables():
    pairs = [(a, b) for a in range(EXPERTS_PER_GROUP) for b in range(a + 1, EXPERTS_PER_GROUP)]
    return jnp.array([p[0] for p in pairs], jnp.int32), jnp.array([p[1] for p in pairs], jnp.int32)


def _dispatch_plan(meta, counts):
    cnt = counts[:N_CLASSES, 0].astype(jnp.int32)
    padded = (cnt + MOE_BLOCK - 1) // MOE_BLOCK * MOE_BLOCK
    pad_end = jnp.cumsum(padded)
    pad_start = pad_end - padded
    nused = pad_end[-1:] // MOE_BLOCK
    cls, rank = meta[:, 0:1, :], meta[:, 1:2, :]
    before = (cls[..., None] > jnp.arange(N_CLASSES, dtype=jnp.int32)).astype(jnp.int32)
    dest = rank + jnp.sum(before * padded, axis=-1)
    blk_row = jnp.arange(N_MOE_BLOCKS, dtype=jnp.int32) * MOE_BLOCK
    blk_cls = jnp.minimum(jnp.sum((pad_end[None, :] <= blk_row[:, None]).astype(jnp.int32), axis=1),
                          N_CLASSES - 1)
    pair_lo, pair_hi = _pair_tables()
    lo_slot = pair_lo[blk_cls % PAIRS_PER_GROUP]
    hi_slot = pair_hi[blk_cls % PAIRS_PER_GROUP]
    need = (blk_cls // PAIRS_PER_GROUP) * EXPERTS_PER_GROUP + hi_slot + 1
    return pad_start, cnt, dest, lo_slot, hi_slot, need, nused


def _rope_tables():
    n_rows = SEQ // GRID_W
    freqs = jnp.power(ROPE_BASE, -jnp.arange(ROPE_FREQS, dtype=F32) / ROPE_FREQS)
    ar = jnp.arange(n_rows).astype(F32)[:, None] * freqs[None, :]
    ac = jnp.arange(GRID_W).astype(F32)[:, None] * freqs[None, :]

    def table(fn):
        by_row = jnp.broadcast_to(fn(ar)[:, None, :], (n_rows, GRID_W, ROPE_FREQS)).reshape(SEQ, ROPE_FREQS)
        by_col = jnp.broadcast_to(fn(ac)[None, :, :], (n_rows, GRID_W, ROPE_FREQS)).reshape(SEQ, ROPE_FREQS)
        return jnp.concatenate([by_row, by_row, by_col, by_col], axis=-1)

    cos, sin = table(jnp.cos), table(jnp.sin)
    cos = jnp.concatenate([jnp.ones((CTX_LEN, HEAD_DIM), F32), cos], axis=0)
    sin = jnp.concatenate([jnp.zeros((CTX_LEN, HEAD_DIM), F32), sin], axis=0)
    first_half = (jnp.arange(HEAD_DIM) % (2 * ROPE_FREQS)) < ROPE_FREQS
    sa = jnp.where(first_half[None, :], -sin, 0.0)
    sb = jnp.where(first_half[None, :], 0.0, sin)
    two = lambda t: jnp.concatenate([t, t], axis=-1)
    return two(cos), two(sa), two(sb)


def kernel(x, c, ctx, c_ctx, w_mod, b_mod, ln1_g, ln1_b, ln2_g, ln2_b, router_w, router_bias, moe_w1, moe_w3, moe_w2, a_w_qkv, a_w_o, a_sink, b_w_in, b_b_in, b_ln_g, b_ln_b, b_w_s, b_b_s, b_w_out, c_w_in, c_w_conv, c_w_out):
    assert x.shape == (1, SEQ, D_MODEL) and ctx.shape == (1, CTX_LEN, D_MODEL)
    tok = (ctx[0], x[0])
    cc = jnp.zeros((8, D_MODEL), F32).at[0].set(c[0]).at[1].set(c_ctx)
    mods = _modulation(cc, w_mod, b_mod)
    cos, sa, sb = _rope_tables()
    rw = jnp.pad(router_w, ((0, 0), (0, LANES - N_EXPERTS))).astype(BF16)
    rb = jnp.broadcast_to(router_bias.astype(F32)[:, None], (N_EXPERTS, TM))

    for i in range(DEPTH):
        kind, j = i % N_MIXERS, i // N_MIXERS
        if kind == 0:
            q, kb, vb = _qkv_project(tok, mods, i, a_w_qkv[j].astype(BF16), cos, sa, sb)
            mix = _attention(a_sink[j], q, kb, vb)
            w_out = a_w_o[j]
        elif kind == 1:
            bs = jnp.repeat(b_b_s[j], D_MODEL // GMLP_GROUPS, axis=1)
            mix = _gmlp_mixer(tok, mods, i, b_w_in[j].astype(BF16), b_b_in[j].reshape(1, -1),
                              b_ln_g[j].reshape(1, -1), b_ln_b[j].reshape(1, -1), b_w_s[j].astype(BF16), bs)
            w_out = b_w_out[j]
        else:
            mix = (c_w_in[j].astype(BF16), jnp.pad(c_w_conv[j], ((0, 5), (0, 0))))
            w_out = c_w_out[j]
        x1, xe, meta, counts = _post(mix, w_out.astype(BF16), tok, mods, i,
                                     ln1_g[i].reshape(1, -1), ln1_b[i].reshape(1, -1), rw, rb, conv=(kind == 2))
        pad_start, cnt, dest, lo_slot, hi_slot, need, nused = _dispatch_plan(meta, counts)
        xs = _scatter_rows(pad_start, cnt, dest, xe)
        ys = _experts(lo_slot, hi_slot, need, nused, xs, moe_w1, moe_w3, moe_w2, i)
        tok = _ln2(x1, ys, dest, mods, i, ln2_g[i].reshape(1, -1), ln2_b[i].reshape(1, -1),
                   latent_only=(i == DEPTH - 1))
    return tok.reshape(1, SEQ, D_MODEL)
```

```python
import functools

import jax
import jax.numpy as jnp
from jax import lax
from jax.experimental import pallas as pl
from jax.experimental.pallas import tpu as pltpu

F32 = jnp.float32
BF16 = jnp.bfloat16

D_MODEL = 1024
SEQ = 16384
DEPTH = 4
GRID_W = 64
CTX_LEN = 256
N_MIXERS = 3
N_HEADS = 16
N_KV_HEADS = 4
HEAD_DIM = 64
ATTN_BLOCK = 128
ATTN_SCALE = HEAD_DIM ** -0.5
ROPE_BASE = 10000.0
ROPE_FREQS = HEAD_DIM // 4
CHUNK = 128
GMLP_GROUPS = 8
N_EXPERTS = 16
N_EXPERT_GROUPS = 4
EXPERTS_PER_GROUP = 4
D_EXPERT = 512
ALPHA = (2 * DEPTH) ** 0.25
LN_EPS = 1e-5

LANES = 128
T_TOK = CTX_LEN + SEQ
TM = 256
N_TILES = T_TOK // TM
CTX_TILES = CTX_LEN // TM
N_ABLK = T_TOK // ATTN_BLOCK
CTX_ABLK = CTX_LEN // ATTN_BLOCK
PAIRS_PER_GROUP = 6
N_CLASSES = N_EXPERT_GROUPS * PAIRS_PER_GROUP
CLASS_ROWS = 32
MOE_BLOCK = 256
N_STAGING = 2
N_MOE_BLOCKS = T_TOK // MOE_BLOCK + N_CLASSES
P_ROWS = N_MOE_BLOCKS * MOE_BLOCK
XE_W = D_MODEL // 2 + LANES
SCAT_ROWS = 5 * TM
ROW_UNROLL = 64
NEG_BIG = -1e30
VMEM_LIMIT = 52 * 1024 * 1024


def _cparams(sem="arbitrary"):
    return pltpu.CompilerParams(dimension_semantics=(sem,), vmem_limit_bytes=VMEM_LIMIT)


def _mod_row(ref):
    is_ctx = pl.program_id(0) < CTX_TILES
    return jnp.where(is_ctx, ref[1:2, :], ref[0:1, :])


def _layer_norm(x, g, b):
    mu = jnp.mean(x, axis=-1, keepdims=True)
    xc = x - mu
    var = jnp.mean(xc * xc, axis=-1, keepdims=True)
    return xc * lax.rsqrt(var + LN_EPS) * g + b


def _sigmoid(x):
    return 1.0 / (1.0 + jnp.exp(-x))


def _mod_kernel(cc_ref, w_ref, b_ref, o_ref):
    cc = cc_ref[...]
    act = cc * _sigmoid(cc)
    o_ref[...] = jnp.dot(act.astype(BF16), w_ref[...].astype(BF16), preferred_element_type=F32) + b_ref[...]


def _modulation(cc, w_mod, b_mod):
    nt = 1536
    return pl.pallas_call(
        _mod_kernel,
        grid=(DEPTH, 6 * D_MODEL // nt),
        in_specs=[
            pl.BlockSpec((8, D_MODEL), lambda l, n: (0, 0)),
            pl.BlockSpec((None, D_MODEL, nt), lambda l, n: (l, 0, n)),
            pl.BlockSpec((None, 1, nt), lambda l, n: (l, 0, n)),
        ],
        out_specs=pl.BlockSpec((None, 8, nt), lambda l, n: (l, 0, n)),
        out_shape=jax.ShapeDtypeStruct((DEPTH, 8, 6 * D_MODEL), F32),
        compiler_params=pltpu.CompilerParams(
            dimension_semantics=("arbitrary", "arbitrary"), vmem_limit_bytes=VMEM_LIMIT),
        name="modulation",
    )(cc, w_mod, b_mod.reshape(DEPTH, 1, 6 * D_MODEL))


def _mod_spec(layer, chunk):
    return pl.BlockSpec((None, 8, D_MODEL), lambda i: (layer, 0, chunk))


def _store_split_heads(ref, tile, pair):
    low = lax.broadcasted_iota(jnp.int32, tile.shape, 1) < HEAD_DIM
    swapped = pltpu.roll(tile, HEAD_DIM, 1)
    parts = ((jnp.where(low, tile, 0.0), jnp.where(low, 0.0, swapped)),
             (jnp.where(low, swapped, 0.0), jnp.where(low, 0.0, tile)))
    blk = ATTN_BLOCK
    for h, (in_low, in_high) in enumerate(parts):
        cols = slice((2 * pair + h) * LANES, (2 * pair + h + 1) * LANES)
        for n in range(TM // blk):
            rows = slice(n * blk, (n + 1) * blk)
            ref[2 * n * blk:(2 * n + 1) * blk, cols] = in_low[rows].astype(BF16)
            ref[(2 * n + 1) * blk:(2 * n + 2) * blk, cols] = in_high[rows].astype(BF16)


def _token_operands(tokens):
    if isinstance(tokens, tuple):
        specs = [pl.BlockSpec((TM, D_MODEL), lambda i: (jnp.minimum(i, CTX_TILES - 1), 0)),
                 pl.BlockSpec((TM, D_MODEL), lambda i: (jnp.maximum(i - CTX_TILES, 0), 0))]
        return list(tokens), specs
    return [tokens], [pl.BlockSpec((TM, D_MODEL), lambda i: (i, 0))]


def _token_tile(tok_refs):
    if len(tok_refs) == 2:
        return jnp.where(pl.program_id(0) < CTX_TILES, tok_refs[0][...], tok_refs[1][...])
    return tok_refs[0][...]


def _qkv_kernel(*refs, n_tok):
    tok_refs = refs[:n_tok]
    sh_ref, sc_ref, w_ref, cos_ref, sa_ref, sb_ref, q_ref, k_ref, v_ref = refs[n_tok:]
    h = (_token_tile(tok_refs) * (1.0 + _mod_row(sc_ref)) + _mod_row(sh_ref)).astype(BF16)
    y = jnp.dot(h, w_ref[...], preferred_element_type=F32)
    cos, sa, sb = cos_ref[...], sa_ref[...], sb_ref[...]
    n_q = D_MODEL // LANES
    n_k = N_KV_HEADS * HEAD_DIM // LANES
    for t in range(n_q + n_k):
        yt = y[:, t * LANES:(t + 1) * LANES]
        r = yt * cos + pltpu.roll(yt, LANES - ROPE_FREQS, 1) * sa + pltpu.roll(yt, ROPE_FREQS, 1) * sb
        if t < n_q:
            q_ref[:, t * LANES:(t + 1) * LANES] = (r * ATTN_SCALE).astype(BF16)
        else:
            _store_split_heads(k_ref, r, t - n_q)
    for t in range(n_k):
        _store_split_heads(v_ref, y[:, (n_q + n_k + t) * LANES:(n_q + n_k + t + 1) * LANES], t)


def _qkv_project(tokens, mods, layer, w, cos, sa, sb):
    kvw = N_KV_HEADS * LANES
    row = lambda i: (i, 0)
    tok_args, tok_specs = _token_operands(tokens)
    return pl.pallas_call(
        functools.partial(_qkv_kernel, n_tok=len(tok_args)),
        grid=(N_TILES,),
        in_specs=tok_specs + [
            _mod_spec(layer, 0), _mod_spec(layer, 1),
            pl.BlockSpec(w.shape, lambda i: (0, 0)),
            pl.BlockSpec((TM, LANES), row), pl.BlockSpec((TM, LANES), row), pl.BlockSpec((TM, LANES), row),
        ],
        out_specs=[pl.BlockSpec((TM, D_MODEL), row), pl.BlockSpec((2 * TM, kvw), row),
                   pl.BlockSpec((2 * TM, kvw), row)],
        out_shape=[jax.ShapeDtypeStruct((T_TOK, D_MODEL), BF16),
                   jax.ShapeDtypeStruct((2 * T_TOK, kvw), BF16),
                   jax.ShapeDtypeStruct((2 * T_TOK, kvw), BF16)],
        compiler_params=_cparams("parallel"),
        name="qkv_rope",
    )(*tok_args, mods, mods, w, cos, sa, sb)


def _attn_kernel(sink_ref, q_ref, kp_ref, km_ref, kn_ref, kx_ref, vp_ref, vm_ref, vn_ref, vx_ref, o_ref):
    blk = ATTN_BLOCK
    r = lax.broadcasted_iota(jnp.int32, (2 * blk, 2 * blk), 0) & (blk - 1)
    c = lax.broadcasted_iota(jnp.int32, (2 * blk, 2 * blk), 1) & (blk - 1)
    top_rows = lax.broadcasted_iota(jnp.int32, (2 * blk, 1), 0) < blk
    even_cols = lax.broadcasted_iota(jnp.int32, (2 * blk, 2 * blk), 1) < blk
    even_lanes = lax.broadcasted_iota(jnp.int32, (2 * blk, LANES), 1) < HEAD_DIM
    nt_dims = (((1,), (1,)), ((), ()))
    t_rows = lax.broadcasted_iota(jnp.int32, (2 * CTX_LEN, LANES), 0)
    t_cols = lax.broadcasted_iota(jnp.int32, (2 * CTX_LEN, LANES), 1)
    tally = jnp.where(t_cols == ((t_rows >> (blk.bit_length() - 1)) & 1), 1.0, 0.0).astype(BF16)

    def halves(s):
        return s[:, :blk], s[:, blk:]

    windows = ((kp_ref, vp_ref, 0), (km_ref, vm_ref, 0), (km_ref, vm_ref, 2 * blk), (kn_ref, vn_ref, 0))
    for sub, j in [(sub, j) for sub in range(2) for j in range(N_KV_HEADS)]:
        i = 2 * pl.program_id(0) + sub
        prev_ok = i >= CTX_ABLK + 1
        cur_ok = i >= CTX_ABLK
        next_ok = jnp.logical_and(i >= CTX_ABLK, i <= N_ABLK - 2)
        m_prev = jnp.logical_and(c >= r, prev_ok)
        m_cur = jnp.logical_and(c >= 0, cur_ok)
        m_next = jnp.logical_and(c <= r, next_ok)
        q_rows = slice(sub * blk, (sub + 1) * blk)
        o_rows = q_rows
        (kp, vp, op), (kc, vc, oc), (kn, vn, on) = windows[sub:sub + 3]
        ks = slice(j * LANES, (j + 1) * LANES)
        k_blk = lambda ref, off: ref[off:off + 2 * blk, ks]
        lhs = jnp.concatenate([q_ref[q_rows, 2 * j * LANES:(2 * j + 1) * LANES],
                               q_ref[q_rows, (2 * j + 1) * LANES:(2 * j + 2) * LANES]], axis=0)
        s_p = jnp.where(m_prev, lax.dot_general(lhs, k_blk(kp, op), nt_dims, preferred_element_type=F32), NEG_BIG)
        s_c = jnp.where(m_cur, lax.dot_general(lhs, k_blk(kc, oc), nt_dims, preferred_element_type=F32), NEG_BIG)
        s_n = jnp.where(m_next, lax.dot_general(lhs, k_blk(kn, on), nt_dims, preferred_element_type=F32), NEG_BIG)
        s_x = lax.dot_general(lhs, kx_ref[:, ks], nt_dims, preferred_element_type=F32)
        s_x0, s_x1 = s_x[:, :2 * blk], s_x[:, 2 * blk:]
        pieces = (s_p, s_c, s_n, s_x0, s_x1)
        sink_e = jnp.where(top_rows, sink_ref[4 * j], sink_ref[4 * j + 2])
        sink_o = jnp.where(top_rows, sink_ref[4 * j + 1], sink_ref[4 * j + 3])
        tile_max = functools.reduce(jnp.maximum, pieces)
        tm_e, tm_o = halves(tile_max)
        m_e = jnp.maximum(jnp.max(tm_e, axis=1, keepdims=True), sink_e)
        m_o = jnp.maximum(jnp.max(tm_o, axis=1, keepdims=True), sink_o)
        m = jnp.where(even_cols, m_e, m_o)
        p_p, p_c, p_n, p_x0, p_x1 = [jnp.exp(s - m).astype(BF16) for s in pieces]
        p_x = jnp.concatenate([p_x0, p_x1], axis=1)
        with_tally = lambda v: jnp.concatenate([v, tally[:v.shape[0]]], axis=1)
        o = (jnp.dot(p_p, with_tally(k_blk(vp, op)), preferred_element_type=F32)
             + jnp.dot(p_c, with_tally(k_blk(vc, oc)), preferred_element_type=F32)
             + jnp.dot(p_n, with_tally(k_blk(vn, on)), preferred_element_type=F32)
             + jnp.dot(p_x, with_tally(vx_ref[:, ks]), preferred_element_type=F32))
        l_e = o[:, LANES:LANES + 1] + jnp.exp(sink_e - m_e)
        l_o = o[:, LANES + 1:LANES + 2] + jnp.exp(sink_o - m_o)
        o = (o[:, :LANES] / jnp.where(even_lanes, l_e, l_o)).astype(BF16)
        o_ref[o_rows, 2 * j * LANES:(2 * j + 1) * LANES] = o[:blk]
        o_ref[o_rows, (2 * j + 1) * LANES:(2 * j + 2) * LANES] = o[blk:]


def _attention(sink, q, kb, vb):
    kvw = N_KV_HEADS * LANES
    blk = ATTN_BLOCK
    lo, hi = CTX_ABLK, N_ABLK - 1
    prev_map = lambda s: (jnp.clip(2 * s - 1, lo, hi), 0)
    pair_map = lambda s: (s, 0)
    next_map = lambda s: (jnp.clip(2 * s + 2, lo, hi), 0)
    ctx_map = lambda s: (0, 0)
    kv_specs = [pl.BlockSpec((2 * blk, kvw), prev_map), pl.BlockSpec((4 * blk, kvw), pair_map),
                pl.BlockSpec((2 * blk, kvw), next_map), pl.BlockSpec((2 * CTX_LEN, kvw), ctx_map)]
    return pl.pallas_call(
        _attn_kernel,
        grid=(N_ABLK // 2,),
        in_specs=[pl.BlockSpec(memory_space=pltpu.SMEM), pl.BlockSpec((2 * blk, D_MODEL), pair_map)]
        + kv_specs + kv_specs,
        out_specs=pl.BlockSpec((2 * blk, D_MODEL), pair_map),
        out_shape=jax.ShapeDtypeStruct((T_TOK, D_MODEL), BF16),
        compiler_params=_cparams("parallel"),
        name="window_attention",
    )(sink, q, kb, kb, kb, kb, vb, vb, vb, vb)


def _gmlp_kernel(x_ref, sh_ref, sc_ref, w_ref, b_ref, g_ref, be_ref, ws_ref, bs_ref, o_ref):
    h = (x_ref[...] * (1.0 + _mod_row(sc_ref)) + _mod_row(sh_ref)).astype(BF16)
    z = jax.nn.gelu(jnp.dot(h, w_ref[...], preferred_element_type=F32) + b_ref[...], approximate=True)
    u = z[:, :D_MODEL]
    v = _layer_norm(z[:, D_MODEL:], g_ref[...], be_ref[...]).astype(BF16)
    gd = D_MODEL // GMLP_GROUPS
    for n in range(TM // CHUNK):
        rows = slice(n * CHUNK, (n + 1) * CHUNK)
        for g in range(GMLP_GROUPS):
            cols = slice(g * gd, (g + 1) * gd)
            s = jnp.dot(ws_ref[g], v[rows, cols], preferred_element_type=F32) + bs_ref[:, cols]
            o_ref[rows, cols] = (u[rows, cols] * s).astype(BF16)


def _gmlp_mixer(x, mods, layer, w_in, b_in, ln_g, ln_b, w_s, b_s):
    row = lambda i: (i, 0)
    fix2 = lambda i: (0, 0)
    return pl.pallas_call(
        _gmlp_kernel,
        grid=(N_TILES,),
        in_specs=[
            pl.BlockSpec((TM, D_MODEL), row),
            _mod_spec(layer, 0), _mod_spec(layer, 1),
            pl.BlockSpec((D_MODEL, 2 * D_MODEL), fix2),
            pl.BlockSpec((1, 2 * D_MODEL), fix2),
            pl.BlockSpec((1, D_MODEL), fix2), pl.BlockSpec((1, D_MODEL), fix2),
            pl.BlockSpec((GMLP_GROUPS, CHUNK, CHUNK), lambda i: (0, 0, 0)),
            pl.BlockSpec((CHUNK, D_MODEL), fix2),
        ],
        out_specs=pl.BlockSpec((TM, D_MODEL), row),
        out_shape=jax.ShapeDtypeStruct((T_TOK, D_MODEL), BF16),
        compiler_params=_cparams("parallel"),
        name="gmlp_mixer",
    )(x, mods, mods, w_in, b_in, ln_g, ln_b, w_s, b_s)


def _route(h2, rw_ref, rb_ref, tri_ref, carry_ref, xe_ref, meta_ref, cnt_ref):
    logits = jnp.dot(h2.astype(BF16), rw_ref[...], preferred_element_type=F32)
    s = _sigmoid(jnp.transpose(logits)[:N_EXPERTS, :])
    ssel = s + rb_ref[...]
    row = lambda v, k: v[k:k + 1, :]
    npg = EXPERTS_PER_GROUP

    def first_argmax(vals):
        best, idx = vals[0], jnp.zeros_like(vals[0])
        for k in range(1, len(vals)):
            better = vals[k] > best
            best = jnp.where(better, vals[k], best)
            idx = jnp.where(better, float(k), idx)
        return best, idx

    def pick(vals, idx):
        out = vals[-1]
        for k in range(len(vals) - 2, -1, -1):
            out = jnp.where(idx == float(k), vals[k], out)
        return out

    group_scores = []
    for g in range(N_EXPERT_GROUPS):
        a, b, c, d = (row(ssel, npg * g + k) for k in range(npg))
        group_scores.append(jnp.maximum(jnp.maximum(jnp.maximum(a + b, a + c), jnp.maximum(a + d, b + c)),
                                        jnp.maximum(b + d, c + d)))
    _, gsel = first_argmax(group_scores)
    biased = [pick([row(ssel, npg * g + k) for g in range(N_EXPERT_GROUPS)], gsel) for k in range(npg)]
    plain = [pick([row(s, npg * g + k) for g in range(N_EXPERT_GROUPS)], gsel) for k in range(npg)]
    _, i1 = first_argmax(biased)
    _, i2 = first_argmax([jnp.where(i1 == float(k), -jnp.inf, biased[k]) for k in range(npg)])
    lo = jnp.minimum(i1, i2)
    hi = jnp.maximum(i1, i2)
    s_lo = pick(plain, lo)
    s_hi = pick(plain, hi)
    den = s_lo + s_hi
    g_lo = s_lo / den
    g_hi = s_hi / den
    pair = lo * (7.0 - lo) * 0.5 + hi - lo - 1.0
    cls = gsel * PAIRS_PER_GROUP + pair
    cls_rows = lax.broadcasted_iota(jnp.int32, (CLASS_ROWS, TM), 0).astype(F32)
    onehot = cls_rows == cls
    oh_f = jnp.where(onehot, 1.0, 0.0)
    before = jnp.dot(oh_f.astype(BF16), tri_ref[...], preferred_element_type=F32)
    carry = carry_ref[...]
    rank = jnp.sum(jnp.where(onehot, before + carry[:, 0:1], 0.0), axis=0, keepdims=True)
    carry = carry + jnp.sum(oh_f, axis=1, keepdims=True)
    carry_ref[...] = carry
    cnt_ref[...] = carry
    as_bits = lambda v: lax.bitcast_convert_type(v.astype(BF16).astype(F32), jnp.uint32)
    half = D_MODEL // 2
    xe_ref[:, :half] = as_bits(h2[:, :half]) | lax.shift_right_logical(as_bits(h2[:, half:]), jnp.uint32(16))
    gate_rows = lax.broadcasted_iota(jnp.int32, (LANES, TM), 0) < LANES // 2
    xe_ref[:, half:] = lax.bitcast_convert_type(jnp.transpose(jnp.where(gate_rows, g_lo, g_hi)), jnp.uint32)
    meta_rows = lax.broadcasted_iota(jnp.int32, (8, TM), 0)
    meta_ref[...] = jnp.where(meta_rows == 0, cls, jnp.where(meta_rows == 1, rank, 0.0)).astype(jnp.int32)


def _gated_conv(x, xp_ref, xn_ref, sha_ref, sca_ref, win_ref, wc_ref):
    i = pl.program_id(0)
    halo = jnp.concatenate([x, xp_ref[...], xn_ref[...]], axis=0)
    h = (halo * (1.0 + _mod_row(sca_ref)) + _mod_row(sha_ref)).astype(BF16)
    proj = jnp.dot(h, win_ref[...], preferred_element_type=F32)
    gate = proj[:TM, :D_MODEL]
    zz_all = proj[:, D_MODEL:2 * D_MODEL] * proj[:, 2 * D_MODEL:]
    zz = zz_all[:TM]
    prev_ok = i >= CTX_TILES + 1
    next_ok = jnp.logical_and(i >= CTX_TILES, i <= N_TILES - 2)
    z_before = jnp.where(prev_ok, zz_all[TM + 7:TM + 8], 0.0)
    z_after = jnp.where(next_ok, zz_all[TM + 8:TM + 9], 0.0)
    rows = lax.broadcasted_iota(jnp.int32, (TM, 1), 0)
    zm1 = jnp.where(rows == 0, z_before, pltpu.roll(zz, 1, 0))
    zp1 = jnp.where(rows == TM - 1, z_after, pltpu.roll(zz, TM - 1, 0))
    zc = wc_ref[0:1, :] * zm1 + wc_ref[1:2, :] * zz + wc_ref[2:3, :] * zp1
    return (gate * zc).astype(BF16)


def _post_kernel(*refs, conv, n_tok):
    tok_refs, refs = refs[:n_tok], refs[n_tok:]
    if conv:
        (xp_ref, xn_ref, sha_ref, sca_ref, win_ref, wc_ref, *refs) = refs
    else:
        (a_ref, *refs) = refs
    (w_ref, ga_ref, shf_ref, scf_ref, lg_ref, lb_ref, rw_ref, rb_ref,
     x1_ref, xe_ref, meta_ref, cnt_ref, tri_ref, carry_ref) = refs

    @pl.when(pl.program_id(0) == 0)
    def _():
        carry_ref[...] = jnp.zeros_like(carry_ref)
        rr = lax.broadcasted_iota(jnp.int32, (TM, TM), 0)
        cc = lax.broadcasted_iota(jnp.int32, (TM, TM), 1)
        tri_ref[...] = jnp.where(rr < cc, 1.0, 0.0).astype(BF16)

    x = _token_tile(tok_refs)
    if conv:
        a = _gated_conv(x, xp_ref, xn_ref, sha_ref, sca_ref, win_ref, wc_ref)
    else:
        a = a_ref[...]
    y = jnp.dot(a, w_ref[...], preferred_element_type=F32)
    x1 = _layer_norm(ALPHA * x + _mod_row(ga_ref) * y, lg_ref[...], lb_ref[...])
    x1_ref[...] = x1
    h2 = x1 * (1.0 + _mod_row(scf_ref)) + _mod_row(shf_ref)
    _route(h2, rw_ref, rb_ref, tri_ref, carry_ref, xe_ref, meta_ref, cnt_ref)


def _post(mix, w_out, tokens, mods, layer, ln_g, ln_b, rw, rb, conv):
    row = lambda i: (i, 0)
    fix2 = lambda i: (0, 0)
    tile = pl.BlockSpec((TM, D_MODEL), row)
    tok_args, tok_specs = _token_operands(tokens)
    if conv:
        assert len(tok_args) == 1
        w_in, wc = mix
        sub = TM // 8
        mix_args = (tok_args[0], tok_args[0], mods, mods, w_in, wc)
        mix_specs = [pl.BlockSpec((8, D_MODEL), lambda i: (jnp.maximum(i * sub - 1, 0), 0)),
                     pl.BlockSpec((8, D_MODEL), lambda i: (jnp.minimum((i + 1) * sub, T_TOK // 8 - 1), 0)),
                     _mod_spec(layer, 0), _mod_spec(layer, 1),
                     pl.BlockSpec(w_in.shape, fix2), pl.BlockSpec((8, D_MODEL), fix2)]
    else:
        mix_args = (mix,)
        mix_specs = [tile]
    vec = pl.BlockSpec((1, D_MODEL), fix2)
    return pl.pallas_call(
        functools.partial(_post_kernel, conv=conv, n_tok=len(tok_args)),
        grid=(N_TILES,),
        in_specs=tok_specs + mix_specs + [
            pl.BlockSpec((D_MODEL, D_MODEL), fix2),
            _mod_spec(layer, 2), _mod_spec(layer, 3), _mod_spec(layer, 4),
            vec, vec,
            pl.BlockSpec((D_MODEL, LANES), fix2), pl.BlockSpec((N_EXPERTS, TM), fix2),
        ],
        out_specs=[tile, pl.BlockSpec((TM, XE_W), row), pl.BlockSpec((None, 8, TM), lambda i: (i, 0, 0)),
                   pl.BlockSpec((CLASS_ROWS, LANES), fix2)],
        out_shape=[jax.ShapeDtypeStruct((T_TOK, D_MODEL), F32),
                   jax.ShapeDtypeStruct((T_TOK, XE_W), jnp.uint32),
                   jax.ShapeDtypeStruct((N_TILES, 8, TM), jnp.int32),
                   jax.ShapeDtypeStruct((CLASS_ROWS, LANES), F32)],
        scratch_shapes=[pltpu.VMEM((TM, TM), BF16), pltpu.VMEM((CLASS_ROWS, LANES), F32)],
        compiler_params=_cparams("arbitrary"),
        name="post_conv" if conv else "post",
    )(*tok_args, *mix_args, w_out, mods, mods, mods, ln_g, ln_b, rw, rb)


def _scatter_kernel(pstart_ref, cnt_ref, dest_ref, xe_ref, xs_ref, zero_ref, sem):
    def row_copy(src, d):
        return pltpu.make_async_copy(src, xs_ref.at[pl.ds(d, 1)], sem)

    first_row = pl.program_id(0) * SCAT_ROWS

    def issue(jo, carry):
        base = pl.multiple_of(jo * ROW_UNROLL, ROW_UNROLL)
        group = xe_ref.at[pl.ds(base, ROW_UNROLL)]
        for k in range(ROW_UNROLL):
            row_copy(group.at[pl.ds(k, 1)], dest_ref[first_row + base + k]).start()
        return carry

    lax.fori_loop(0, SCAT_ROWS // ROW_UNROLL, issue, 0)

    def drain(j, carry):
        row_copy(xe_ref.at[pl.ds(0, 1)], 0).wait()
        return carry

    lax.fori_loop(0, SCAT_ROWS, drain, 0, unroll=8)

    @pl.when(pl.program_id(0) == pl.num_programs(0) - 1)
    def _():
        zero_ref[...] = jnp.zeros_like(zero_ref)

        def per_class(c, carry):
            n = cnt_ref[c]
            first = pstart_ref[c] + n
            last = pstart_ref[c] + jnp.bitwise_and(n + (MOE_BLOCK - 1), -MOE_BLOCK)
            aligned = jnp.minimum(jnp.bitwise_and(first + 7, -8), last)

            def group_copy(k):
                row = pl.multiple_of(aligned + 8 * k, 8)
                return pltpu.make_async_copy(zero_ref.at[pl.ds(0, 8)], xs_ref.at[pl.ds(row, 8)], sem)

            def fill(k, cc):
                row_copy(zero_ref.at[pl.ds(0, 1)], first + k).start()
                return cc

            def fill_wait(k, cc):
                row_copy(zero_ref.at[pl.ds(0, 1)], 0).wait()
                return cc

            def fill_group(k, cc):
                group_copy(k).start()
                return cc

            def fill_group_wait(k, cc):
                group_copy(k).wait()
                return cc

            n_groups = lax.shift_right_logical(last - aligned, 3)
            lax.fori_loop(0, aligned - first, fill, 0)
            lax.fori_loop(0, n_groups, fill_group, 0)
            lax.fori_loop(0, aligned - first, fill_wait, 0)
            lax.fori_loop(0, n_groups, fill_group_wait, 0)
            return carry

        lax.fori_loop(0, N_CLASSES, per_class, 0)

        last = N_CLASSES - 1
        rows_used = pstart_ref[last] + jnp.bitwise_and(cnt_ref[last] + (MOE_BLOCK - 1), -MOE_BLOCK)
        first_free = lax.shift_right_logical(rows_used, MOE_BLOCK.bit_length() - 1)

        def block_copy(blk):
            return pltpu.make_async_copy(zero_ref, xs_ref.at[pl.ds(blk * MOE_BLOCK, MOE_BLOCK)], sem)

        def fill_block(blk, cc):
            block_copy(blk).start()
            return cc

        def fill_block_wait(blk, cc):
            block_copy(0).wait()
            return cc

        lax.fori_loop(first_free, N_MOE_BLOCKS, fill_block, 0)
        lax.fori_loop(first_free, N_MOE_BLOCKS, fill_block_wait, 0)


def _scatter_rows(pstart, cnt, dest, xe):
    grid_spec = pltpu.PrefetchScalarGridSpec(
        num_scalar_prefetch=2,
        grid=(T_TOK // SCAT_ROWS,),
        in_specs=[pl.BlockSpec((T_TOK,), lambda i, ps, cn: (0,), memory_space=pltpu.SMEM),
                  pl.BlockSpec((SCAT_ROWS, XE_W), lambda i, ps, cn: (i, 0))],
        out_specs=pl.BlockSpec(memory_space=pl.ANY),
        scratch_shapes=[pltpu.VMEM((MOE_BLOCK, XE_W), xe.dtype), pltpu.SemaphoreType.DMA(())],
    )
    return pl.pallas_call(
        _scatter_kernel,
        grid_spec=grid_spec,
        out_shape=jax.ShapeDtypeStruct((P_ROWS, XE_W), xe.dtype),
        compiler_params=_cparams("arbitrary"),
        name="scatter_rows",
    )(pstart, cnt, dest.reshape(T_TOK), xe)


def _expert_kernel(lo_ref, hi_ref, need_ref, nused_ref, x_ref, w1_hbm, w3_hbm, w2_hbm, o_ref,
                   st1, st3, st2, w1s, w3s, w2s, sems, done_ref, *, layer):
    b = pl.program_id(0)

    def fetch(k, buf):
        return (pltpu.make_async_copy(w1_hbm.at[layer, k], st1.at[buf], sems.at[buf]),
                pltpu.make_async_copy(w3_hbm.at[layer, k], st3.at[buf], sems.at[buf]),
                pltpu.make_async_copy(w2_hbm.at[layer, k], st2.at[buf], sems.at[buf]))

    @pl.when(b == 0)
    def _():
        done_ref[0] = 0
        for k in range(N_STAGING):
            for cp in fetch(k, k):
                cp.start()

    used = b < nused_ref[0]

    @pl.when(jnp.logical_not(used))
    def _():
        o_ref[...] = jnp.zeros_like(o_ref)

    @pl.when(used)
    def _():
        def install(k, carry):
            buf = k % N_STAGING
            slot = k % EXPERTS_PER_GROUP
            for cp in fetch(k, buf):
                cp.wait()
            w1s[slot] = st1[buf].astype(BF16)
            w3s[slot] = st3[buf].astype(BF16)
            w2s[slot] = st2[buf].astype(BF16)

            @pl.when(k + N_STAGING < N_EXPERTS)
            def _():
                for cp in fetch(k + N_STAGING, buf):
                    cp.start()

            return carry

        done = done_ref[0]
        lax.fori_loop(done, need_ref[b], install, 0)
        done_ref[0] = jnp.maximum(done, need_ref[b])

        half = D_MODEL // 2
        words = x_ref[:, :half]
        as_f32 = lambda bits: lax.bitcast_convert_type(bits, F32)
        xb = jnp.concatenate([as_f32(words & jnp.uint32(0xFFFF0000)),
                              as_f32(lax.shift_left(words, jnp.uint32(16)))], axis=1).astype(BF16)
        gates = as_f32(x_ref[:, half:])

        def expert(slot):
            h1 = jnp.dot(xb, w1s[slot], preferred_element_type=F32)
            h3 = jnp.dot(xb, w3s[slot], preferred_element_type=F32)
            act = (h1 * _sigmoid(h1) * h3).astype(BF16)
            return jnp.dot(act, w2s[slot], preferred_element_type=F32)

        ya = expert(lo_ref[b])
        yb = expert(hi_ref[b])
        o_ref[...] = gates[:, 0:1] * ya + gates[:, LANES // 2:LANES // 2 + 1] * yb

    @pl.when(b == pl.num_programs(0) - 1)
    def _():
        def drain(k, carry):
            for cp in fetch(k, k % N_STAGING):
                cp.wait()
            return carry

        done = done_ref[0]
        lax.fori_loop(done, jnp.minimum(done + N_STAGING, N_EXPERTS), drain, 0)


def _experts(lo_slot, hi_slot, need, nused, xs, w1, w3, w2, layer):
    rows = lambda b, lo, hi, nd, nu: (jnp.maximum(jnp.minimum(b, nu[0] - 1), 0), 0)
    any_spec = pl.BlockSpec(memory_space=pl.ANY)
    up, down = (D_MODEL, D_EXPERT), (D_EXPERT, D_MODEL)
    grid_spec = pltpu.PrefetchScalarGridSpec(
        num_scalar_prefetch=4,
        grid=(N_MOE_BLOCKS,),
        in_specs=[pl.BlockSpec((MOE_BLOCK, XE_W), rows), any_spec, any_spec, any_spec],
        out_specs=pl.BlockSpec((MOE_BLOCK, D_MODEL), lambda b, lo, hi, nd, nu: (b, 0)),
        scratch_shapes=[pltpu.VMEM((N_STAGING,) + up, F32), pltpu.VMEM((N_STAGING,) + up, F32),
                        pltpu.VMEM((N_STAGING,) + down, F32),
                        pltpu.VMEM((EXPERTS_PER_GROUP,) + up, BF16), pltpu.VMEM((EXPERTS_PER_GROUP,) + up, BF16),
                        pltpu.VMEM((EXPERTS_PER_GROUP,) + down, BF16),
                        pltpu.SemaphoreType.DMA((N_STAGING,)), pltpu.SMEM((1,), jnp.int32)],
    )
    return pl.pallas_call(
        functools.partial(_expert_kernel, layer=layer),
        grid_spec=grid_spec,
        out_shape=jax.ShapeDtypeStruct((P_ROWS, D_MODEL), F32),
        compiler_params=_cparams("arbitrary"),
        name="expert_pairs",
    )(lo_slot, hi_slot, need, nused, xs, w1, w3, w2)


def _ln2_kernel(dest_ref, dest_next_ref, ys_ref, x_ref, gf_ref, lg_ref, lb_ref, o_ref, ybuf, sems):
    i = pl.program_id(0)
    slot = i % 2

    def row_copy(src_row, dst_rows, dst_slot):
        return pltpu.make_async_copy(ys_ref.at[pl.ds(src_row, 1)], dst_rows, sems.at[dst_slot])

    def issue_tile(idx_ref, dst_slot):
        def issue(jo, carry):
            base = pl.multiple_of(jo * ROW_UNROLL, ROW_UNROLL)
            group = ybuf.at[dst_slot, pl.ds(base, ROW_UNROLL)]
            for k in range(ROW_UNROLL):
                row_copy(idx_ref[base + k], group.at[pl.ds(k, 1)], dst_slot).start()
            return carry

        lax.fori_loop(0, TM // ROW_UNROLL, issue, 0)

    @pl.when(i == 0)
    def _():
        issue_tile(dest_ref, 0)

    @pl.when(i + 1 < pl.num_programs(0))
    def _():
        issue_tile(dest_next_ref, 1 - slot)

    def drain(j, carry):
        row_copy(0, ybuf.at[slot, pl.ds(0, 1)], slot).wait()
        return carry

    lax.fori_loop(0, TM, drain, 0, unroll=8)
    o_ref[...] = _layer_norm(ALPHA * x_ref[...] + _mod_row(gf_ref) * ybuf[slot], lg_ref[...], lb_ref[...])


def _ln2(x1, ys, dest, mods, layer, ln_g, ln_b, latent_only):
    row = lambda i: (i, 0)
    tile = pl.BlockSpec((TM, D_MODEL), row)
    vec = pl.BlockSpec((1, D_MODEL), lambda i: (0, 0))
    idx = lambda f: pl.BlockSpec((TM,), f, memory_space=pltpu.SMEM)
    if latent_only:
        out_rows = SEQ
        out_spec = pl.BlockSpec((TM, D_MODEL), lambda i: (jnp.maximum(i - CTX_TILES, 0), 0))
    else:
        out_rows = T_TOK
        out_spec = tile
    return pl.pallas_call(
        _ln2_kernel,
        grid=(N_TILES,),
        in_specs=[idx(lambda i: (i,)), idx(lambda i: (jnp.minimum(i + 1, N_TILES - 1),)),
                  pl.BlockSpec(memory_space=pl.ANY), tile, _mod_spec(layer, 5), vec, vec],
        out_specs=out_spec,
        out_shape=jax.ShapeDtypeStruct((out_rows, D_MODEL), F32),
        scratch_shapes=[pltpu.VMEM((2, TM, D_MODEL), F32), pltpu.SemaphoreType.DMA((2,))],
        compiler_params=_cparams("arbitrary"),
        name="moe_gather_residual_ln",
    )(dest.reshape(T_TOK), dest.reshape(T_TOK), ys, x1, mods, ln_g, ln_b)


def _pair_tables():
    pairs = [(a, b) for a in range(EXPERTS_PER_GROUP) for b in range(a + 1, EXPERTS_PER_GROUP)]
    return jnp.array([p[0] for p in pairs], jnp.int32), jnp.array([p[1] for p in pairs], jnp.int32)


def _dispatch_plan(meta, counts):
    cnt = counts[:N_CLASSES, 0].astype(jnp.int32)
    padded = (cnt + MOE_BLOCK - 1) // MOE_BLOCK * MOE_BLOCK
    pad_end = jnp.cumsum(padded)
    pad_start = pad_end - padded
    nused = pad_end[-1:] // MOE_BLOCK
    cls, rank = meta[:, 0:1, :], meta[:, 1:2, :]
    before = (cls[..., None] > jnp.arange(N_CLASSES, dtype=jnp.int32)).astype(jnp.int32)
    dest = rank + jnp.sum(before * padded, axis=-1)
    blk_row = jnp.arange(N_MOE_BLOCKS, dtype=jnp.int32) * MOE_BLOCK
    blk_cls = jnp.minimum(jnp.sum((pad_end[None, :] <= blk_row[:, None]).astype(jnp.int32), axis=1),
                          N_CLASSES - 1)
    pair_lo, pair_hi = _pair_tables()
    lo_slot = pair_lo[blk_cls % PAIRS_PER_GROUP]
    hi_slot = pair_hi[blk_cls % PAIRS_PER_GROUP]
    need = (blk_cls // PAIRS_PER_GROUP) * EXPERTS_PER_GROUP + hi_slot + 1
    return pad_start, cnt, dest, lo_slot, hi_slot, need, nused


def _rope_tables():
    n_rows = SEQ // GRID_W
    freqs = jnp.power(ROPE_BASE, -jnp.arange(ROPE_FREQS, dtype=F32) / ROPE_FREQS)
    ar = jnp.arange(n_rows).astype(F32)[:, None] * freqs[None, :]
    ac = jnp.arange(GRID_W).astype(F32)[:, None] * freqs[None, :]

    def table(fn):
        by_row = jnp.broadcast_to(fn(ar)[:, None, :], (n_rows, GRID_W, ROPE_FREQS)).reshape(SEQ, ROPE_FREQS)
        by_col = jnp.broadcast_to(fn(ac)[None, :, :], (n_rows, GRID_W, ROPE_FREQS)).reshape(SEQ, ROPE_FREQS)
        return jnp.concatenate([by_row, by_row, by_col, by_col], axis=-1)

    cos, sin = table(jnp.cos), table(jnp.sin)
    cos = jnp.concatenate([jnp.ones((CTX_LEN, HEAD_DIM), F32), cos], axis=0)
    sin = jnp.concatenate([jnp.zeros((CTX_LEN, HEAD_DIM), F32), sin], axis=0)
    first_half = (jnp.arange(HEAD_DIM) % (2 * ROPE_FREQS)) < ROPE_FREQS
    sa = jnp.where(first_half[None, :], -sin, 0.0)
    sb = jnp.where(first_half[None, :], 0.0, sin)
    two = lambda t: jnp.concatenate([t, t], axis=-1)
    return two(cos), two(sa), two(sb)


def kernel(x, c, ctx, c_ctx, w_mod, b_mod, ln1_g, ln1_b, ln2_g, ln2_b, router_w, router_bias, moe_w1, moe_w3, moe_w2, a_w_qkv, a_w_o, a_sink, b_w_in, b_b_in, b_ln_g, b_ln_b, b_w_s, b_b_s, b_w_out, c_w_in, c_w_conv, c_w_out):
    assert x.shape == (1, SEQ, D_MODEL) and ctx.shape == (1, CTX_LEN, D_MODEL)
    tok = (ctx[0], x[0])
    cc = jnp.zeros((8, D_MODEL), F32).at[0].set(c[0]).at[1].set(c_ctx)
    mods = _modulation(cc, w_mod, b_mod)
    cos, sa, sb = _rope_tables()
    rw = jnp.pad(router_w, ((0, 0), (0, LANES - N_EXPERTS))).astype(BF16)
    rb = jnp.broadcast_to(router_bias.astype(F32)[:, None], (N_EXPERTS, TM))

    for i in range(DEPTH):
        kind, j = i % N_MIXERS, i // N_MIXERS
        if kind == 0:
            q, kb, vb = _qkv_project(tok, mods, i, a_w_qkv[j].astype(BF16), cos, sa, sb)
            mix = _attention(a_sink[j], q, kb, vb)
            w_out = a_w_o[j]
        elif kind == 1:
            bs = jnp.repeat(b_b_s[j], D_MODEL // GMLP_GROUPS, axis=1)
            mix = _gmlp_mixer(tok, mods, i, b_w_in[j].astype(BF16), b_b_in[j].reshape(1, -1),
                              b_ln_g[j].reshape(1, -1), b_ln_b[j].reshape(1, -1), b_w_s[j].astype(BF16), bs)
            w_out = b_w_out[j]
        else:
            mix = (c_w_in[j].astype(BF16), jnp.pad(c_w_conv[j], ((0, 5), (0, 0))))
            w_out = c_w_out[j]
        x1, xe, meta, counts = _post(mix, w_out.astype(BF16), tok, mods, i,
                                     ln1_g[i].reshape(1, -1), ln1_b[i].reshape(1, -1), rw, rb, conv=(kind == 2))
        pad_start, cnt, dest, lo_slot, hi_slot, need, nused = _dispatch_plan(meta, counts)
        xs = _scatter_rows(pad_start, cnt, dest, xe)
        ys = _experts(lo_slot, hi_slot, need, nused, xs, moe_w1, moe_w3, moe_w2, i)
        tok = _ln2(x1, ys, dest, mods, i, ln2_g[i].reshape(1, -1), ln2_b[i].reshape(1, -1),
                   latent_only=(i == DEPTH - 1))
    return tok.reshape(1, SEQ, D_MODEL)
```

```python
import functools

import jax
import jax.numpy as jnp
from jax import lax
from jax.experimental import pallas as pl
from jax.experimental.pallas import tpu as pltpu

F32 = jnp.float32
BF16 = jnp.bfloat16

D_MODEL = 1024
SEQ = 16384
DEPTH = 4
GRID_W = 64
CTX_LEN = 256
N_MIXERS = 3
N_HEADS = 16
N_KV_HEADS = 4
HEAD_DIM = 64
ATTN_BLOCK = 128
ATTN_SCALE = HEAD_DIM ** -0.5
ROPE_BASE = 10000.0
ROPE_FREQS = HEAD_DIM // 4
CHUNK = 128
GMLP_GROUPS = 8
N_EXPERTS = 16
N_EXPERT_GROUPS = 4
EXPERTS_PER_GROUP = 4
D_EXPERT = 512
ALPHA = (2 * DEPTH) ** 0.25
LN_EPS = 1e-5

LANES = 128
T_TOK = CTX_LEN + SEQ
TM = 256
N_TILES = T_TOK // TM
CTX_TILES = CTX_LEN // TM
N_ABLK = T_TOK // ATTN_BLOCK
CTX_ABLK = CTX_LEN // ATTN_BLOCK
PAIRS_PER_GROUP = 6
N_CLASSES = N_EXPERT_GROUPS * PAIRS_PER_GROUP
CLASS_ROWS = 32
MOE_BLOCK = 256
N_STAGING = 2
N_MOE_BLOCKS = T_TOK // MOE_BLOCK + N_CLASSES
P_ROWS = N_MOE_BLOCKS * MOE_BLOCK
XE_W = D_MODEL // 2 + LANES
SCAT_ROWS = 5 * TM
ROW_UNROLL = 64
NEG_BIG = -1e30
VMEM_LIMIT = 52 * 1024 * 1024


def _cparams(sem="arbitrary"):
    return pltpu.CompilerParams(dimension_semantics=(sem,), vmem_limit_bytes=VMEM_LIMIT)


def _mod_row(ref):
    is_ctx = pl.program_id(0) < CTX_TILES
    return jnp.where(is_ctx, ref[1:2, :], ref[0:1, :])


def _layer_norm(x, g, b):
    mu = jnp.mean(x, axis=-1, keepdims=True)
    xc = x - mu
    var = jnp.mean(xc * xc, axis=-1, keepdims=True)
    return xc * lax.rsqrt(var + LN_EPS) * g + b


def _sigmoid(x):
    return 1.0 / (1.0 + jnp.exp(-x))


def _mod_kernel(cc_ref, w_ref, b_ref, o_ref):
    cc = cc_ref[...]
    act = cc * _sigmoid(cc)
    o_ref[...] = jnp.dot(act.astype(BF16), w_ref[...].astype(BF16), preferred_element_type=F32) + b_ref[...]


def _modulation(cc, w_mod, b_mod):
    nt = 1536
    return pl.pallas_call(
        _mod_kernel,
        grid=(DEPTH, 6 * D_MODEL // nt),
        in_specs=[
            pl.BlockSpec((8, D_MODEL), lambda l, n: (0, 0)),
            pl.BlockSpec((None, D_MODEL, nt), lambda l, n: (l, 0, n)),
            pl.BlockSpec((None, 1, nt), lambda l, n: (l, 0, n)),
        ],
        out_specs=pl.BlockSpec((None, 8, nt), lambda l, n: (l, 0, n)),
        out_shape=jax.ShapeDtypeStruct((DEPTH, 8, 6 * D_MODEL), F32),
        compiler_params=pltpu.CompilerParams(
            dimension_semantics=("arbitrary", "arbitrary"), vmem_limit_bytes=VMEM_LIMIT),
        name="modulation",
    )(cc, w_mod, b_mod.reshape(DEPTH, 1, 6 * D_MODEL))


def _mod_spec(layer, chunk):
    return pl.BlockSpec((None, 8, D_MODEL), lambda i: (layer, 0, chunk))


def _store_split_heads(ref, tile, pair):
    low = lax.broadcasted_iota(jnp.int32, tile.shape, 1) < HEAD_DIM
    swapped = pltpu.roll(tile, HEAD_DIM, 1)
    parts = ((jnp.where(low, tile, 0.0), jnp.where(low, 0.0, swapped)),
             (jnp.where(low, swapped, 0.0), jnp.where(low, 0.0, tile)))
    blk = ATTN_BLOCK
    for h, (in_low, in_high) in enumerate(parts):
        cols = slice((2 * pair + h) * LANES, (2 * pair + h + 1) * LANES)
        for n in range(TM // blk):
            rows = slice(n * blk, (n + 1) * blk)
            ref[2 * n * blk:(2 * n + 1) * blk, cols] = in_low[rows].astype(BF16)
            ref[(2 * n + 1) * blk:(2 * n + 2) * blk, cols] = in_high[rows].astype(BF16)


def _token_operands(tokens):
    if isinstance(tokens, tuple):
        specs = [pl.BlockSpec((TM, D_MODEL), lambda i: (jnp.minimum(i, CTX_TILES - 1), 0)),
                 pl.BlockSpec((TM, D_MODEL), lambda i: (jnp.maximum(i - CTX_TILES, 0), 0))]
        return list(tokens), specs
    return [tokens], [pl.BlockSpec((TM, D_MODEL), lambda i: (i, 0))]


def _token_tile(tok_refs):
    if len(tok_refs) == 2:
        return jnp.where(pl.program_id(0) < CTX_TILES, tok_refs[0][...], tok_refs[1][...])
    return tok_refs[0][...]


def _qkv_kernel(*refs, n_tok):
    tok_refs = refs[:n_tok]
    sh_ref, sc_ref, w_ref, cos_ref, sa_ref, sb_ref, q_ref, k_ref, v_ref = refs[n_tok:]
    h = (_token_tile(tok_refs) * (1.0 + _mod_row(sc_ref)) + _mod_row(sh_ref)).astype(BF16)
    y = jnp.dot(h, w_ref[...], preferred_element_type=F32)
    cos, sa, sb = cos_ref[...], sa_ref[...], sb_ref[...]
    n_q = D_MODEL // LANES
    n_k = N_KV_HEADS * HEAD_DIM // LANES
    for t in range(n_q + n_k):
        yt = y[:, t * LANES:(t + 1) * LANES]
        r = yt * cos + pltpu.roll(yt, LANES - ROPE_FREQS, 1) * sa + pltpu.roll(yt, ROPE_FREQS, 1) * sb
        if t < n_q:
            q_ref[:, t * LANES:(t + 1) * LANES] = (r * ATTN_SCALE).astype(BF16)
        else:
            _store_split_heads(k_ref, r, t - n_q)
    for t in range(n_k):
        _store_split_heads(v_ref, y[:, (n_q + n_k + t) * LANES:(n_q + n_k + t + 1) * LANES], t)


def _qkv_project(tokens, mods, layer, w, cos, sa, sb):
    kvw = N_KV_HEADS * LANES
    row = lambda i: (i, 0)
    tok_args, tok_specs = _token_operands(tokens)
    return pl.pallas_call(
        functools.partial(_qkv_kernel, n_tok=len(tok_args)),
        grid=(N_TILES,),
        in_specs=tok_specs + [
            _mod_spec(layer, 0), _mod_spec(layer, 1),
            pl.BlockSpec(w.shape, lambda i: (0, 0)),
            pl.BlockSpec((TM, LANES), row), pl.BlockSpec((TM, LANES), row), pl.BlockSpec((TM, LANES), row),
        ],
        out_specs=[pl.BlockSpec((TM, D_MODEL), row), pl.BlockSpec((2 * TM, kvw), row),
                   pl.BlockSpec((2 * TM, kvw), row)],
        out_shape=[jax.ShapeDtypeStruct((T_TOK, D_MODEL), BF16),
                   jax.ShapeDtypeStruct((2 * T_TOK, kvw), BF16),
                   jax.ShapeDtypeStruct((2 * T_TOK, kvw), BF16)],
        compiler_params=_cparams("parallel"),
        name="qkv_rope",
    )(*tok_args, mods, mods, w, cos, sa, sb)


def _attn_kernel(sink_ref, q_ref, kp_ref, km_ref, kn_ref, kx_ref, vp_ref, vm_ref, vn_ref, vx_ref, o_ref):
    blk = ATTN_BLOCK
    r = lax.broadcasted_iota(jnp.int32, (2 * blk, 2 * blk), 0) & (blk - 1)
    c = lax.broadcasted_iota(jnp.int32, (2 * blk, 2 * blk), 1) & (blk - 1)
    top_rows = lax.broadcasted_iota(jnp.int32, (2 * blk, 1), 0) < blk
    even_cols = lax.broadcasted_iota(jnp.int32, (2 * blk, 2 * blk), 1) < blk
    even_lanes = lax.broadcasted_iota(jnp.int32, (2 * blk, LANES), 1) < HEAD_DIM
    nt_dims = (((1,), (1,)), ((), ()))
    t_rows = lax.broadcasted_iota(jnp.int32, (2 * CTX_LEN, LANES), 0)
    t_cols = lax.broadcasted_iota(jnp.int32, (2 * CTX_LEN, LANES), 1)
    tally = jnp.where(t_cols == ((t_rows >> (blk.bit_length() - 1)) & 1), 1.0, 0.0).astype(BF16)

    def halves(s):
        return s[:, :blk], s[:, blk:]

    windows = ((kp_ref, vp_ref, 0), (km_ref, vm_ref, 0), (km_ref, vm_ref, 2 * blk), (kn_ref, vn_ref, 0))
    for sub, j in [(sub, j) for sub in range(2) for j in range(N_KV_HEADS)]:
        i = 2 * pl.program_id(0) + sub
        prev_ok = i >= CTX_ABLK + 1
        cur_ok = i >= CTX_ABLK
        next_ok = jnp.logical_and(i >= CTX_ABLK, i <= N_ABLK - 2)
        m_prev = jnp.logical_and(c >= r, prev_ok)
        m_cur = jnp.logical_and(c >= 0, cur_ok)
        m_next = jnp.logical_and(c <= r, next_ok)
        q_rows = slice(sub * blk, (sub + 1) * blk)
        o_rows = q_rows
        (kp, vp, op), (kc, vc, oc), (kn, vn, on) = windows[sub:sub + 3]
        ks = slice(j * LANES, (j + 1) * LANES)
        k_blk = lambda ref, off: ref[off:off + 2 * blk, ks]
        lhs = jnp.concatenate([q_ref[q_rows, 2 * j * LANES:(2 * j + 1) * LANES],
                               q_ref[q_rows, (2 * j + 1) * LANES:(2 * j + 2) * LANES]], axis=0)
        s_p = jnp.where(m_prev, lax.dot_general(lhs, k_blk(kp, op), nt_dims, preferred_element_type=F32), NEG_BIG)
        s_c = jnp.where(m_cur, lax.dot_general(lhs, k_blk(kc, oc), nt_dims, preferred_element_type=F32), NEG_BIG)
        s_n = jnp.where(m_next, lax.dot_general(lhs, k_blk(kn, on), nt_dims, preferred_element_type=F32), NEG_BIG)
        s_x = lax.dot_general(lhs, kx_ref[:, ks], nt_dims, preferred_element_type=F32)
        s_x0, s_x1 = s_x[:, :2 * blk], s_x[:, 2 * blk:]
        pieces = (s_p, s_c, s_n, s_x0, s_x1)
        sink_e = jnp.where(top_rows, sink_ref[4 * j], sink_ref[4 * j + 2])
        sink_o = jnp.where(top_rows, sink_ref[4 * j + 1], sink_ref[4 * j + 3])
        tile_max = functools.reduce(jnp.maximum, pieces)
        tm_e, tm_o = halves(tile_max)
        m_e = jnp.maximum(jnp.max(tm_e, axis=1, keepdims=True), sink_e)
        m_o = jnp.maximum(jnp.max(tm_o, axis=1, keepdims=True), sink_o)
        m = jnp.where(even_cols, m_e, m_o)
        p_p, p_c, p_n, p_x0, p_x1 = [jnp.exp(s - m).astype(BF16) for s in pieces]
        p_x = jnp.concatenate([p_x0, p_x1], axis=1)
        with_tally = lambda v: jnp.concatenate([v, tally[:v.shape[0]]], axis=1)
        o = (jnp.dot(p_p, with_tally(k_blk(vp, op)), preferred_element_type=F32)
             + jnp.dot(p_c, with_tally(k_blk(vc, oc)), preferred_element_type=F32)
             + jnp.dot(p_n, with_tally(k_blk(vn, on)), preferred_element_type=F32)
             + jnp.dot(p_x, with_tally(vx_ref[:, ks]), preferred_element_type=F32))
        l_e = o[:, LANES:LANES + 1] + jnp.exp(sink_e - m_e)
        l_o = o[:, LANES + 1:LANES + 2] + jnp.exp(sink_o - m_o)
        o = (o[:, :LANES] / jnp.where(even_lanes, l_e, l_o)).astype(BF16)
        o_ref[o_rows, 2 * j * LANES:(2 * j + 1) * LANES] = o[:blk]
        o_ref[o_rows, (2 * j + 1) * LANES:(2 * j + 2) * LANES] = o[blk:]


def _attention(sink, q, kb, vb):
    kvw = N_KV_HEADS * LANES
    blk = ATTN_BLOCK
    lo, hi = CTX_ABLK, N_ABLK - 1
    prev_map = lambda s: (jnp.clip(2 * s - 1, lo, hi), 0)
    pair_map = lambda s: (s, 0)
    next_map = lambda s: (jnp.clip(2 * s + 2, lo, hi), 0)
    ctx_map = lambda s: (0, 0)
    kv_specs = [pl.BlockSpec((2 * blk, kvw), prev_map), pl.BlockSpec((4 * blk, kvw), pair_map),
                pl.BlockSpec((2 * blk, kvw), next_map), pl.BlockSpec((2 * CTX_LEN, kvw), ctx_map)]
    return pl.pallas_call(
        _attn_kernel,
        grid=(N_ABLK // 2,),
        in_specs=[pl.BlockSpec(memory_space=pltpu.SMEM), pl.BlockSpec((2 * blk, D_MODEL), pair_map)]
        + kv_specs + kv_specs,
        out_specs=pl.BlockSpec((2 * blk, D_MODEL), pair_map),
        out_shape=jax.ShapeDtypeStruct((T_TOK, D_MODEL), BF16),
        compiler_params=_cparams("parallel"),
        name="window_attention",
    )(sink, q, kb, kb, kb, kb, vb, vb, vb, vb)


def _gmlp_kernel(x_ref, sh_ref, sc_ref, w_ref, b_ref, g_ref, be_ref, ws_ref, bs_ref, o_ref):
    h = (x_ref[...] * (1.0 + _mod_row(sc_ref)) + _mod_row(sh_ref)).astype(BF16)
    z = jax.nn.gelu(jnp.dot(h, w_ref[...], preferred_element_type=F32) + b_ref[...], approximate=True)
    u = z[:, :D_MODEL]
    v = _layer_norm(z[:, D_MODEL:], g_ref[...], be_ref[...]).astype(BF16)
    gd = D_MODEL // GMLP_GROUPS
    for n in range(TM // CHUNK):
        rows = slice(n * CHUNK, (n + 1) * CHUNK)
        for g in range(GMLP_GROUPS):
            cols = slice(g * gd, (g + 1) * gd)
            s = jnp.dot(ws_ref[g], v[rows, cols], preferred_element_type=F32) + bs_ref[:, cols]
            o_ref[rows, cols] = (u[rows, cols] * s).astype(BF16)


def _gmlp_mixer(x, mods, layer, w_in, b_in, ln_g, ln_b, w_s, b_s):
    row = lambda i: (i, 0)
    fix2 = lambda i: (0, 0)
    return pl.pallas_call(
        _gmlp_kernel,
        grid=(N_TILES,),
        in_specs=[
            pl.BlockSpec((TM, D_MODEL), row),
            _mod_spec(layer, 0), _mod_spec(layer, 1),
            pl.BlockSpec((D_MODEL, 2 * D_MODEL), fix2),
            pl.BlockSpec((1, 2 * D_MODEL), fix2),
            pl.BlockSpec((1, D_MODEL), fix2), pl.BlockSpec((1, D_MODEL), fix2),
            pl.BlockSpec((GMLP_GROUPS, CHUNK, CHUNK), lambda i: (0, 0, 0)),
            pl.BlockSpec((CHUNK, D_MODEL), fix2),
        ],
        out_specs=pl.BlockSpec((TM, D_MODEL), row),
        out_shape=jax.ShapeDtypeStruct((T_TOK, D_MODEL), BF16),
        compiler_params=_cparams("parallel"),
        name="gmlp_mixer",
    )(x, mods, mods, w_in, b_in, ln_g, ln_b, w_s, b_s)


def _route(h2, rw_ref, rb_ref, tri_ref, carry_ref, xe_ref, meta_ref, cnt_ref):
    logits = jnp.dot(h2.astype(BF16), rw_ref[...], preferred_element_type=F32)
    s = _sigmoid(jnp.transpose(logits)[:N_EXPERTS, :])
    ssel = s + rb_ref[...]
    row = lambda v, k: v[k:k + 1, :]
    npg = EXPERTS_PER_GROUP

    def first_argmax(vals):
        best, idx = vals[0], jnp.zeros_like(vals[0])
        for k in range(1, len(vals)):
            better = vals[k] > best
            best = jnp.where(better, vals[k], best)
            idx = jnp.where(better, float(k), idx)
        return best, idx

    def pick(vals, idx):
        out = vals[-1]
        for k in range(len(vals) - 2, -1, -1):
            out = jnp.where(idx == float(k), vals[k], out)
        return out

    group_scores = []
    for g in range(N_EXPERT_GROUPS):
        a, b, c, d = (row(ssel, npg * g + k) for k in range(npg))
        group_scores.append(jnp.maximum(jnp.maximum(jnp.maximum(a + b, a + c), jnp.maximum(a + d, b + c)),
                                        jnp.maximum(b + d, c + d)))
    _, gsel = first_argmax(group_scores)
    biased = [pick([row(ssel, npg * g + k) for g in range(N_EXPERT_GROUPS)], gsel) for k in range(npg)]
    plain = [pick([row(s, npg * g + k) for g in range(N_EXPERT_GROUPS)], gsel) for k in range(npg)]
    _, i1 = first_argmax(biased)
    _, i2 = first_argmax([jnp.where(i1 == float(k), -jnp.inf, biased[k]) for k in range(npg)])
    lo = jnp.minimum(i1, i2)
    hi = jnp.maximum(i1, i2)
    s_lo = pick(plain, lo)
    s_hi = pick(plain, hi)
    den = s_lo + s_hi
    g_lo = s_lo / den
    g_hi = s_hi / den
    pair = lo * (7.0 - lo) * 0.5 + hi - lo - 1.0
    cls = gsel * PAIRS_PER_GROUP + pair
    cls_rows = lax.broadcasted_iota(jnp.int32, (CLASS_ROWS, TM), 0).astype(F32)
    onehot = cls_rows == cls
    oh_f = jnp.where(onehot, 1.0, 0.0)
    before = jnp.dot(oh_f.astype(BF16), tri_ref[...], preferred_element_type=F32)
    carry = carry_ref[...]
    rank = jnp.sum(jnp.where(onehot, before + carry[:, 0:1], 0.0), axis=0, keepdims=True)
    carry = carry + jnp.sum(oh_f, axis=1, keepdims=True)
    carry_ref[...] = carry
    cnt_ref[...] = carry
    as_bits = lambda v: lax.bitcast_convert_type(v.astype(BF16).astype(F32), jnp.uint32)
    half = D_MODEL // 2
    xe_ref[:, :half] = as_bits(h2[:, :half]) | lax.shift_right_logical(as_bits(h2[:, half:]), jnp.uint32(16))
    gate_rows = lax.broadcasted_iota(jnp.int32, (LANES, TM), 0) < LANES // 2
    xe_ref[:, half:] = lax.bitcast_convert_type(jnp.transpose(jnp.where(gate_rows, g_lo, g_hi)), jnp.uint32)
    meta_rows = lax.broadcasted_iota(jnp.int32, (8, TM), 0)
    meta_ref[...] = jnp.where(meta_rows == 0, cls, jnp.where(meta_rows == 1, rank, 0.0)).astype(jnp.int32)


def _gated_conv(x, xp_ref, xn_ref, sha_ref, sca_ref, win_ref, wc_ref):
    i = pl.program_id(0)
    halo = jnp.concatenate([x, xp_ref[...], xn_ref[...]], axis=0)
    h = (halo * (1.0 + _mod_row(sca_ref)) + _mod_row(sha_ref)).astype(BF16)
    proj = jnp.dot(h, win_ref[...], preferred_element_type=F32)
    gate = proj[:TM, :D_MODEL]
    zz_all = proj[:, D_MODEL:2 * D_MODEL] * proj[:, 2 * D_MODEL:]
    zz = zz_all[:TM]
    prev_ok = i >= CTX_TILES + 1
    next_ok = jnp.logical_and(i >= CTX_TILES, i <= N_TILES - 2)
    z_before = jnp.where(prev_ok, zz_all[TM + 7:TM + 8], 0.0)
    z_after = jnp.where(next_ok, zz_all[TM + 8:TM + 9], 0.0)
    rows = lax.broadcasted_iota(jnp.int32, (TM, 1), 0)
    zm1 = jnp.where(rows == 0, z_before, pltpu.roll(zz, 1, 0))
    zp1 = jnp.where(rows == TM - 1, z_after, pltpu.roll(zz, TM - 1, 0))
    zc = wc_ref[0:1, :] * zm1 + wc_ref[1:2, :] * zz + wc_ref[2:3, :] * zp1
    return (gate * zc).astype(BF16)


def _post_kernel(*refs, conv, n_tok):
    tok_refs, refs = refs[:n_tok], refs[n_tok:]
    if conv:
        (xp_ref, xn_ref, sha_ref, sca_ref, win_ref, wc_ref, *refs) = refs
    else:
        (a_ref, *refs) = refs
    (w_ref, ga_ref, shf_ref, scf_ref, lg_ref, lb_ref, rw_ref, rb_ref,
     x1_ref, xe_ref, meta_ref, cnt_ref, tri_ref, carry_ref) = refs

    @pl.when(pl.program_id(0) == 0)
    def _():
        carry_ref[...] = jnp.zeros_like(carry_ref)
        rr = lax.broadcasted_iota(jnp.int32, (TM, TM), 0)
        cc = lax.broadcasted_iota(jnp.int32, (TM, TM), 1)
        tri_ref[...] = jnp.where(rr < cc, 1.0, 0.0).astype(BF16)

    x = _token_tile(tok_refs)
    if conv:
        a = _gated_conv(x, xp_ref, xn_ref, sha_ref, sca_ref, win_ref, wc_ref)
    else:
        a = a_ref[...]
    y = jnp.dot(a, w_ref[...], preferred_element_type=F32)
    x1 = _layer_norm(ALPHA * x + _mod_row(ga_ref) * y, lg_ref[...], lb_ref[...])
    x1_ref[...] = x1
    h2 = x1 * (1.0 + _mod_row(scf_ref)) + _mod_row(shf_ref)
    _route(h2, rw_ref, rb_ref, tri_ref, carry_ref, xe_ref, meta_ref, cnt_ref)


def _post(mix, w_out, tokens, mods, layer, ln_g, ln_b, rw, rb, conv):
    row = lambda i: (i, 0)
    fix2 = lambda i: (0, 0)
    tile = pl.BlockSpec((TM, D_MODEL), row)
    tok_args, tok_specs = _token_operands(tokens)
    if conv:
        assert len(tok_args) == 1
        w_in, wc = mix
        sub = TM // 8
        mix_args = (tok_args[0], tok_args[0], mods, mods, w_in, wc)
        mix_specs = [pl.BlockSpec((8, D_MODEL), lambda i: (jnp.maximum(i * sub - 1, 0), 0)),
                     pl.BlockSpec((8, D_MODEL), lambda i: (jnp.minimum((i + 1) * sub, T_TOK // 8 - 1), 0)),
                     _mod_spec(layer, 0), _mod_spec(layer, 1),
                     pl.BlockSpec(w_in.shape, fix2), pl.BlockSpec((8, D_MODEL), fix2)]
    else:
        mix_args = (mix,)
        mix_specs = [tile]
    vec = pl.BlockSpec((1, D_MODEL), fix2)
    return pl.pallas_call(
        functools.partial(_post_kernel, conv=conv, n_tok=len(tok_args)),
        grid=(N_TILES,),
        in_specs=tok_specs + mix_specs + [
            pl.BlockSpec((D_MODEL, D_MODEL), fix2),
            _mod_spec(layer, 2), _mod_spec(layer, 3), _mod_spec(layer, 4),
            vec, vec,
            pl.BlockSpec((D_MODEL, LANES), fix2), pl.BlockSpec((N_EXPERTS, TM), fix2),
        ],
        out_specs=[tile, pl.BlockSpec((TM, XE_W), row), pl.BlockSpec((None, 8, TM), lambda i: (i, 0, 0)),
                   pl.BlockSpec((CLASS_ROWS, LANES), fix2)],
        out_shape=[jax.ShapeDtypeStruct((T_TOK, D_MODEL), F32),
                   jax.ShapeDtypeStruct((T_TOK, XE_W), jnp.uint32),
                   jax.ShapeDtypeStruct((N_TILES, 8, TM), jnp.int32),
                   jax.ShapeDtypeStruct((CLASS_ROWS, LANES), F32)],
        scratch_shapes=[pltpu.VMEM((TM, TM), BF16), pltpu.VMEM((CLASS_ROWS, LANES), F32)],
        compiler_params=_cparams("arbitrary"),
        name="post_conv" if conv else "post",
    )(*tok_args, *mix_args, w_out, mods, mods, mods, ln_g, ln_b, rw, rb)


def _scatter_kernel(pstart_ref, cnt_ref, dest_ref, xe_ref, xs_ref, zero_ref, sem):
    def row_copy(src, d):
        return pltpu.make_async_copy(src, xs_ref.at[pl.ds(d, 1)], sem)

    first_row = pl.program_id(0) * SCAT_ROWS

    def issue(jo, carry):
        base = pl.multiple_of(jo * ROW_UNROLL, ROW_UNROLL)
        group = xe_ref.at[pl.ds(base, ROW_UNROLL)]
        for k in range(ROW_UNROLL):
            row_copy(group.at[pl.ds(k, 1)], dest_ref[first_row + base + k]).start()
        return carry

    lax.fori_loop(0, SCAT_ROWS // ROW_UNROLL, issue, 0)

    def drain(j, carry):
        row_copy(xe_ref.at[pl.ds(0, 1)], 0).wait()
        return carry

    lax.fori_loop(0, SCAT_ROWS, drain, 0, unroll=8)

    @pl.when(pl.program_id(0) == pl.num_programs(0) - 1)
    def _():
        zero_ref[...] = jnp.zeros_like(zero_ref)

        def per_class(c, carry):
            n = cnt_ref[c]
            first = pstart_ref[c] + n
            last = pstart_ref[c] + jnp.bitwise_and(n + (MOE_BLOCK - 1), -MOE_BLOCK)
            aligned = jnp.minimum(jnp.bitwise_and(first + 7, -8), last)

            def group_copy(k):
                row = pl.multiple_of(aligned + 8 * k, 8)
                return pltpu.make_async_copy(zero_ref.at[pl.ds(0, 8)], xs_ref.at[pl.ds(row, 8)], sem)

            def fill(k, cc):
                row_copy(zero_ref.at[pl.ds(0, 1)], first + k).start()
                return cc

            def fill_wait(k, cc):
                row_copy(zero_ref.at[pl.ds(0, 1)], 0).wait()
                return cc

            def fill_group(k, cc):
                group_copy(k).start()
                return cc

            def fill_group_wait(k, cc):
                group_copy(k).wait()
                return cc

            n_groups = lax.shift_right_logical(last - aligned, 3)
            lax.fori_loop(0, aligned - first, fill, 0)
            lax.fori_loop(0, n_groups, fill_group, 0)
            lax.fori_loop(0, aligned - first, fill_wait, 0)
            lax.fori_loop(0, n_groups, fill_group_wait, 0)
            return carry

        lax.fori_loop(0, N_CLASSES, per_class, 0)

        last = N_CLASSES - 1
        rows_used = pstart_ref[last] + jnp.bitwise_and(cnt_ref[last] + (MOE_BLOCK - 1), -MOE_BLOCK)
        first_free = lax.shift_right_logical(rows_used, MOE_BLOCK.bit_length() - 1)

        def block_copy(blk):
            return pltpu.make_async_copy(zero_ref, xs_ref.at[pl.ds(blk * MOE_BLOCK, MOE_BLOCK)], sem)

        def fill_block(blk, cc):
            block_copy(blk).start()
            return cc

        def fill_block_wait(blk, cc):
            block_copy(0).wait()
            return cc

        lax.fori_loop(first_free, N_MOE_BLOCKS, fill_block, 0)
        lax.fori_loop(first_free, N_MOE_BLOCKS, fill_block_wait, 0)


def _scatter_rows(pstart, cnt, dest, xe):
    grid_spec = pltpu.PrefetchScalarGridSpec(
        num_scalar_prefetch=2,
        grid=(T_TOK // SCAT_ROWS,),
        in_specs=[pl.BlockSpec((T_TOK,), lambda i, ps, cn: (0,), memory_space=pltpu.SMEM),
                  pl.BlockSpec((SCAT_ROWS, XE_W), lambda i, ps, cn: (i, 0))],
        out_specs=pl.BlockSpec(memory_space=pl.ANY),
        scratch_shapes=[pltpu.VMEM((MOE_BLOCK, XE_W), xe.dtype), pltpu.SemaphoreType.DMA(())],
    )
    return pl.pallas_call(
        _scatter_kernel,
        grid_spec=grid_spec,
        out_shape=jax.ShapeDtypeStruct((P_ROWS, XE_W), xe.dtype),
        compiler_params=_cparams("arbitrary"),
        name="scatter_rows",
    )(pstart, cnt, dest.reshape(T_TOK), xe)


def _expert_kernel(lo_ref, hi_ref, need_ref, half_ref, nused_ref, x_ref, w1_hbm, w3_hbm, w2_hbm, o_ref,
                   st1, st3, st2, w1s, w3s, w2s, sems, done_ref, *, layer):
    b = pl.program_id(0)

    def fetch(k, buf):
        return (pltpu.make_async_copy(w1_hbm.at[layer, k], st1.at[buf], sems.at[buf]),
                pltpu.make_async_copy(w3_hbm.at[layer, k], st3.at[buf], sems.at[buf]),
                pltpu.make_async_copy(w2_hbm.at[layer, k], st2.at[buf], sems.at[buf]))

    @pl.when(b == 0)
    def _():
        done_ref[0] = 0
        for k in range(N_STAGING):
            for cp in fetch(k, k):
                cp.start()

    used = b < nused_ref[0]

    @pl.when(jnp.logical_not(used))
    def _():
        o_ref[...] = jnp.zeros_like(o_ref)

    @pl.when(used)
    def _():
        def install(k, carry):
            buf = k % N_STAGING
            slot = k % EXPERTS_PER_GROUP
            for cp in fetch(k, buf):
                cp.wait()
            w1s[slot] = st1[buf].astype(BF16)
            w3s[slot] = st3[buf].astype(BF16)
            w2s[slot] = st2[buf].astype(BF16)

            @pl.when(k + N_STAGING < N_EXPERTS)
            def _():
                for cp in fetch(k + N_STAGING, buf):
                    cp.start()

            return carry

        done = done_ref[0]
        lax.fori_loop(done, need_ref[b], install, 0)
        done_ref[0] = jnp.maximum(done, need_ref[b])

        def compute(n_rows):
            half = D_MODEL // 2
            words = x_ref[:n_rows, :half]
            as_f32 = lambda bits: lax.bitcast_convert_type(bits, F32)
            xb = jnp.concatenate([as_f32(words & jnp.uint32(0xFFFF0000)),
                                  as_f32(lax.shift_left(words, jnp.uint32(16)))], axis=1).astype(BF16)
            gates = as_f32(x_ref[:n_rows, half:])

            def expert(slot):
                h1 = jnp.dot(xb, w1s[slot], preferred_element_type=F32)
                h3 = jnp.dot(xb, w3s[slot], preferred_element_type=F32)
                act = (h1 * _sigmoid(h1) * h3).astype(BF16)
                return jnp.dot(act, w2s[slot], preferred_element_type=F32)

            ya = expert(lo_ref[b])
            yb = expert(hi_ref[b])
            o_ref[:n_rows, :] = gates[:, 0:1] * ya + gates[:, LANES // 2:LANES // 2 + 1] * yb

        @pl.when(half_ref[b] == 0)
        def _():
            compute(MOE_BLOCK)

        @pl.when(half_ref[b] != 0)
        def _():
            compute(MOE_BLOCK // 2)
            o_ref[MOE_BLOCK // 2:, :] = jnp.zeros((MOE_BLOCK // 2, D_MODEL), F32)

    @pl.when(b == pl.num_programs(0) - 1)
    def _():
        def drain(k, carry):
            for cp in fetch(k, k % N_STAGING):
                cp.wait()
            return carry

        done = done_ref[0]
        lax.fori_loop(done, jnp.minimum(done + N_STAGING, N_EXPERTS), drain, 0)


def _experts(blocks, nused, xs, w1, w3, w2, layer):
    rows = lambda b, lo, hi, nd, hf, nu: (jnp.maximum(jnp.minimum(b, nu[0] - 1), 0), 0)
    any_spec = pl.BlockSpec(memory_space=pl.ANY)
    up, down = (D_MODEL, D_EXPERT), (D_EXPERT, D_MODEL)
    grid_spec = pltpu.PrefetchScalarGridSpec(
        num_scalar_prefetch=5,
        grid=(N_MOE_BLOCKS,),
        in_specs=[pl.BlockSpec((MOE_BLOCK, XE_W), rows), any_spec, any_spec, any_spec],
        out_specs=pl.BlockSpec((MOE_BLOCK, D_MODEL), lambda b, lo, hi, nd, hf, nu: (b, 0)),
        scratch_shapes=[pltpu.VMEM((N_STAGING,) + up, F32), pltpu.VMEM((N_STAGING,) + up, F32),
                        pltpu.VMEM((N_STAGING,) + down, F32),
                        pltpu.VMEM((EXPERTS_PER_GROUP,) + up, BF16), pltpu.VMEM((EXPERTS_PER_GROUP,) + up, BF16),
                        pltpu.VMEM((EXPERTS_PER_GROUP,) + down, BF16),
                        pltpu.SemaphoreType.DMA((N_STAGING,)), pltpu.SMEM((1,), jnp.int32)],
    )
    return pl.pallas_call(
        functools.partial(_expert_kernel, layer=layer),
        grid_spec=grid_spec,
        out_shape=jax.ShapeDtypeStruct((P_ROWS, D_MODEL), F32),
        compiler_params=_cparams("arbitrary"),
        name="expert_pairs",
    )(*blocks, nused, xs, w1, w3, w2)


def _ln2_kernel(dest_ref, dest_next_ref, ys_ref, x_ref, gf_ref, lg_ref, lb_ref, o_ref, ybuf, sems):
    i = pl.program_id(0)
    slot = i % 2

    def row_copy(src_row, dst_rows, dst_slot):
        return pltpu.make_async_copy(ys_ref.at[pl.ds(src_row, 1)], dst_rows, sems.at[dst_slot])

    def issue_tile(idx_ref, dst_slot):
        def issue(jo, carry):
            base = pl.multiple_of(jo * ROW_UNROLL, ROW_UNROLL)
            group = ybuf.at[dst_slot, pl.ds(base, ROW_UNROLL)]
            for k in range(ROW_UNROLL):
                row_copy(idx_ref[base + k], group.at[pl.ds(k, 1)], dst_slot).start()
            return carry

        lax.fori_loop(0, TM // ROW_UNROLL, issue, 0)

    @pl.when(i == 0)
    def _():
        issue_tile(dest_ref, 0)

    @pl.when(i + 1 < pl.num_programs(0))
    def _():
        issue_tile(dest_next_ref, 1 - slot)

    def drain(j, carry):
        row_copy(0, ybuf.at[slot, pl.ds(0, 1)], slot).wait()
        return carry

    lax.fori_loop(0, TM, drain, 0, unroll=8)
    o_ref[...] = _layer_norm(ALPHA * x_ref[...] + _mod_row(gf_ref) * ybuf[slot], lg_ref[...], lb_ref[...])


def _ln2(x1, ys, dest, mods, layer, ln_g, ln_b, latent_only):
    row = lambda i: (i, 0)
    tile = pl.BlockSpec((TM, D_MODEL), row)
    vec = pl.BlockSpec((1, D_MODEL), lambda i: (0, 0))
    idx = lambda f: pl.BlockSpec((TM,), f, memory_space=pltpu.SMEM)
    if latent_only:
        out_rows = SEQ
        out_spec = pl.BlockSpec((TM, D_MODEL), lambda i: (jnp.maximum(i - CTX_TILES, 0), 0))
    else:
        out_rows = T_TOK
        out_spec = tile
    return pl.pallas_call(
        _ln2_kernel,
        grid=(N_TILES,),
        in_specs=[idx(lambda i: (i,)), idx(lambda i: (jnp.minimum(i + 1, N_TILES - 1),)),
                  pl.BlockSpec(memory_space=pl.ANY), tile, _mod_spec(layer, 5), vec, vec],
        out_specs=out_spec,
        out_shape=jax.ShapeDtypeStruct((out_rows, D_MODEL), F32),
        scratch_shapes=[pltpu.VMEM((2, TM, D_MODEL), F32), pltpu.SemaphoreType.DMA((2,))],
        compiler_params=_cparams("arbitrary"),
        name="moe_gather_residual_ln",
    )(dest.reshape(T_TOK), dest.reshape(T_TOK), ys, x1, mods, ln_g, ln_b)


def _pair_tables():
    pairs = [(a, b) for a in range(EXPERTS_PER_GROUP) for b in range(a + 1, EXPERTS_PER_GROUP)]
    return jnp.array([p[0] for p in pairs], jnp.int32), jnp.array([p[1] for p in pairs], jnp.int32)


def _dispatch_plan(meta, counts):
    cnt = counts[:N_CLASSES, 0].astype(jnp.int32)
    padded = (cnt + MOE_BLOCK - 1) // MOE_BLOCK * MOE_BLOCK
    pad_end = jnp.cumsum(padded)
    pad_start = pad_end - padded
    nused = pad_end[-1:] // MOE_BLOCK
    cls, rank = meta[:, 0:1, :], meta[:, 1:2, :]
    before = (cls[..., None] > jnp.arange(N_CLASSES, dtype=jnp.int32)).astype(jnp.int32)
    dest = rank + jnp.sum(before * padded, axis=-1)
    blk_row = jnp.arange(N_MOE_BLOCKS, dtype=jnp.int32) * MOE_BLOCK
    blk_cls = jnp.minimum(jnp.sum((pad_end[None, :] <= blk_row[:, None]).astype(jnp.int32), axis=1),
                          N_CLASSES - 1)
    pair_lo, pair_hi = _pair_tables()
    lo_slot = pair_lo[blk_cls % PAIRS_PER_GROUP]
    hi_slot = pair_hi[blk_cls % PAIRS_PER_GROUP]
    need = (blk_cls // PAIRS_PER_GROUP) * EXPERTS_PER_GROUP + hi_slot + 1
    rows_left = cnt[blk_cls] - (blk_row - pad_start[blk_cls])
    half_full = (rows_left <= MOE_BLOCK // 2).astype(jnp.int32)
    return pad_start, cnt, dest, (lo_slot, hi_slot, need, half_full), nused


def _rope_tables():
    n_rows = SEQ // GRID_W
    freqs = jnp.power(ROPE_BASE, -jnp.arange(ROPE_FREQS, dtype=F32) / ROPE_FREQS)
    ar = jnp.arange(n_rows).astype(F32)[:, None] * freqs[None, :]
    ac = jnp.arange(GRID_W).astype(F32)[:, None] * freqs[None, :]

    def table(fn):
        by_row = jnp.broadcast_to(fn(ar)[:, None, :], (n_rows, GRID_W, ROPE_FREQS)).reshape(SEQ, ROPE_FREQS)
        by_col = jnp.broadcast_to(fn(ac)[None, :, :], (n_rows, GRID_W, ROPE_FREQS)).reshape(SEQ, ROPE_FREQS)
        return jnp.concatenate([by_row, by_row, by_col, by_col], axis=-1)

    cos, sin = table(jnp.cos), table(jnp.sin)
    cos = jnp.concatenate([jnp.ones((CTX_LEN, HEAD_DIM), F32), cos], axis=0)
    sin = jnp.concatenate([jnp.zeros((CTX_LEN, HEAD_DIM), F32), sin], axis=0)
    first_half = (jnp.arange(HEAD_DIM) % (2 * ROPE_FREQS)) < ROPE_FREQS
    sa = jnp.where(first_half[None, :], -sin, 0.0)
    sb = jnp.where(first_half[None, :], 0.0, sin)
    two = lambda t: jnp.concatenate([t, t], axis=-1)
    return two(cos), two(sa), two(sb)


def kernel(x, c, ctx, c_ctx, w_mod, b_mod, ln1_g, ln1_b, ln2_g, ln2_b, router_w, router_bias, moe_w1, moe_w3, moe_w2, a_w_qkv, a_w_o, a_sink, b_w_in, b_b_in, b_ln_g, b_ln_b, b_w_s, b_b_s, b_w_out, c_w_in, c_w_conv, c_w_out):
    assert x.shape == (1, SEQ, D_MODEL) and ctx.shape == (1, CTX_LEN, D_MODEL)
    tok = (ctx[0], x[0])
    cc = jnp.zeros((8, D_MODEL), F32).at[0].set(c[0]).at[1].set(c_ctx)
    mods = _modulation(cc, w_mod, b_mod)
    cos, sa, sb = _rope_tables()
    rw = jnp.pad(router_w, ((0, 0), (0, LANES - N_EXPERTS))).astype(BF16)
    rb = jnp.broadcast_to(router_bias.astype(F32)[:, None], (N_EXPERTS, TM))

    for i in range(DEPTH):
        kind, j = i % N_MIXERS, i // N_MIXERS
        if kind == 0:
            q, kb, vb = _qkv_project(tok, mods, i, a_w_qkv[j].astype(BF16), cos, sa, sb)
            mix = _attention(a_sink[j], q, kb, vb)
            w_out = a_w_o[j]
        elif kind == 1:
            bs = jnp.repeat(b_b_s[j], D_MODEL // GMLP_GROUPS, axis=1)
            mix = _gmlp_mixer(tok, mods, i, b_w_in[j].astype(BF16), b_b_in[j].reshape(1, -1),
                              b_ln_g[j].reshape(1, -1), b_ln_b[j].reshape(1, -1), b_w_s[j].astype(BF16), bs)
            w_out = b_w_out[j]
        else:
            mix = (c_w_in[j].astype(BF16), jnp.pad(c_w_conv[j], ((0, 5), (0, 0))))
            w_out = c_w_out[j]
        x1, xe, meta, counts = _post(mix, w_out.astype(BF16), tok, mods, i,
                                     ln1_g[i].reshape(1, -1), ln1_b[i].reshape(1, -1), rw, rb, conv=(kind == 2))
        pad_start, cnt, dest, blocks, nused = _dispatch_plan(meta, counts)
        xs = _scatter_rows(pad_start, cnt, dest, xe)
        ys = _experts(blocks, nused, xs, moe_w1, moe_w3, moe_w2, i)
        tok = _ln2(x1, ys, dest, mods, i, ln2_g[i].reshape(1, -1), ln2_b[i].reshape(1, -1),
                   latent_only=(i == DEPTH - 1))
    return tok.reshape(1, SEQ, D_MODEL)
```

```python
import functools

import jax
import jax.numpy as jnp
from jax import lax
from jax.experimental import pallas as pl
from jax.experimental.pallas import tpu as pltpu

F32 = jnp.float32
BF16 = jnp.bfloat16

D_MODEL = 1024
SEQ = 16384
DEPTH = 4
GRID_W = 64
CTX_LEN = 256
N_MIXERS = 3
N_HEADS = 16
N_KV_HEADS = 4
HEAD_DIM = 64
ATTN_BLOCK = 128
ATTN_SCALE = HEAD_DIM ** -0.5
ROPE_BASE = 10000.0
ROPE_FREQS = HEAD_DIM // 4
CHUNK = 128
GMLP_GROUPS = 8
N_EXPERTS = 16
N_EXPERT_GROUPS = 4
EXPERTS_PER_GROUP = 4
D_EXPERT = 512
ALPHA = (2 * DEPTH) ** 0.25
LN_EPS = 1e-5

LANES = 128
T_TOK = CTX_LEN + SEQ
TM = 256
N_TILES = T_TOK // TM
CTX_TILES = CTX_LEN // TM
N_ABLK = T_TOK // ATTN_BLOCK
CTX_ABLK = CTX_LEN // ATTN_BLOCK
PAIRS_PER_GROUP = 6
N_CLASSES = N_EXPERT_GROUPS * PAIRS_PER_GROUP
CLASS_ROWS = 32
MOE_BLOCK = 256
N_STAGING = 2
N_MOE_BLOCKS = T_TOK // MOE_BLOCK + N_CLASSES
P_ROWS = N_MOE_BLOCKS * MOE_BLOCK
XE_W = D_MODEL // 2 + LANES
SCAT_ROWS = 5 * TM
ROW_UNROLL = 64
NEG_BIG = -1e30
VMEM_LIMIT = 52 * 1024 * 1024


def _cparams(sem="arbitrary"):
    return pltpu.CompilerParams(dimension_semantics=(sem,), vmem_limit_bytes=VMEM_LIMIT)


def _mod_row(ref):
    is_ctx = pl.program_id(0) < CTX_TILES
    return jnp.where(is_ctx, ref[1:2, :], ref[0:1, :])


def _layer_norm(x, g, b):
    mu = jnp.mean(x, axis=-1, keepdims=True)
    xc = x - mu
    var = jnp.mean(xc * xc, axis=-1, keepdims=True)
    return xc * lax.rsqrt(var + LN_EPS) * g + b


def _sigmoid(x):
    return 1.0 / (1.0 + jnp.exp(-x))


def _mod_kernel(cc_ref, w_ref, b_ref, o_ref):
    cc = cc_ref[...]
    act = cc * _sigmoid(cc)
    o_ref[...] = jnp.dot(act.astype(BF16), w_ref[...].astype(BF16), preferred_element_type=F32) + b_ref[...]


def _modulation(cc, w_mod, b_mod):
    nt = 1536
    return pl.pallas_call(
        _mod_kernel,
        grid=(DEPTH, 6 * D_MODEL // nt),
        in_specs=[
            pl.BlockSpec((8, D_MODEL), lambda l, n: (0, 0)),
            pl.BlockSpec((None, D_MODEL, nt), lambda l, n: (l, 0, n)),
            pl.BlockSpec((None, 1, nt), lambda l, n: (l, 0, n)),
        ],
        out_specs=pl.BlockSpec((None, 8, nt), lambda l, n: (l, 0, n)),
        out_shape=jax.ShapeDtypeStruct((DEPTH, 8, 6 * D_MODEL), F32),
        compiler_params=pltpu.CompilerParams(
            dimension_semantics=("arbitrary", "arbitrary"), vmem_limit_bytes=VMEM_LIMIT),
        name="modulation",
    )(cc, w_mod, b_mod.reshape(DEPTH, 1, 6 * D_MODEL))


def _mod_spec(layer, chunk):
    return pl.BlockSpec((None, 8, D_MODEL), lambda i: (layer, 0, chunk))


def _store_split_heads(ref, tile, pair):
    low = lax.broadcasted_iota(jnp.int32, tile.shape, 1) < HEAD_DIM
    swapped = pltpu.roll(tile, HEAD_DIM, 1)
    parts = ((jnp.where(low, tile, 0.0), jnp.where(low, 0.0, swapped)),
             (jnp.where(low, swapped, 0.0), jnp.where(low, 0.0, tile)))
    blk = ATTN_BLOCK
    for h, (in_low, in_high) in enumerate(parts):
        cols = slice((2 * pair + h) * LANES, (2 * pair + h + 1) * LANES)
        for n in range(TM // blk):
            rows = slice(n * blk, (n + 1) * blk)
            ref[2 * n * blk:(2 * n + 1) * blk, cols] = in_low[rows].astype(BF16)
            ref[(2 * n + 1) * blk:(2 * n + 2) * blk, cols] = in_high[rows].astype(BF16)


def _token_operands(tokens):
    if isinstance(tokens, tuple):
        specs = [pl.BlockSpec((TM, D_MODEL), lambda i: (jnp.minimum(i, CTX_TILES - 1), 0)),
                 pl.BlockSpec((TM, D_MODEL), lambda i: (jnp.maximum(i - CTX_TILES, 0), 0))]
        return list(tokens), specs
    return [tokens], [pl.BlockSpec((TM, D_MODEL), lambda i: (i, 0))]


def _token_tile(tok_refs):
    if len(tok_refs) == 2:
        return jnp.where(pl.program_id(0) < CTX_TILES, tok_refs[0][...], tok_refs[1][...])
    return tok_refs[0][...]


def _qkv_kernel(*refs, n_tok):
    tok_refs = refs[:n_tok]
    sh_ref, sc_ref, w_ref, cos_ref, sa_ref, sb_ref, q_ref, k_ref, v_ref = refs[n_tok:]
    h = (_token_tile(tok_refs) * (1.0 + _mod_row(sc_ref)) + _mod_row(sh_ref)).astype(BF16)
    y = jnp.dot(h, w_ref[...], preferred_element_type=F32)
    cos, sa, sb = cos_ref[...], sa_ref[...], sb_ref[...]
    n_q = D_MODEL // LANES
    n_k = N_KV_HEADS * HEAD_DIM // LANES
    for t in range(n_q + n_k):
        yt = y[:, t * LANES:(t + 1) * LANES]
        r = yt * cos + pltpu.roll(yt, LANES - ROPE_FREQS, 1) * sa + pltpu.roll(yt, ROPE_FREQS, 1) * sb
        if t < n_q:
            q_ref[:, t * LANES:(t + 1) * LANES] = (r * ATTN_SCALE).astype(BF16)
        else:
            _store_split_heads(k_ref, r, t - n_q)
    for t in range(n_k):
        _store_split_heads(v_ref, y[:, (n_q + n_k + t) * LANES:(n_q + n_k + t + 1) * LANES], t)


def _qkv_project(tokens, mods, layer, w, cos, sa, sb):
    kvw = N_KV_HEADS * LANES
    row = lambda i: (i, 0)
    tok_args, tok_specs = _token_operands(tokens)
    return pl.pallas_call(
        functools.partial(_qkv_kernel, n_tok=len(tok_args)),
        grid=(N_TILES,),
        in_specs=tok_specs + [
            _mod_spec(layer, 0), _mod_spec(layer, 1),
            pl.BlockSpec(w.shape, lambda i: (0, 0)),
            pl.BlockSpec((TM, LANES), row), pl.BlockSpec((TM, LANES), row), pl.BlockSpec((TM, LANES), row),
        ],
        out_specs=[pl.BlockSpec((TM, D_MODEL), row), pl.BlockSpec((2 * TM, kvw), row),
                   pl.BlockSpec((2 * TM, kvw), row)],
        out_shape=[jax.ShapeDtypeStruct((T_TOK, D_MODEL), BF16),
                   jax.ShapeDtypeStruct((2 * T_TOK, kvw), BF16),
                   jax.ShapeDtypeStruct((2 * T_TOK, kvw), BF16)],
        compiler_params=_cparams("parallel"),
        name="qkv_rope",
    )(*tok_args, mods, mods, w, cos, sa, sb)


def _attn_kernel(sink_ref, q_ref, kp_ref, km_ref, kn_ref, kx_ref, vp_ref, vm_ref, vn_ref, vx_ref, o_ref):
    blk = ATTN_BLOCK
    r = lax.broadcasted_iota(jnp.int32, (2 * blk, 2 * blk), 0) & (blk - 1)
    c = lax.broadcasted_iota(jnp.int32, (2 * blk, 2 * blk), 1) & (blk - 1)
    top_rows = lax.broadcasted_iota(jnp.int32, (2 * blk, 1), 0) < blk
    even_cols = lax.broadcasted_iota(jnp.int32, (2 * blk, 2 * blk), 1) < blk
    even_lanes = lax.broadcasted_iota(jnp.int32, (2 * blk, LANES), 1) < HEAD_DIM
    nt_dims = (((1,), (1,)), ((), ()))
    t_rows = lax.broadcasted_iota(jnp.int32, (2 * CTX_LEN, LANES), 0)
    t_cols = lax.broadcasted_iota(jnp.int32, (2 * CTX_LEN, LANES), 1)
    tally = jnp.where(t_cols == ((t_rows >> (blk.bit_length() - 1)) & 1), 1.0, 0.0).astype(BF16)

    def halves(s):
        return s[:, :blk], s[:, blk:]

    windows = ((kp_ref, vp_ref, 0), (km_ref, vm_ref, 0), (km_ref, vm_ref, 2 * blk), (kn_ref, vn_ref, 0))
    for sub, j in [(sub, j) for sub in range(2) for j in range(N_KV_HEADS)]:
        i = 2 * pl.program_id(0) + sub
        prev_ok = i >= CTX_ABLK + 1
        cur_ok = i >= CTX_ABLK
        next_ok = jnp.logical_and(i >= CTX_ABLK, i <= N_ABLK - 2)
        if j == 0:
            b_prev = jnp.where(jnp.logical_and(c >= r, prev_ok), 0.0, NEG_BIG)
            b_cur = jnp.where(cur_ok, 0.0, NEG_BIG)
            b_next = jnp.where(jnp.logical_and(c <= r, next_ok), 0.0, NEG_BIG)
        q_rows = slice(sub * blk, (sub + 1) * blk)
        o_rows = q_rows
        (kp, vp, op), (kc, vc, oc), (kn, vn, on) = windows[sub:sub + 3]
        ks = slice(j * LANES, (j + 1) * LANES)
        k_blk = lambda ref, off: ref[off:off + 2 * blk, ks]
        lhs = jnp.concatenate([q_ref[q_rows, 2 * j * LANES:(2 * j + 1) * LANES],
                               q_ref[q_rows, (2 * j + 1) * LANES:(2 * j + 2) * LANES]], axis=0)
        s_p = lax.dot_general(lhs, k_blk(kp, op), nt_dims, preferred_element_type=F32) + b_prev
        s_c = lax.dot_general(lhs, k_blk(kc, oc), nt_dims, preferred_element_type=F32) + b_cur
        s_n = lax.dot_general(lhs, k_blk(kn, on), nt_dims, preferred_element_type=F32) + b_next
        s_x = lax.dot_general(lhs, kx_ref[:, ks], nt_dims, preferred_element_type=F32)
        s_x0, s_x1 = s_x[:, :2 * blk], s_x[:, 2 * blk:]
        pieces = (s_p, s_c, s_n, s_x0, s_x1)
        sink_e = jnp.where(top_rows, sink_ref[4 * j], sink_ref[4 * j + 2])
        sink_o = jnp.where(top_rows, sink_ref[4 * j + 1], sink_ref[4 * j + 3])
        tile_max = functools.reduce(jnp.maximum, pieces)
        tm_e, tm_o = halves(tile_max)
        m_e = jnp.maximum(jnp.max(tm_e, axis=1, keepdims=True), sink_e)
        m_o = jnp.maximum(jnp.max(tm_o, axis=1, keepdims=True), sink_o)
        m = jnp.where(even_cols, m_e, m_o)
        p_p, p_c, p_n, p_x0, p_x1 = [jnp.exp(s - m).astype(BF16) for s in pieces]
        p_x = jnp.concatenate([p_x0, p_x1], axis=1)
        with_tally = lambda v: jnp.concatenate([v, tally[:v.shape[0]]], axis=1)
        o = (jnp.dot(p_p, with_tally(k_blk(vp, op)), preferred_element_type=F32)
             + jnp.dot(p_c, with_tally(k_blk(vc, oc)), preferred_element_type=F32)
             + jnp.dot(p_n, with_tally(k_blk(vn, on)), preferred_element_type=F32)
             + jnp.dot(p_x, with_tally(vx_ref[:, ks]), preferred_element_type=F32))
        l_e = o[:, LANES:LANES + 1] + jnp.exp(sink_e - m_e)
        l_o = o[:, LANES + 1:LANES + 2] + jnp.exp(sink_o - m_o)
        o = (o[:, :LANES] / jnp.where(even_lanes, l_e, l_o)).astype(BF16)
        o_ref[o_rows, 2 * j * LANES:(2 * j + 1) * LANES] = o[:blk]
        o_ref[o_rows, (2 * j + 1) * LANES:(2 * j + 2) * LANES] = o[blk:]


def _attention(sink, q, kb, vb):
    kvw = N_KV_HEADS * LANES
    blk = ATTN_BLOCK
    lo, hi = CTX_ABLK, N_ABLK - 1
    prev_map = lambda s: (jnp.clip(2 * s - 1, lo, hi), 0)
    pair_map = lambda s: (s, 0)
    next_map = lambda s: (jnp.clip(2 * s + 2, lo, hi), 0)
    ctx_map = lambda s: (0, 0)
    kv_specs = [pl.BlockSpec((2 * blk, kvw), prev_map), pl.BlockSpec((4 * blk, kvw), pair_map),
                pl.BlockSpec((2 * blk, kvw), next_map), pl.BlockSpec((2 * CTX_LEN, kvw), ctx_map)]
    return pl.pallas_call(
        _attn_kernel,
        grid=(N_ABLK // 2,),
        in_specs=[pl.BlockSpec(memory_space=pltpu.SMEM), pl.BlockSpec((2 * blk, D_MODEL), pair_map)]
        + kv_specs + kv_specs,
        out_specs=pl.BlockSpec((2 * blk, D_MODEL), pair_map),
        out_shape=jax.ShapeDtypeStruct((T_TOK, D_MODEL), BF16),
        compiler_params=_cparams("parallel"),
        name="window_attention",
    )(sink, q, kb, kb, kb, kb, vb, vb, vb, vb)


def _gmlp_kernel(x_ref, sh_ref, sc_ref, w_ref, b_ref, g_ref, be_ref, ws_ref, bs_ref, o_ref):
    h = (x_ref[...] * (1.0 + _mod_row(sc_ref)) + _mod_row(sh_ref)).astype(BF16)
    z = jax.nn.gelu(jnp.dot(h, w_ref[...], preferred_element_type=F32) + b_ref[...], approximate=True)
    u = z[:, :D_MODEL]
    v = _layer_norm(z[:, D_MODEL:], g_ref[...], be_ref[...]).astype(BF16)
    gd = D_MODEL // GMLP_GROUPS
    for n in range(TM // CHUNK):
        rows = slice(n * CHUNK, (n + 1) * CHUNK)
        for g in range(GMLP_GROUPS):
            cols = slice(g * gd, (g + 1) * gd)
            s = jnp.dot(ws_ref[g], v[rows, cols], preferred_element_type=F32) + bs_ref[:, cols]
            o_ref[rows, cols] = (u[rows, cols] * s).astype(BF16)


def _gmlp_mixer(x, mods, layer, w_in, b_in, ln_g, ln_b, w_s, b_s):
    row = lambda i: (i, 0)
    fix2 = lambda i: (0, 0)
    return pl.pallas_call(
        _gmlp_kernel,
        grid=(N_TILES,),
        in_specs=[
            pl.BlockSpec((TM, D_MODEL), row),
            _mod_spec(layer, 0), _mod_spec(layer, 1),
            pl.BlockSpec((D_MODEL, 2 * D_MODEL), fix2),
            pl.BlockSpec((1, 2 * D_MODEL), fix2),
            pl.BlockSpec((1, D_MODEL), fix2), pl.BlockSpec((1, D_MODEL), fix2),
            pl.BlockSpec((GMLP_GROUPS, CHUNK, CHUNK), lambda i: (0, 0, 0)),
            pl.BlockSpec((CHUNK, D_MODEL), fix2),
        ],
        out_specs=pl.BlockSpec((TM, D_MODEL), row),
        out_shape=jax.ShapeDtypeStruct((T_TOK, D_MODEL), BF16),
        compiler_params=_cparams("parallel"),
        name="gmlp_mixer",
    )(x, mods, mods, w_in, b_in, ln_g, ln_b, w_s, b_s)


def _route(h2, rw_ref, rb_ref, tri_ref, carry_ref, xe_ref, meta_ref, cnt_ref):
    logits = jnp.dot(h2.astype(BF16), rw_ref[...], preferred_element_type=F32)
    s = _sigmoid(jnp.transpose(logits)[:N_EXPERTS, :])
    ssel = s + rb_ref[...]
    row = lambda v, k: v[k:k + 1, :]
    npg = EXPERTS_PER_GROUP

    def first_argmax(vals):
        best, idx = vals[0], jnp.zeros_like(vals[0])
        for k in range(1, len(vals)):
            better = vals[k] > best
            best = jnp.where(better, vals[k], best)
            idx = jnp.where(better, float(k), idx)
        return best, idx

    def pick(vals, idx):
        out = vals[-1]
        for k in range(len(vals) - 2, -1, -1):
            out = jnp.where(idx == float(k), vals[k], out)
        return out

    group_scores = []
    for g in range(N_EXPERT_GROUPS):
        a, b, c, d = (row(ssel, npg * g + k) for k in range(npg))
        group_scores.append(jnp.maximum(jnp.maximum(jnp.maximum(a + b, a + c), jnp.maximum(a + d, b + c)),
                                        jnp.maximum(b + d, c + d)))
    _, gsel = first_argmax(group_scores)
    biased = [pick([row(ssel, npg * g + k) for g in range(N_EXPERT_GROUPS)], gsel) for k in range(npg)]
    plain = [pick([row(s, npg * g + k) for g in range(N_EXPERT_GROUPS)], gsel) for k in range(npg)]
    _, i1 = first_argmax(biased)
    _, i2 = first_argmax([jnp.where(i1 == float(k), -jnp.inf, biased[k]) for k in range(npg)])
    lo = jnp.minimum(i1, i2)
    hi = jnp.maximum(i1, i2)
    s_lo = pick(plain, lo)
    s_hi = pick(plain, hi)
    den = s_lo + s_hi
    g_lo = s_lo / den
    g_hi = s_hi / den
    pair = lo * (7.0 - lo) * 0.5 + hi - lo - 1.0
    cls = gsel * PAIRS_PER_GROUP + pair
    cls_rows = lax.broadcasted_iota(jnp.int32, (CLASS_ROWS, TM), 0).astype(F32)
    onehot = cls_rows == cls
    oh_f = jnp.where(onehot, 1.0, 0.0)
    before = jnp.dot(oh_f.astype(BF16), tri_ref[...], preferred_element_type=F32)
    carry = carry_ref[...]
    rank = jnp.sum(jnp.where(onehot, before + carry[:, 0:1], 0.0), axis=0, keepdims=True)
    carry = carry + jnp.sum(oh_f, axis=1, keepdims=True)
    carry_ref[...] = carry
    cnt_ref[...] = carry
    as_bits = lambda v: lax.bitcast_convert_type(v.astype(BF16).astype(F32), jnp.uint32)
    half = D_MODEL // 2
    xe_ref[:, :half] = as_bits(h2[:, :half]) | lax.shift_right_logical(as_bits(h2[:, half:]), jnp.uint32(16))
    gate_rows = lax.broadcasted_iota(jnp.int32, (LANES, TM), 0) < LANES // 2
    xe_ref[:, half:] = lax.bitcast_convert_type(jnp.transpose(jnp.where(gate_rows, g_lo, g_hi)), jnp.uint32)
    meta_rows = lax.broadcasted_iota(jnp.int32, (8, TM), 0)
    meta_ref[...] = jnp.where(meta_rows == 0, cls, jnp.where(meta_rows == 1, rank, 0.0)).astype(jnp.int32)


def _gated_conv(x, xp_ref, xn_ref, sha_ref, sca_ref, win_ref, wc_ref):
    i = pl.program_id(0)
    halo = jnp.concatenate([x, xp_ref[...], xn_ref[...]], axis=0)
    h = (halo * (1.0 + _mod_row(sca_ref)) + _mod_row(sha_ref)).astype(BF16)
    proj = jnp.dot(h, win_ref[...], preferred_element_type=F32)
    gate = proj[:TM, :D_MODEL]
    zz_all = proj[:, D_MODEL:2 * D_MODEL] * proj[:, 2 * D_MODEL:]
    zz = zz_all[:TM]
    prev_ok = i >= CTX_TILES + 1
    next_ok = jnp.logical_and(i >= CTX_TILES, i <= N_TILES - 2)
    z_before = jnp.where(prev_ok, zz_all[TM + 7:TM + 8], 0.0)
    z_after = jnp.where(next_ok, zz_all[TM + 8:TM + 9], 0.0)
    rows = lax.broadcasted_iota(jnp.int32, (TM, 1), 0)
    zm1 = jnp.where(rows == 0, z_before, pltpu.roll(zz, 1, 0))
    zp1 = jnp.where(rows == TM - 1, z_after, pltpu.roll(zz, TM - 1, 0))
    zc = wc_ref[0:1, :] * zm1 + wc_ref[1:2, :] * zz + wc_ref[2:3, :] * zp1
    return (gate * zc).astype(BF16)


def _post_kernel(*refs, conv, n_tok):
    tok_refs, refs = refs[:n_tok], refs[n_tok:]
    if conv:
        (xp_ref, xn_ref, sha_ref, sca_ref, win_ref, wc_ref, *refs) = refs
    else:
        (a_ref, *refs) = refs
    (w_ref, ga_ref, shf_ref, scf_ref, lg_ref, lb_ref, rw_ref, rb_ref,
     x1_ref, xe_ref, meta_ref, cnt_ref, tri_ref, carry_ref) = refs

    @pl.when(pl.program_id(0) == 0)
    def _():
        carry_ref[...] = jnp.zeros_like(carry_ref)
        rr = lax.broadcasted_iota(jnp.int32, (TM, TM), 0)
        cc = lax.broadcasted_iota(jnp.int32, (TM, TM), 1)
        tri_ref[...] = jnp.where(rr < cc, 1.0, 0.0).astype(BF16)

    x = _token_tile(tok_refs)
    if conv:
        a = _gated_conv(x, xp_ref, xn_ref, sha_ref, sca_ref, win_ref, wc_ref)
    else:
        a = a_ref[...]
    y = jnp.dot(a, w_ref[...], preferred_element_type=F32)
    x1 = _layer_norm(ALPHA * x + _mod_row(ga_ref) * y, lg_ref[...], lb_ref[...])
    x1_ref[...] = x1
    h2 = x1 * (1.0 + _mod_row(scf_ref)) + _mod_row(shf_ref)
    _route(h2, rw_ref, rb_ref, tri_ref, carry_ref, xe_ref, meta_ref, cnt_ref)


def _post(mix, w_out, tokens, mods, layer, ln_g, ln_b, rw, rb, conv):
    row = lambda i: (i, 0)
    fix2 = lambda i: (0, 0)
    tile = pl.BlockSpec((TM, D_MODEL), row)
    tok_args, tok_specs = _token_operands(tokens)
    if conv:
        assert len(tok_args) == 1
        w_in, wc = mix
        sub = TM // 8
        mix_args = (tok_args[0], tok_args[0], mods, mods, w_in, wc)
        mix_specs = [pl.BlockSpec((8, D_MODEL), lambda i: (jnp.maximum(i * sub - 1, 0), 0)),
                     pl.BlockSpec((8, D_MODEL), lambda i: (jnp.minimum((i + 1) * sub, T_TOK // 8 - 1), 0)),
                     _mod_spec(layer, 0), _mod_spec(layer, 1),
                     pl.BlockSpec(w_in.shape, fix2), pl.BlockSpec((8, D_MODEL), fix2)]
    else:
        mix_args = (mix,)
        mix_specs = [tile]
    vec = pl.BlockSpec((1, D_MODEL), fix2)
    return pl.pallas_call(
        functools.partial(_post_kernel, conv=conv, n_tok=len(tok_args)),
        grid=(N_TILES,),
        in_specs=tok_specs + mix_specs + [
            pl.BlockSpec((D_MODEL, D_MODEL), fix2),
            _mod_spec(layer, 2), _mod_spec(layer, 3), _mod_spec(layer, 4),
            vec, vec,
            pl.BlockSpec((D_MODEL, LANES), fix2), pl.BlockSpec((N_EXPERTS, TM), fix2),
        ],
        out_specs=[tile, pl.BlockSpec((TM, XE_W), row), pl.BlockSpec((None, 8, TM), lambda i: (i, 0, 0)),
                   pl.BlockSpec((CLASS_ROWS, LANES), fix2)],
        out_shape=[jax.ShapeDtypeStruct((T_TOK, D_MODEL), F32),
                   jax.ShapeDtypeStruct((T_TOK, XE_W), jnp.uint32),
                   jax.ShapeDtypeStruct((N_TILES, 8, TM), jnp.int32),
                   jax.ShapeDtypeStruct((CLASS_ROWS, LANES), F32)],
        scratch_shapes=[pltpu.VMEM((TM, TM), BF16), pltpu.VMEM((CLASS_ROWS, LANES), F32)],
        compiler_params=_cparams("arbitrary"),
        name="post_conv" if conv else "post",
    )(*tok_args, *mix_args, w_out, mods, mods, mods, ln_g, ln_b, rw, rb)


def _scatter_kernel(pstart_ref, cnt_ref, dest_ref, xe_ref, xs_ref, zero_ref, sem):
    def row_copy(src, d):
        return pltpu.make_async_copy(src, xs_ref.at[pl.ds(d, 1)], sem)

    first_row = pl.program_id(0) * SCAT_ROWS

    def issue(jo, carry):
        base = pl.multiple_of(jo * ROW_UNROLL, ROW_UNROLL)
        group = xe_ref.at[pl.ds(base, ROW_UNROLL)]
        for k in range(ROW_UNROLL):
            row_copy(group.at[pl.ds(k, 1)], dest_ref[first_row + base + k]).start()
        return carry

    lax.fori_loop(0, SCAT_ROWS // ROW_UNROLL, issue, 0)

    def drain(j, carry):
        row_copy(xe_ref.at[pl.ds(0, 1)], 0).wait()
        return carry

    lax.fori_loop(0, SCAT_ROWS, drain, 0, unroll=8)

    @pl.when(pl.program_id(0) == pl.num_programs(0) - 1)
    def _():
        zero_ref[...] = jnp.zeros_like(zero_ref)

        def per_class(c, carry):
            n = cnt_ref[c]
            first = pstart_ref[c] + n
            last = pstart_ref[c] + jnp.bitwise_and(n + (MOE_BLOCK - 1), -MOE_BLOCK)
            aligned = jnp.minimum(jnp.bitwise_and(first + 7, -8), last)

            def group_copy(k):
                row = pl.multiple_of(aligned + 8 * k, 8)
                return pltpu.make_async_copy(zero_ref.at[pl.ds(0, 8)], xs_ref.at[pl.ds(row, 8)], sem)

            def fill(k, cc):
                row_copy(zero_ref.at[pl.ds(0, 1)], first + k).start()
                return cc

            def fill_wait(k, cc):
                row_copy(zero_ref.at[pl.ds(0, 1)], 0).wait()
                return cc

            def fill_group(k, cc):
                group_copy(k).start()
                return cc

            def fill_group_wait(k, cc):
                group_copy(k).wait()
                return cc

            n_groups = lax.shift_right_logical(last - aligned, 3)
            lax.fori_loop(0, aligned - first, fill, 0)
            lax.fori_loop(0, n_groups, fill_group, 0)
            lax.fori_loop(0, aligned - first, fill_wait, 0)
            lax.fori_loop(0, n_groups, fill_group_wait, 0)
            return carry

        lax.fori_loop(0, N_CLASSES, per_class, 0)

        last = N_CLASSES - 1
        rows_used = pstart_ref[last] + jnp.bitwise_and(cnt_ref[last] + (MOE_BLOCK - 1), -MOE_BLOCK)
        first_free = lax.shift_right_logical(rows_used, MOE_BLOCK.bit_length() - 1)

        def block_copy(blk):
            return pltpu.make_async_copy(zero_ref, xs_ref.at[pl.ds(blk * MOE_BLOCK, MOE_BLOCK)], sem)

        def fill_block(blk, cc):
            block_copy(blk).start()
            return cc

        def fill_block_wait(blk, cc):
            block_copy(0).wait()
            return cc

        lax.fori_loop(first_free, N_MOE_BLOCKS, fill_block, 0)
        lax.fori_loop(first_free, N_MOE_BLOCKS, fill_block_wait, 0)


def _scatter_rows(pstart, cnt, dest, xe):
    grid_spec = pltpu.PrefetchScalarGridSpec(
        num_scalar_prefetch=2,
        grid=(T_TOK // SCAT_ROWS,),
        in_specs=[pl.BlockSpec((T_TOK,), lambda i, ps, cn: (0,), memory_space=pltpu.SMEM),
                  pl.BlockSpec((SCAT_ROWS, XE_W), lambda i, ps, cn: (i, 0))],
        out_specs=pl.BlockSpec(memory_space=pl.ANY),
        scratch_shapes=[pltpu.VMEM((MOE_BLOCK, XE_W), xe.dtype), pltpu.SemaphoreType.DMA(())],
    )
    return pl.pallas_call(
        _scatter_kernel,
        grid_spec=grid_spec,
        out_shape=jax.ShapeDtypeStruct((P_ROWS, XE_W), xe.dtype),
        compiler_params=_cparams("arbitrary"),
        name="scatter_rows",
    )(pstart, cnt, dest.reshape(T_TOK), xe)


def _expert_kernel(lo_ref, hi_ref, need_ref, half_ref, nused_ref, x_ref, w1_hbm, w3_hbm, w2_hbm, o_ref,
                   st1, st3, st2, w1s, w3s, w2s, sems, done_ref, *, layer):
    b = pl.program_id(0)

    def fetch(k, buf):
        return (pltpu.make_async_copy(w1_hbm.at[layer, k], st1.at[buf], sems.at[buf]),
                pltpu.make_async_copy(w3_hbm.at[layer, k], st3.at[buf], sems.at[buf]),
                pltpu.make_async_copy(w2_hbm.at[layer, k], st2.at[buf], sems.at[buf]))

    @pl.when(b == 0)
    def _():
        done_ref[0] = 0
        for k in range(N_STAGING):
            for cp in fetch(k, k):
                cp.start()

    used = b < nused_ref[0]

    @pl.when(jnp.logical_not(used))
    def _():
        o_ref[...] = jnp.zeros_like(o_ref)

    @pl.when(used)
    def _():
        def install(k, carry):
            buf = k % N_STAGING
            slot = k % EXPERTS_PER_GROUP
            for cp in fetch(k, buf):
                cp.wait()
            w1s[slot] = st1[buf].astype(BF16)
            w3s[slot] = st3[buf].astype(BF16)
            w2s[slot] = st2[buf].astype(BF16)

            @pl.when(k + N_STAGING < N_EXPERTS)
            def _():
                for cp in fetch(k + N_STAGING, buf):
                    cp.start()

            return carry

        done = done_ref[0]
        lax.fori_loop(done, need_ref[b], install, 0)
        done_ref[0] = jnp.maximum(done, need_ref[b])

        def compute(n_rows):
            half = D_MODEL // 2
            words = x_ref[:n_rows, :half]
            as_f32 = lambda bits: lax.bitcast_convert_type(bits, F32)
            xb = jnp.concatenate([as_f32(words & jnp.uint32(0xFFFF0000)),
                                  as_f32(lax.shift_left(words, jnp.uint32(16)))], axis=1).astype(BF16)
            gates = as_f32(x_ref[:n_rows, half:])

            def expert(slot):
                h1 = jnp.dot(xb, w1s[slot], preferred_element_type=F32)
                h3 = jnp.dot(xb, w3s[slot], preferred_element_type=F32)
                act = (h1 * _sigmoid(h1) * h3).astype(BF16)
                return jnp.dot(act, w2s[slot], preferred_element_type=F32)

            ya = expert(lo_ref[b])
            yb = expert(hi_ref[b])
            o_ref[:n_rows, :] = gates[:, 0:1] * ya + gates[:, LANES // 2:LANES // 2 + 1] * yb

        @pl.when(half_ref[b] == 0)
        def _():
            compute(MOE_BLOCK)

        @pl.when(half_ref[b] != 0)
        def _():
            compute(MOE_BLOCK // 2)
            o_ref[MOE_BLOCK // 2:, :] = jnp.zeros((MOE_BLOCK // 2, D_MODEL), F32)

    @pl.when(b == pl.num_programs(0) - 1)
    def _():
        def drain(k, carry):
            for cp in fetch(k, k % N_STAGING):
                cp.wait()
            return carry

        done = done_ref[0]
        lax.fori_loop(done, jnp.minimum(done + N_STAGING, N_EXPERTS), drain, 0)


def _experts(blocks, nused, xs, w1, w3, w2, layer):
    rows = lambda b, lo, hi, nd, hf, nu: (jnp.maximum(jnp.minimum(b, nu[0] - 1), 0), 0)
    any_spec = pl.BlockSpec(memory_space=pl.ANY)
    up, down = (D_MODEL, D_EXPERT), (D_EXPERT, D_MODEL)
    grid_spec = pltpu.PrefetchScalarGridSpec(
        num_scalar_prefetch=5,
        grid=(N_MOE_BLOCKS,),
        in_specs=[pl.BlockSpec((MOE_BLOCK, XE_W), rows), any_spec, any_spec, any_spec],
        out_specs=pl.BlockSpec((MOE_BLOCK, D_MODEL), lambda b, lo, hi, nd, hf, nu: (b, 0)),
        scratch_shapes=[pltpu.VMEM((N_STAGING,) + up, F32), pltpu.VMEM((N_STAGING,) + up, F32),
                        pltpu.VMEM((N_STAGING,) + down, F32),
                        pltpu.VMEM((EXPERTS_PER_GROUP,) + up, BF16), pltpu.VMEM((EXPERTS_PER_GROUP,) + up, BF16),
                        pltpu.VMEM((EXPERTS_PER_GROUP,) + down, BF16),
                        pltpu.SemaphoreType.DMA((N_STAGING,)), pltpu.SMEM((1,), jnp.int32)],
    )
    return pl.pallas_call(
        functools.partial(_expert_kernel, layer=layer),
        grid_spec=grid_spec,
        out_shape=jax.ShapeDtypeStruct((P_ROWS, D_MODEL), F32),
        compiler_params=_cparams("arbitrary"),
        name="expert_pairs",
    )(*blocks, nused, xs, w1, w3, w2)


def _ln2_kernel(dest_ref, dest_next_ref, ys_ref, x_ref, gf_ref, lg_ref, lb_ref, o_ref, ybuf, sems):
    i = pl.program_id(0)
    slot = i % 2

    def row_copy(src_row, dst_rows, dst_slot):
        return pltpu.make_async_copy(ys_ref.at[pl.ds(src_row, 1)], dst_rows, sems.at[dst_slot])

    def issue_tile(idx_ref, dst_slot):
        def issue(jo, carry):
            base = pl.multiple_of(jo * ROW_UNROLL, ROW_UNROLL)
            group = ybuf.at[dst_slot, pl.ds(base, ROW_UNROLL)]
            for k in range(ROW_UNROLL):
                row_copy(idx_ref[base + k], group.at[pl.ds(k, 1)], dst_slot).start()
            return carry

        lax.fori_loop(0, TM // ROW_UNROLL, issue, 0)

    @pl.when(i == 0)
    def _():
        issue_tile(dest_ref, 0)

    @pl.when(i + 1 < pl.num_programs(0))
    def _():
        issue_tile(dest_next_ref, 1 - slot)

    def drain(j, carry):
        row_copy(0, ybuf.at[slot, pl.ds(0, 1)], slot).wait()
        return carry

    lax.fori_loop(0, TM, drain, 0, unroll=8)
    o_ref[...] = _layer_norm(ALPHA * x_ref[...] + _mod_row(gf_ref) * ybuf[slot], lg_ref[...], lb_ref[...])


def _ln2(x1, ys, dest, mods, layer, ln_g, ln_b, latent_only):
    row = lambda i: (i, 0)
    tile = pl.BlockSpec((TM, D_MODEL), row)
    vec = pl.BlockSpec((1, D_MODEL), lambda i: (0, 0))
    idx = lambda f: pl.BlockSpec((TM,), f, memory_space=pltpu.SMEM)
    if latent_only:
        out_rows = SEQ
        out_spec = pl.BlockSpec((TM, D_MODEL), lambda i: (jnp.maximum(i - CTX_TILES, 0), 0))
    else:
        out_rows = T_TOK
        out_spec = tile
    return pl.pallas_call(
        _ln2_kernel,
        grid=(N_TILES,),
        in_specs=[idx(lambda i: (i,)), idx(lambda i: (jnp.minimum(i + 1, N_TILES - 1),)),
                  pl.BlockSpec(memory_space=pl.ANY), tile, _mod_spec(layer, 5), vec, vec],
        out_specs=out_spec,
        out_shape=jax.ShapeDtypeStruct((out_rows, D_MODEL), F32),
        scratch_shapes=[pltpu.VMEM((2, TM, D_MODEL), F32), pltpu.SemaphoreType.DMA((2,))],
        compiler_params=_cparams("arbitrary"),
        name="moe_gather_residual_ln",
    )(dest.reshape(T_TOK), dest.reshape(T_TOK), ys, x1, mods, ln_g, ln_b)


def _pair_tables():
    pairs = [(a, b) for a in range(EXPERTS_PER_GROUP) for b in range(a + 1, EXPERTS_PER_GROUP)]
    return jnp.array([p[0] for p in pairs], jnp.int32), jnp.array([p[1] for p in pairs], jnp.int32)


def _dispatch_plan(meta, counts):
    cnt = counts[:N_CLASSES, 0].astype(jnp.int32)
    padded = (cnt + MOE_BLOCK - 1) // MOE_BLOCK * MOE_BLOCK
    pad_end = jnp.cumsum(padded)
    pad_start = pad_end - padded
    nused = pad_end[-1:] // MOE_BLOCK
    cls, rank = meta[:, 0:1, :], meta[:, 1:2, :]
    before = (cls[..., None] > jnp.arange(N_CLASSES, dtype=jnp.int32)).astype(jnp.int32)
    dest = rank + jnp.sum(before * padded, axis=-1)
    blk_row = jnp.arange(N_MOE_BLOCKS, dtype=jnp.int32) * MOE_BLOCK
    blk_cls = jnp.minimum(jnp.sum((pad_end[None, :] <= blk_row[:, None]).astype(jnp.int32), axis=1),
                          N_CLASSES - 1)
    pair_lo, pair_hi = _pair_tables()
    lo_slot = pair_lo[blk_cls % PAIRS_PER_GROUP]
    hi_slot = pair_hi[blk_cls % PAIRS_PER_GROUP]
    need = (blk_cls // PAIRS_PER_GROUP) * EXPERTS_PER_GROUP + hi_slot + 1
    rows_left = cnt[blk_cls] - (blk_row - pad_start[blk_cls])
    half_full = (rows_left <= MOE_BLOCK // 2).astype(jnp.int32)
    return pad_start, cnt, dest, (lo_slot, hi_slot, need, half_full), nused


def _rope_tables():
    n_rows = SEQ // GRID_W
    freqs = jnp.power(ROPE_BASE, -jnp.arange(ROPE_FREQS, dtype=F32) / ROPE_FREQS)
    ar = jnp.arange(n_rows).astype(F32)[:, None] * freqs[None, :]
    ac = jnp.arange(GRID_W).astype(F32)[:, None] * freqs[None, :]

    def table(fn):
        by_row = jnp.broadcast_to(fn(ar)[:, None, :], (n_rows, GRID_W, ROPE_FREQS)).reshape(SEQ, ROPE_FREQS)
        by_col = jnp.broadcast_to(fn(ac)[None, :, :], (n_rows, GRID_W, ROPE_FREQS)).reshape(SEQ, ROPE_FREQS)
        return jnp.concatenate([by_row, by_row, by_col, by_col], axis=-1)

    cos, sin = table(jnp.cos), table(jnp.sin)
    cos = jnp.concatenate([jnp.ones((CTX_LEN, HEAD_DIM), F32), cos], axis=0)
    sin = jnp.concatenate([jnp.zeros((CTX_LEN, HEAD_DIM), F32), sin], axis=0)
    first_half = (jnp.arange(HEAD_DIM) % (2 * ROPE_FREQS)) < ROPE_FREQS
    sa = jnp.where(first_half[None, :], -sin, 0.0)
    sb = jnp.where(first_half[None, :], 0.0, sin)
    two = lambda t: jnp.concatenate([t, t], axis=-1)
    return two(cos), two(sa), two(sb)


def kernel(x, c, ctx, c_ctx, w_mod, b_mod, ln1_g, ln1_b, ln2_g, ln2_b, router_w, router_bias, moe_w1, moe_w3, moe_w2, a_w_qkv, a_w_o, a_sink, b_w_in, b_b_in, b_ln_g, b_ln_b, b_w_s, b_b_s, b_w_out, c_w_in, c_w_conv, c_w_out):
    assert x.shape == (1, SEQ, D_MODEL) and ctx.shape == (1, CTX_LEN, D_MODEL)
    tok = (ctx[0], x[0])
    cc = jnp.zeros((8, D_MODEL), F32).at[0].set(c[0]).at[1].set(c_ctx)
    mods = _modulation(cc, w_mod, b_mod)
    cos, sa, sb = _rope_tables()
    rw = jnp.pad(router_w, ((0, 0), (0, LANES - N_EXPERTS))).astype(BF16)
    rb = jnp.broadcast_to(router_bias.astype(F32)[:, None], (N_EXPERTS, TM))

    for i in range(DEPTH):
        kind, j = i % N_MIXERS, i // N_MIXERS
        if kind == 0:
            q, kb, vb = _qkv_project(tok, mods, i, a_w_qkv[j].astype(BF16), cos, sa, sb)
            mix = _attention(a_sink[j], q, kb, vb)
            w_out = a_w_o[j]
        elif kind == 1:
            bs = jnp.repeat(b_b_s[j], D_MODEL // GMLP_GROUPS, axis=1)
            mix = _gmlp_mixer(tok, mods, i, b_w_in[j].astype(BF16), b_b_in[j].reshape(1, -1),
                              b_ln_g[j].reshape(1, -1), b_ln_b[j].reshape(1, -1), b_w_s[j].astype(BF16), bs)
            w_out = b_w_out[j]
        else:
            mix = (c_w_in[j].astype(BF16), jnp.pad(c_w_conv[j], ((0, 5), (0, 0))))
            w_out = c_w_out[j]
        x1, xe, meta, counts = _post(mix, w_out.astype(BF16), tok, mods, i,
                                     ln1_g[i].reshape(1, -1), ln1_b[i].reshape(1, -1), rw, rb, conv=(kind == 2))
        pad_start, cnt, dest, blocks, nused = _dispatch_plan(meta, counts)
        xs = _scatter_rows(pad_start, cnt, dest, xe)
        ys = _experts(blocks, nused, xs, moe_w1, moe_w3, moe_w2, i)
        tok = _ln2(x1, ys, dest, mods, i, ln2_g[i].reshape(1, -1), ln2_b[i].reshape(1, -1),
                   latent_only=(i == DEPTH - 1))
    return tok.reshape(1, SEQ, D_MODEL)
```
